```python
import jax
import jax.numpy as jnp
from jax import lax
import numpy as np

D_MODEL = 2048
BATCH = 2
SEQ = 8192
DEPTH = 1
DEC_BATCH = 8
DEC_SEQ = 32
PAST_LEN = 2048

CHUNK = 64
A_HEADS = 16
A_KV_HEADS = 4
A_GROUP = A_HEADS // A_KV_HEADS
A_HEAD_DIM = 64
A_WIDTH = A_HEADS * A_HEAD_DIM
A_KV_WIDTH = A_KV_HEADS * A_HEAD_DIM
WINDOW = 128
WIN_CHUNKS = WINDOW // CHUNK
B_HEADS = 8
B_KEY_DIM = 128
B_VAL_DIM = 128
B_KEY_WIDTH = B_HEADS * B_KEY_DIM
B_WIDTH = B_HEADS * B_VAL_DIM
B_BLOCK = CHUNK
MEM_LEN = 256
C_HEADS = 4
C_HEAD_DIM = 256
C_WIDTH = C_HEADS * C_HEAD_DIM
N_BRANCH = 3
MIX_WIDTH = A_WIDTH + B_WIDTH + C_WIDTH
IN_SIZES = (A_WIDTH, A_KV_WIDTH, A_KV_WIDTH, B_KEY_WIDTH, B_KEY_WIDTH, B_WIDTH, B_WIDTH, C_WIDTH, N_BRANCH * D_MODEL)
IN_OFFSETS = tuple(sum(IN_SIZES[: i + 1]) for i in range(len(IN_SIZES) - 1))
IN_TOTAL = sum(IN_SIZES)
N_EXPERTS = 64
N_GROUPS = 8
TOPK_GROUPS = 4
TOP_K = 8
EXPERT_DIM = 512
SHARED_DIM = 512
ROUTED_SCALE = 2.5
DISPATCH_BLOCK = 128
DN_ALPHA = (2 * DEPTH) ** 0.25
DN_BETA = (8 * DEPTH) ** -0.25
LN_EPS = 1e-5
RMS_EPS = 1e-6

kernel_name = 'hybrid_stream_encoder_step'


def layer_norm(x, g, b):
    xf = x.astype(jnp.float32)
    mu = jnp.mean(xf, axis=-1, keepdims=True)
    var = jnp.mean(jnp.square(xf - mu), axis=-1, keepdims=True)
    y = (xf - mu) * lax.rsqrt(var + LN_EPS) * g.astype(jnp.float32) + b.astype(jnp.float32)
    return y.astype(x.dtype)


def alibi_slopes():
    return jnp.asarray(2.0 ** (-8.0 * np.arange(1, A_HEADS + 1) / A_HEADS), dtype=jnp.float32)


def split_heads(t, n):
    return t.reshape(t.shape[0], t.shape[1], n, t.shape[2] // n)


def project_in(x, w):
    z = jnp.einsum('btd,df->btf', x, w)
    return jnp.split(z, IN_OFFSETS, axis=-1)


def sink_softmax(s, sink):
    m = jnp.maximum(jnp.max(s, axis=-1, keepdims=True), sink)
    p = jnp.exp(s - m)
    return p / (jnp.sum(p, axis=-1, keepdims=True) + jnp.exp(sink - m))


def swa_prompt(q, k, v, sinks, slopes):
    B, T = q.shape[0], q.shape[1]
    n_c = T // CHUNK
    pad = WIN_CHUNKS * CHUNK
    n_band = pad + CHUNK
    qb = q.reshape(B, n_c, CHUNK, A_KV_HEADS, A_GROUP, A_HEAD_DIM)

    def band(t):
        tp = jnp.pad(t, ((0, 0), (pad, 0), (0, 0), (0, 0))).reshape(B, n_c + WIN_CHUNKS, CHUNK, A_KV_HEADS, A_HEAD_DIM)
        return jnp.concatenate([tp[:, j:j + n_c] for j in range(WIN_CHUNKS + 1)], axis=2)

    kb, vb = band(k), band(v)
    q_pos = jnp.arange(T).reshape(n_c, CHUNK)
    k_pos = jnp.arange(n_c)[:, None] * CHUNK - pad + jnp.arange(n_band)[None, :]
    dist = jnp.abs(q_pos[:, :, None] - k_pos[:, None, :]).astype(jnp.float32)
    s = jnp.einsum('bcqkgd,bcskd->bckgqs', qb, kb).astype(jnp.float32) * (A_HEAD_DIM ** -0.5)
    s = s - slopes.reshape(A_KV_HEADS, A_GROUP)[None, None, :, :, None, None] * dist[None, :, None, None]
    s = jnp.where((k_pos >= 0)[None, :, None, None, None, :], s, -jnp.inf)
    p = sink_softmax(s, sinks.astype(jnp.float32).reshape(A_KV_HEADS, A_GROUP)[None, None, :, :, None, None])
    o = jnp.einsum('bckgqs,bcskd->bcqkgd', p.astype(v.dtype), vb)
    return o.reshape(B, T, A_WIDTH)


def swa_sample(q, k, v, k_cache, v_cache, sinks, slopes):
    B, S = q.shape[0], q.shape[1]
    Lc = k_cache.shape[1]
    k_all = jnp.concatenate([k_cache.astype(k.dtype), k], axis=1)
    v_all = jnp.concatenate([v_cache.astype(v.dtype), v], axis=1)
    q_pos = PAST_LEN + jnp.arange(S)
    k_pos = PAST_LEN - Lc + jnp.arange(Lc + S)
    cdiff = q_pos[:, None] // CHUNK - k_pos[None, :] // CHUNK
    valid = (cdiff >= 0) & (cdiff <= WIN_CHUNKS)
    dist = jnp.abs(q_pos[:, None] - k_pos[None, :]).astype(jnp.float32)
    qg = q.reshape(B, S, A_KV_HEADS, A_GROUP, A_HEAD_DIM)
    s = jnp.einsum('bqkgd,bskd->bkgqs', qg, k_all).astype(jnp.float32) * (A_HEAD_DIM ** -0.5)
    s = s - slopes.reshape(A_KV_HEADS, A_GROUP)[None, :, :, None, None] * dist
    s = jnp.where(valid, s, -jnp.inf)
    p = sink_softmax(s, sinks.astype(jnp.float32).reshape(A_KV_HEADS, A_GROUP)[None, :, :, None, None])
    o = jnp.einsum('bkgqs,bskd->bqkgd', p.astype(v_all.dtype), v_all).reshape(B, S, A_WIDTH)
    return o, k_all[:, -Lc:], v_all[:, -Lc:]


def hgrn_gates(bq, bf, bi, lb):
    B, T = bq.shape[0], bq.shape[1]

    def heads(t):
        return jnp.swapaxes(t.astype(jnp.float32).reshape(B, T, B_HEADS, t.shape[-1] // B_HEADS), 1, 2)

    fl = bf.astype(jnp.float32)
    q = jax.nn.silu(bq.astype(jnp.float32)) * (B_KEY_DIM ** -0.5)
    logf = jnp.logaddexp(jnp.log(lb), jnp.log1p(-lb) + jax.nn.log_sigmoid(fl))
    k = (1.0 - lb) * jax.nn.sigmoid(-fl)
    return heads(q), heads(k), heads(bi), heads(logf)


def hgrn_block(S, q, k, v, logf):
    L = q.shape[2]
    b = jnp.cumsum(logf, axis=2)
    causal = jnp.tril(jnp.ones((L, L), dtype=bool))
    diff = b[:, :, :, None, :] - b[:, :, None, :, :]
    decay = jnp.exp(jnp.where(causal[None, None, :, :, None], diff, -jnp.inf))
    attn = jnp.einsum('bhtk,bhsk,bhtsk->bhts', q, k, decay)
    o = jnp.einsum('bhts,bhsv->bhtv', attn, v) + jnp.einsum('bhtk,bhkv->bhtv', q * jnp.exp(b), S)
    b_last = b[:, :, -1:, :]
    S_new = jnp.exp(b_last)[:, :, 0, :, None] * S + jnp.einsum('bhsk,bhsv->bhkv', k * jnp.exp(b_last - b), v)
    return S_new, o


def hgrn_prompt(q, k, v, logf):
    B, H, T = q.shape[0], q.shape[1], q.shape[2]
    n_b = T // B_BLOCK

    def blocks(t):
        return jnp.moveaxis(t.reshape(B, H, n_b, B_BLOCK, t.shape[-1]), 2, 0)

    S0 = jnp.zeros((B, H, B_KEY_DIM, B_VAL_DIM), jnp.float32)
    S_fin, o = lax.scan(lambda S, blk: hgrn_block(S, *blk), S0, (blocks(q), blocks(k), blocks(v), blocks(logf)))
    return jnp.moveaxis(o, 0, 2).reshape(B, H, T, B_VAL_DIM), S_fin


def hgrn_readout(o, gate, norm_g):
    B, T = o.shape[0], o.shape[2]
    o = jnp.swapaxes(o, 1, 2)
    o = o * lax.rsqrt(jnp.mean(jnp.square(o), axis=-1, keepdims=True) + RMS_EPS) * norm_g.astype(jnp.float32)
    return (o.reshape(B, T, B_WIDTH) * jax.nn.silu(gate.astype(jnp.float32))).astype(gate.dtype)


def memory_kv(mem, w):
    mk, mv = jnp.split(jnp.einsum('bmd,df->bmf', mem, w), 2, axis=-1)
    return split_heads(mk, C_HEADS), split_heads(mv, C_HEADS)


def memory_attend(q, mk, mv):
    B, T = q.shape[0], q.shape[1]
    s = jnp.einsum('bthd,bmhd->bhtm', q, mk).astype(jnp.float32) * (C_HEAD_DIM ** -0.5)
    p = jax.nn.softmax(s, axis=-1)
    return jnp.einsum('bhtm,bmhd->bthd', p.astype(mv.dtype), mv).reshape(B, T, C_WIDTH)


def swiglu(x, wg, wu, wd):
    return (jax.nn.silu(x @ wg) * (x @ wu)) @ wd


def route(x2, w_router, bias):
    T = x2.shape[0]
    scores = jax.nn.sigmoid(x2.astype(jnp.float32) @ w_router.astype(jnp.float32))
    choice = scores + bias.astype(jnp.float32)
    grouped = choice.reshape(T, N_GROUPS, N_EXPERTS // N_GROUPS)
    group_score = jnp.sum(lax.top_k(grouped, 2)[0], axis=-1)
    _, gidx = lax.top_k(group_score, TOPK_GROUPS)
    gsel = jnp.zeros((T, N_GROUPS), dtype=bool).at[jnp.arange(T)[:, None], gidx].set(True)
    masked = jnp.where(jnp.repeat(gsel, N_EXPERTS // N_GROUPS, axis=-1), choice, -jnp.inf)
    _, idx = lax.top_k(masked, TOP_K)
    w = jnp.take_along_axis(scores, idx, axis=-1)
    return idx, w / jnp.sum(w, axis=-1, keepdims=True) * ROUTED_SCALE


def routed_experts(x2, idx, wts, w_gate, w_up, w_down):
    T = x2.shape[0]
    A = T * TOP_K
    M = DISPATCH_BLOCK
    n_blocks = (A + M - 1) // M + N_EXPERTS
    flat_e = idx.reshape(A)
    order = jnp.argsort(flat_e)
    sorted_e = flat_e[order]
    sorted_t = (order // TOP_K).astype(jnp.int32)
    sorted_w = wts.reshape(A)[order]
    counts = jnp.bincount(flat_e, length=N_EXPERTS)
    padded = (counts + M - 1) // M * M
    pad_end = jnp.cumsum(padded)
    pad_start = pad_end - padded
    cnt_start = jnp.cumsum(counts) - counts
    dest = pad_start[sorted_e] + jnp.arange(A) - cnt_start[sorted_e]
    buf_t = jnp.zeros((n_blocks * M,), jnp.int32).at[dest].set(sorted_t)
    buf_w = jnp.zeros((n_blocks * M,), x2.dtype).at[dest].set(sorted_w)
    blk_e = jnp.minimum(jnp.searchsorted(pad_end, jnp.arange(n_blocks) * M, side='right'), N_EXPERTS - 1)

    def expert_block(args):
        t_blk, w_blk, e = args
        return swiglu(x2[t_blk], w_gate[e], w_up[e], w_down[e]) * w_blk[:, None]

    yb = lax.map(expert_block, (buf_t.reshape(n_blocks, M), buf_w.reshape(n_blocks, M), blk_e))
    return jax.ops.segment_sum(yb.reshape(n_blocks * M, D_MODEL), buf_t, num_segments=T)


def merge_and_channel_mix(x, a_out, b_out, c_out, gate_logits, w_branch, w_out, ln1_g, ln1_b, w_router, router_bias,
                          w_eg, w_eu, w_ed, w_sg, w_su, w_sd, ln2_g, ln2_b):
    B, T = x.shape[0], x.shape[1]
    g = jax.nn.sigmoid(gate_logits).reshape(B, T, N_BRANCH, D_MODEL)
    p_a, p_b, p_c = jnp.split(w_branch, (A_WIDTH, A_WIDTH + B_WIDTH), axis=0)
    h = g[:, :, 0] * (a_out @ p_a) + g[:, :, 1] * (b_out @ p_b) + g[:, :, 2] * (c_out @ p_c)
    x1 = layer_norm(DN_ALPHA * x + h @ w_out, ln1_g, ln1_b)
    x2 = x1.reshape(B * T, D_MODEL)
    idx, wts = route(x2, w_router, router_bias)
    y = routed_experts(x2, idx, wts.astype(x2.dtype), w_eg, w_eu, w_ed) + swiglu(x2, w_sg, w_su, w_sd)
    return layer_norm(DN_ALPHA * x1 + y.reshape(B, T, D_MODEL), ln2_g, ln2_b)


def setup_inputs(seed: int = 0) -> dict:
    key = jax.random.key(seed)
    ks = jax.random.split(key, 27)

    def n(k, shape, s):
        return jax.random.normal(k, shape, jnp.float32) * s

    win = min(WINDOW, PAST_LEN)
    return {
        'x_prompt': n(ks[0], (BATCH, SEQ, D_MODEL), 1.0),
        'x_sample': n(ks[1], (DEC_BATCH, DEC_SEQ, D_MODEL), 1.0),
        'cache_win_k': n(ks[2], (DEPTH, DEC_BATCH, win, A_KV_HEADS, A_HEAD_DIM), 1.0),
        'cache_win_v': n(ks[3], (DEPTH, DEC_BATCH, win, A_KV_HEADS, A_HEAD_DIM), 1.0),
        'state_hgrn': n(ks[4], (DEPTH, DEC_BATCH, B_HEADS, B_KEY_DIM, B_VAL_DIM), 0.5),
        'cache_mem_k': n(ks[5], (DEPTH, DEC_BATCH, MEM_LEN, C_HEADS, C_HEAD_DIM), 1.0),
        'cache_mem_v': n(ks[6], (DEPTH, DEC_BATCH, MEM_LEN, C_HEADS, C_HEAD_DIM), 1.0),
        'mem_prompt': n(ks[7], (BATCH, MEM_LEN, D_MODEL), 1.0),
        'w_in': n(ks[8], (DEPTH, D_MODEL, IN_TOTAL), D_MODEL ** -0.5),
        'w_mem_kv': n(ks[9], (DEPTH, D_MODEL, 2 * C_WIDTH), D_MODEL ** -0.5),
        'a_sinks': n(ks[10], (DEPTH, A_HEADS), 0.5),
        'b_lb_logits': n(ks[11], (DEPTH + 1, B_KEY_WIDTH), 0.1),
        'b_norm_g': 1.0 + n(ks[12], (DEPTH, B_VAL_DIM), 0.02),
        'w_branch': n(ks[13], (DEPTH, MIX_WIDTH, D_MODEL), (A_WIDTH ** -0.5) * DN_BETA),
        'w_out': n(ks[14], (DEPTH, D_MODEL, D_MODEL), (D_MODEL ** -0.5) * DN_BETA),
        'ln1_g': 1.0 + n(ks[15], (DEPTH, D_MODEL), 0.02),
        'ln1_b': n(ks[16], (DEPTH, D_MODEL), 0.02),
        'w_router': n(ks[17], (DEPTH, D_MODEL, N_EXPERTS), D_MODEL ** -0.5),
        'router_bias': n(ks[18], (DEPTH, N_EXPERTS), 0.01),
        'w_exp_gate': n(ks[19], (DEPTH, N_EXPERTS, D_MODEL, EXPERT_DIM), D_MODEL ** -0.5),
        'w_exp_up': n(ks[20], (DEPTH, N_EXPERTS, D_MODEL, EXPERT_DIM), D_MODEL ** -0.5),
        'w_exp_down': n(ks[21], (DEPTH, N_EXPERTS, EXPERT_DIM, D_MODEL), (EXPERT_DIM ** -0.5) * DN_BETA),
        'w_sh_gate': n(ks[22], (DEPTH, D_MODEL, SHARED_DIM), D_MODEL ** -0.5),
        'w_sh_up': n(ks[23], (DEPTH, D_MODEL, SHARED_DIM), D_MODEL ** -0.5),
        'w_sh_down': n(ks[24], (DEPTH, SHARED_DIM, D_MODEL), (SHARED_DIM ** -0.5) * DN_BETA),
        'ln2_g': 1.0 + n(ks[25], (DEPTH, D_MODEL), 0.02),
        'ln2_b': n(ks[26], (DEPTH, D_MODEL), 0.02),
    }


def reference(x_prompt, x_sample, cache_win_k, cache_win_v, state_hgrn, cache_mem_k, cache_mem_v, mem_prompt,
              w_in, w_mem_kv, a_sinks, b_lb_logits, b_norm_g, w_branch, w_out, ln1_g, ln1_b,
              w_router, router_bias, w_exp_gate, w_exp_up, w_exp_down, w_sh_gate, w_sh_up, w_sh_down, ln2_g, ln2_b):
    slopes = alibi_slopes()
    lower_bounds = jnp.cumsum(jax.nn.softmax(b_lb_logits.astype(jnp.float32), axis=0), axis=0)
    xp, xs = x_prompt, x_sample
    wk_p, wv_p, hs_p, mk_p, mv_p, wk_s, wv_s, hs_s = [], [], [], [], [], [], [], []
    for l in range(DEPTH):
        ffn = (w_branch[l], w_out[l], ln1_g[l], ln1_b[l], w_router[l], router_bias[l], w_exp_gate[l], w_exp_up[l],
               w_exp_down[l], w_sh_gate[l], w_sh_up[l], w_sh_down[l], ln2_g[l], ln2_b[l])
        cache_rows = cache_win_k.shape[2]
        aq, ak, av, bq, bf, bi, bg, cq, gl = project_in(xp, w_in[l])
        kh, vh = split_heads(ak, A_KV_HEADS), split_heads(av, A_KV_HEADS)
        a_out = swa_prompt(split_heads(aq, A_HEADS), kh, vh, a_sinks[l], slopes)
        q, k, v, logf = hgrn_gates(bq, bf, bi, lower_bounds[l])
        o, s_fin = hgrn_prompt(q, k, v, logf)
        b_out = hgrn_readout(o, bg, b_norm_g[l])
        mk, mv = memory_kv(mem_prompt, w_mem_kv[l])
        c_out = memory_attend(split_heads(cq, C_HEADS), mk, mv)
        wk_p.append(kh[:, -cache_rows:])
        wv_p.append(vh[:, -cache_rows:])
        hs_p.append(s_fin)
        mk_p.append(mk)
        mv_p.append(mv)
        xp = merge_and_channel_mix(xp, a_out, b_out, c_out, gl, *ffn)
        aq, ak, av, bq, bf, bi, bg, cq, gl = project_in(xs, w_in[l])
        a_out, kw, vw = swa_sample(split_heads(aq, A_HEADS), split_heads(ak, A_KV_HEADS), split_heads(av, A_KV_HEADS),
                                   cache_win_k[l], cache_win_v[l], a_sinks[l], slopes)
        q, k, v, logf = hgrn_gates(bq, bf, bi, lower_bounds[l])
        s_new, o = hgrn_block(state_hgrn[l].astype(jnp.float32), q, k, v, logf)
        b_out = hgrn_readout(o, bg, b_norm_g[l])
        c_out = memory_attend(split_heads(cq, C_HEADS), cache_mem_k[l].astype(xs.dtype), cache_mem_v[l].astype(xs.dtype))
        wk_s.append(kw)
        wv_s.append(vw)
        hs_s.append(s_new)
        xs = merge_and_channel_mix(xs, a_out, b_out, c_out, gl, *ffn)
    return (xp, xs, jnp.stack(wk_p), jnp.stack(wv_p), jnp.stack(hs_p), jnp.stack(mk_p), jnp.stack(mv_p),
            jnp.stack(wk_s), jnp.stack(wv_s), jnp.stack(hs_s))
```

```python
import functools

import jax
import jax.numpy as jnp
import numpy as np
from jax import lax
from jax.experimental import pallas as pl
from jax.experimental.pallas import tpu as pltpu

F32 = jnp.float32
BF16 = jnp.bfloat16

D_MODEL = 2048
DEPTH = 1
PAST_LEN = 2048
CHUNK = 64
A_HEADS = 16
A_KV_HEADS = 4
A_GROUP = A_HEADS // A_KV_HEADS
A_HEAD_DIM = 64
A_WIDTH = A_HEADS * A_HEAD_DIM
A_KV_WIDTH = A_KV_HEADS * A_HEAD_DIM
WINDOW = 128
WIN_CHUNKS = WINDOW // CHUNK
B_HEADS = 8
B_KEY_DIM = 128
B_VAL_DIM = 128
B_WIDTH = B_HEADS * B_VAL_DIM
SUB = 16
MEM_LEN = 256
C_HEADS = 4
C_HEAD_DIM = 256
C_WIDTH = C_HEADS * C_HEAD_DIM
N_EXPERTS = 64
N_GROUPS = 8
GROUP_SIZE = N_EXPERTS // N_GROUPS
TOPK_GROUPS = 4
TOP_K = 8
EXPERT_DIM = 512
ROUTED_SCALE = 2.5
DN_ALPHA = (2 * DEPTH) ** 0.25
LN_EPS = 1e-5
RMS_EPS = 1e-6

MOE_BLOCK = 256
VMEM_LIMIT = 56 * 1024 * 1024


def _cparams(sem):
    return pltpu.CompilerParams(dimension_semantics=sem, vmem_limit_bytes=VMEM_LIMIT)


def _resident(shape, index_map):
    return pl.BlockSpec(shape, index_map, pipeline_mode=pl.Buffered(1))


def _sigmoid(x):
    return 1.0 / (1.0 + jnp.exp(-x))


def _layer_norm(x, g, b):
    mu = jnp.mean(x, axis=-1, keepdims=True)
    xc = x - mu
    var = jnp.mean(xc * xc, axis=-1, keepdims=True)
    return xc * lax.rsqrt(var + LN_EPS) * g + b


def _mm_kernel(x_ref, w_ref, o_ref, *, scale):
    acc = jnp.dot(x_ref[...], w_ref[...], preferred_element_type=F32)
    if scale != 1.0:
        acc = acc * scale
    o_ref[...] = acc.astype(o_ref.dtype)


def _matmul(x, w, *, tm, tn, out_dtype, scale=1.0, name):
    m, k = x.shape
    n = w.shape[1]
    return pl.pallas_call(
        functools.partial(_mm_kernel, scale=scale),
        grid=(m // tm, n // tn),
        in_specs=[pl.BlockSpec((tm, k), lambda i, j: (i, 0)), pl.BlockSpec((k, tn), lambda i, j: (0, j))],
        out_specs=pl.BlockSpec((tm, tn), lambda i, j: (i, j)),
        out_shape=jax.ShapeDtypeStruct((m, n), out_dtype),
        compiler_params=_cparams(("parallel", "parallel")),
        name=name,
    )(x, w)


def _swa_heads(q, kband, vband, bias_ref, sink_ref, valid):
    rows = q.shape[0]
    out_pairs = []
    for h in range(A_KV_HEADS):
        q4 = jnp.concatenate(
            [q[:, (A_GROUP * h + g) * A_HEAD_DIM:(A_GROUP * h + g + 1) * A_HEAD_DIM] for g in range(A_GROUP)], axis=0)
        kh = kband[:, h * A_HEAD_DIM:(h + 1) * A_HEAD_DIM]
        vh = vband[:, h * A_HEAD_DIM:(h + 1) * A_HEAD_DIM]
        s = lax.dot_general(q4, kh, (((1,), (1,)), ((), ())), preferred_element_type=F32)
        s = s - bias_ref[h]
        if valid is not None:
            s = jnp.where(valid, s, -jnp.inf)
        sink = sink_ref[h]
        m = jnp.maximum(jnp.max(s, axis=-1, keepdims=True), sink)
        p = jnp.exp(s - m)
        p = p / (jnp.sum(p, axis=-1, keepdims=True) + jnp.exp(sink - m))
        o = jnp.dot(p.astype(BF16), vh, preferred_element_type=F32)
        for g in range(0, A_GROUP, 2):
            out_pairs.append(jnp.concatenate([o[g * rows:(g + 1) * rows], o[(g + 1) * rows:(g + 2) * rows]], axis=1))
    return out_pairs


def _swa_prompt_kernel(q_ref, kc_ref, kp_ref, vc_ref, vp_ref, bias_ref, sink_ref, o_ref, *, n_chunks):
    i = pl.program_id(1)
    pad = WIN_CHUNKS * CHUNK
    n_band = pad + CHUNK
    kb = jnp.concatenate([kp_ref[...], kc_ref[...]], axis=0).astype(BF16)
    vb = jnp.concatenate([vp_ref[...], vc_ref[...]], axis=0).astype(BF16)
    s_idx = lax.broadcasted_iota(jnp.int32, (1, n_band), 1)
    for c in range(n_chunks):
        q = q_ref[c * CHUNK:(c + 1) * CHUNK, :]
        valid = (s_idx + (i * (n_chunks * CHUNK) + c * CHUNK - pad)) >= 0
        pairs = _swa_heads(q, kb[c * CHUNK:c * CHUNK + n_band], vb[c * CHUNK:c * CHUNK + n_band], bias_ref, sink_ref,
                           valid)
        for j, pr in enumerate(pairs):
            o_ref[c * CHUNK:(c + 1) * CHUNK, j * 128:(j + 1) * 128] = pr.astype(o_ref.dtype)


def _swa_sample_kernel(q_ref, kn_ref, kc_ref, vn_ref, vc_ref, bias_ref, sink_ref, o_ref):
    kb = jnp.concatenate([kc_ref[0], kn_ref[...]], axis=0).astype(BF16)
    vb = jnp.concatenate([vc_ref[0], vn_ref[...]], axis=0).astype(BF16)
    pairs = _swa_heads(q_ref[...], kb, vb, bias_ref, sink_ref, None)
    for j, pr in enumerate(pairs):
        o_ref[:, j * 128:(j + 1) * 128] = pr.astype(o_ref.dtype)


def _alibi_slopes():
    return (2.0 ** (-8.0 * np.arange(1, A_HEADS + 1) / A_HEADS)).astype(np.float32)


def _swa_bias(q_pos, k_pos, valid):
    dist = np.abs(q_pos[:, None] - k_pos[None, :]).astype(np.float32)
    slopes = _alibi_slopes().reshape(A_KV_HEADS, A_GROUP)
    bias = slopes[:, :, None, None] * dist[None, None]
    if valid is not None:
        bias = np.where(valid[None, None], bias, np.inf)
    return bias.reshape(A_KV_HEADS, A_GROUP * len(q_pos), len(k_pos)).astype(np.float32)


def _sink_cols(sinks, rows):
    s = sinks.astype(F32).reshape(A_KV_HEADS, A_GROUP, 1)
    return jnp.broadcast_to(s, (A_KV_HEADS, A_GROUP, rows)).reshape(A_KV_HEADS, A_GROUP * rows, 1)


def _swa_prompt(q, k, v, sinks, *, batch, seq, n_chunks=4):
    tq = n_chunks * CHUNK
    pad = WIN_CHUNKS * CHUNK
    nb = seq // tq
    bias = jnp.asarray(_swa_bias(pad + np.arange(CHUNK), np.arange(pad + CHUNK), None))
    sink = _sink_cols(sinks, CHUNK)
    prev_per_blk = tq // pad

    def cur(b, i):
        return (b * nb + i, 0)

    def prev(b, i):
        return (jnp.maximum((b * nb + i) * prev_per_blk - 1, b * nb * prev_per_blk), 0)

    return pl.pallas_call(
        functools.partial(_swa_prompt_kernel, n_chunks=n_chunks),
        grid=(batch, nb),
        in_specs=[
            pl.BlockSpec((tq, A_WIDTH), cur),
            pl.BlockSpec((tq, A_KV_WIDTH), cur),
            pl.BlockSpec((pad, A_KV_WIDTH), prev),
            pl.BlockSpec((tq, A_KV_WIDTH), cur),
            pl.BlockSpec((pad, A_KV_WIDTH), prev),
            pl.BlockSpec(bias.shape, lambda b, i: (0, 0, 0)),
            pl.BlockSpec(sink.shape, lambda b, i: (0, 0, 0)),
        ],
        out_specs=pl.BlockSpec((tq, A_WIDTH), cur),
        out_shape=jax.ShapeDtypeStruct((batch * seq, A_WIDTH), BF16),
        compiler_params=_cparams(("parallel", "arbitrary")),
        name="swa_prompt",
    )(q, k, k, v, v, bias, sink)


def _swa_sample(q, k, v, k_cache, v_cache, sinks, *, row0, batch, seq):
    lc = k_cache.shape[1]
    q_pos = PAST_LEN + np.arange(seq)
    k_pos = PAST_LEN - lc + np.arange(lc + seq)
    cdiff = q_pos[:, None] // CHUNK - k_pos[None, :] // CHUNK
    valid = (cdiff >= 0) & (cdiff <= WIN_CHUNKS)
    bias = jnp.asarray(_swa_bias(q_pos, k_pos, valid))
    sink = _sink_cols(sinks, seq)
    blk0 = row0 // seq

    def rows(b):
        return (blk0 + b, 0)

    return pl.pallas_call(
        _swa_sample_kernel,
        grid=(batch,),
        in_specs=[
            pl.BlockSpec((seq, A_WIDTH), rows),
            pl.BlockSpec((seq, A_KV_WIDTH), rows),
            pl.BlockSpec((1, lc, A_KV_WIDTH), lambda b: (b, 0, 0)),
            pl.BlockSpec((seq, A_KV_WIDTH), rows),
            pl.BlockSpec((1, lc, A_KV_WIDTH), lambda b: (b, 0, 0)),
            pl.BlockSpec(bias.shape, lambda b: (0, 0, 0)),
            pl.BlockSpec(sink.shape, lambda b: (0, 0, 0)),
        ],
        out_specs=pl.BlockSpec((seq, A_WIDTH), lambda b: (b, 0)),
        out_shape=jax.ShapeDtypeStruct((batch * seq, A_WIDTH), BF16),
        compiler_params=_cparams(("parallel",)),
        name="swa_sample",
    )(q, k, k_cache, v, v_cache, bias, sink)


def _split3(x):
    hi = x.astype(BF16)
    r1 = x - hi.astype(F32)
    mid = r1.astype(BF16)
    lo = (r1 - mid.astype(F32)).astype(BF16)
    return hi, mid, lo


def _dot3(mat, parts):
    acc = jnp.dot(mat, parts[0], preferred_element_type=F32)
    acc = acc + jnp.dot(mat, parts[1], preferred_element_type=F32)
    return acc + jnp.dot(mat, parts[2], preferred_element_type=F32)


def _hgrn_kernel(zq_ref, zf_ref, zi_ref, zg_ref, lb_ref, ng_ref, tri_ref, tsel_ref, s0_ref, o_ref, sfin_ref, s_scr,
                 *, blk):
    j = pl.program_id(1)

    @pl.when(j == 0)
    def _():
        s_scr[...] = s0_ref[0]

    bq = zq_ref[...]
    fl = zf_ref[...]
    v = zi_ref[...]
    bg = zg_ref[...]
    log_lb = lb_ref[0:1, :]
    log1m_lb = lb_ref[1:2, :]
    one_m_lb = lb_ref[2:3, :]

    q = bq * _sigmoid(bq) * (B_KEY_DIM ** -0.5)
    log_sig = jnp.minimum(fl, 0.0) - jnp.log1p(jnp.exp(-jnp.abs(fl)))
    c = log1m_lb + log_sig
    logf = jnp.maximum(log_lb, c) + jnp.log1p(jnp.exp(-jnp.abs(log_lb - c)))
    k = one_m_lb * _sigmoid(-fl)

    parts = _split3(logf)
    b = _dot3(tri_ref[...], parts)
    rq = _dot3(tsel_ref[...], parts)
    qt = q * jnp.exp(b - rq)
    qb = (q * jnp.exp(b)).astype(BF16)
    b_last = b[blk - 1:blk, :]
    khat = (k * jnp.exp(b_last - b)).astype(BF16)
    e_last = jnp.exp(b_last)
    vb = v.astype(BF16)
    row = lax.broadcasted_iota(jnp.int32, (blk, 1), 0)
    n_sub = blk // SUB
    kts = []
    for i in range(n_sub):
        r_i = rq[i * SUB:i * SUB + 1, :]
        kts.append(jnp.where(row < (i + 1) * SUB, k * jnp.exp(r_i - b), 0.0).astype(BF16))
    qt = qt.astype(BF16)
    tril = lax.broadcasted_iota(jnp.int32, (blk, blk), 0) >= lax.broadcasted_iota(jnp.int32, (blk, blk), 1)

    outs = []
    for h in range(B_HEADS):
        hs = slice(h * B_KEY_DIM, (h + 1) * B_KEY_DIM)
        a = jnp.concatenate(
            [lax.dot_general(qt[i * SUB:(i + 1) * SUB, hs], kts[i][:, hs], (((1,), (1,)), ((), ())),
                             preferred_element_type=F32) for i in range(n_sub)], axis=0)
        a = jnp.where(tril, a, 0.0).astype(BF16)
        s_h = s_scr[h]
        o = jnp.dot(a, vb[:, hs], preferred_element_type=F32)
        o = o + jnp.dot(qb[:, hs], s_h.astype(BF16), preferred_element_type=F32)
        ds = lax.dot_general(khat[:, hs], vb[:, hs], (((0,), (0,)), ((), ())), preferred_element_type=F32)
        decay = jnp.transpose(jnp.broadcast_to(e_last[:, hs], (B_KEY_DIM, B_KEY_DIM)))
        s_scr[h] = decay * s_h + ds
        o = o * lax.rsqrt(jnp.mean(o * o, axis=-1, keepdims=True) + RMS_EPS)
        outs.append(o)
    o_all = jnp.concatenate(outs, axis=1) * ng_ref[...] * (bg * _sigmoid(bg))
    o_ref[...] = o_all.astype(o_ref.dtype)

    @pl.when(j == pl.num_programs(1) - 1)
    def _():
        sfin_ref[0] = s_scr[...]


def _hgrn(zb, lbp, ng, s0, *, row0, batch, seq, blk):
    nb = seq // blk
    blk0 = row0 // blk
    t = np.arange(blk)
    tri = jnp.asarray((t[:, None] >= t[None, :]).astype(np.float32), dtype=BF16)
    tsel = jnp.asarray((t[None, :] < (t[:, None] // SUB) * SUB).astype(np.float32), dtype=BF16)

    def zspec(col):
        return pl.BlockSpec((blk, B_WIDTH), lambda b, j: (blk0 + b * nb + j, col))

    state_spec = pl.BlockSpec((1, B_HEADS, B_KEY_DIM, B_VAL_DIM), lambda b, j: (b, 0, 0, 0))
    return pl.pallas_call(
        functools.partial(_hgrn_kernel, blk=blk),
        grid=(batch, nb),
        in_specs=[
            zspec(0), zspec(1), zspec(2), zspec(3),
            pl.BlockSpec((3, B_WIDTH), lambda b, j: (0, 0)),
            pl.BlockSpec((1, B_WIDTH), lambda b, j: (0, 0)),
            pl.BlockSpec((blk, blk), lambda b, j: (0, 0)),
            pl.BlockSpec((blk, blk), lambda b, j: (0, 0)),
            state_spec,
        ],
        out_specs=[pl.BlockSpec((blk, B_WIDTH), lambda b, j: (b * nb + j, 0)), state_spec],
        out_shape=[jax.ShapeDtypeStruct((batch * seq, B_WIDTH), BF16),
                   jax.ShapeDtypeStruct((batch, B_HEADS, B_KEY_DIM, B_VAL_DIM), F32)],
        scratch_shapes=[pltpu.VMEM((B_HEADS, B_KEY_DIM, B_VAL_DIM), F32)],
        compiler_params=_cparams(("parallel", "arbitrary")),
        name=f"hgrn_blk{blk}",
    )(zb, zb, zb, zb, lbp, ng, tri, tsel, s0)


def _cattn_kernel(q_ref, mk_ref, mv_ref, o_ref):
    mk = mk_ref[0].astype(BF16)
    mv = mv_ref[0].astype(BF16)
    q = q_ref[...]
    for h in range(C_HEADS):
        hs = slice(h * C_HEAD_DIM, (h + 1) * C_HEAD_DIM)
        s = lax.dot_general(q[:, hs], mk[:, hs], (((1,), (1,)), ((), ())), preferred_element_type=F32)
        p = jnp.exp(s - jnp.max(s, axis=-1, keepdims=True))
        p = p / jnp.sum(p, axis=-1, keepdims=True)
        o_ref[:, hs] = jnp.dot(p.astype(BF16), mv[:, hs], preferred_element_type=F32).astype(o_ref.dtype)


def _cattn(q, mk, mv, *, row0, batch, seq, tq):
    nb = seq // tq
    blk0 = row0 // tq
    mem_spec = pl.BlockSpec((1, MEM_LEN, C_WIDTH), lambda b, i: (b, 0, 0))
    return pl.pallas_call(
        _cattn_kernel,
        grid=(batch, nb),
        in_specs=[pl.BlockSpec((tq, C_WIDTH), lambda b, i: (blk0 + b * nb + i, 0)), mem_spec, mem_spec],
        out_specs=pl.BlockSpec((tq, C_WIDTH), lambda b, i: (b * nb + i, 0)),
        out_shape=jax.ShapeDtypeStruct((batch * seq, C_WIDTH), BF16),
        compiler_params=_cparams(("parallel", "parallel")),
        name=f"cattn_tq{tq}",
    )(q, mk, mv)


def _merge_kernel(a_ref, b_ref, c_ref, g0_ref, g1_ref, g2_ref, pa_ref, pb_ref, pc_ref, o_ref):
    h = _sigmoid(g0_ref[...]) * jnp.dot(a_ref[...], pa_ref[...], preferred_element_type=F32)
    h = h + _sigmoid(g1_ref[...]) * jnp.dot(b_ref[...], pb_ref[...], preferred_element_type=F32)
    h = h + _sigmoid(g2_ref[...]) * jnp.dot(c_ref[...], pc_ref[...], preferred_element_type=F32)
    o_ref[...] = h.astype(o_ref.dtype)


def _merge(a, b, c, gl, p, *, tm):
    n = a.shape[0]
    width = a.shape[1]

    def rows(i):
        return (i, 0)

    return pl.pallas_call(
        _merge_kernel,
        grid=(n // tm,),
        in_specs=[
            pl.BlockSpec((tm, width), rows), pl.BlockSpec((tm, width), rows), pl.BlockSpec((tm, width), rows),
            pl.BlockSpec((tm, D_MODEL), lambda i: (i, 0)),
            pl.BlockSpec((tm, D_MODEL), lambda i: (i, 1)),
            pl.BlockSpec((tm, D_MODEL), lambda i: (i, 2)),
            _resident((width, D_MODEL), lambda i: (0, 0)),
            _resident((width, D_MODEL), lambda i: (1, 0)),
            _resident((width, D_MODEL), lambda i: (2, 0)),
        ],
        out_specs=pl.BlockSpec((tm, D_MODEL), rows),
        out_shape=jax.ShapeDtypeStruct((n, D_MODEL), BF16),
        compiler_params=_cparams(("parallel",)),
        name="merge",
    )(a, b, c, gl, gl, gl, p, p, p)


def _first_index(hit_src, m, iota, size, axis):
    return jnp.min(jnp.where(hit_src == m, iota, size), axis=axis, keepdims=True)


def _route(x1, wr_ref, rb_ref, upper_ref, run_ref, idx_ref, wts_ref, pos_ref):
    tm = x1.shape[0]
    logits = lax.dot_general(wr_ref[...], x1, (((1,), (1,)), ((), ())), precision=lax.Precision.HIGHEST,
                             preferred_element_type=F32)
    scores = _sigmoid(logits)
    choice = scores + rb_ref[...]
    g3 = choice.reshape(N_GROUPS, GROUP_SIZE, tm)
    mem_iota = lax.broadcasted_iota(jnp.int32, g3.shape, 1)
    m1 = jnp.max(g3, axis=1, keepdims=True)
    first = _first_index(g3, m1, mem_iota, GROUP_SIZE, 1)
    m2 = jnp.max(jnp.where(mem_iota == first, -jnp.inf, g3), axis=1, keepdims=True)
    gscore = (m1 + m2).reshape(N_GROUPS, tm)
    g_iota = lax.broadcasted_iota(jnp.int32, gscore.shape, 0)
    gsel = jnp.zeros(gscore.shape, F32)
    cur = gscore
    for _ in range(TOPK_GROUPS):
        m = jnp.max(cur, axis=0, keepdims=True)
        hit = g_iota == _first_index(cur, m, g_iota, N_GROUPS, 0)
        gsel = jnp.where(hit, 1.0, gsel)
        cur = jnp.where(hit, -jnp.inf, cur)
    masked = jnp.where(gsel.reshape(N_GROUPS, 1, tm) > 0.5, g3, -jnp.inf).reshape(N_EXPERTS, tm)
    e_iota = lax.broadcasted_iota(jnp.int32, masked.shape, 0)
    sel = jnp.zeros(masked.shape, F32)
    cur = masked
    idxs, ws = [], []
    for _ in range(TOP_K):
        m = jnp.max(cur, axis=0, keepdims=True)
        ei = _first_index(cur, m, e_iota, N_EXPERTS, 0)
        hit = e_iota == ei
        idxs.append(ei)
        ws.append(jnp.sum(jnp.where(hit, scores, 0.0), axis=0, keepdims=True))
        sel = jnp.where(hit, 1.0, sel)
        cur = jnp.where(hit, -jnp.inf, cur)
    w = jnp.concatenate(ws, axis=0)
    w = w / jnp.sum(w, axis=0, keepdims=True) * ROUTED_SCALE
    excl = jnp.dot(sel.astype(BF16), upper_ref[...], preferred_element_type=F32)
    posfull = excl + run_ref[:, 0:1]
    run_ref[...] = run_ref[...] + jnp.sum(sel, axis=1, keepdims=True)
    pos = [jnp.sum(jnp.where(e_iota == ei, posfull, 0.0), axis=0, keepdims=True) for ei in idxs]
    idx_ref[0] = jnp.concatenate(idxs, axis=0)
    wts_ref[0] = w
    pos_ref[0] = jnp.concatenate(pos, axis=0).astype(jnp.int32)


def _out_ln_route_kernel(x_ref, h_ref, wo_ref, g_ref, b_ref, wr_ref, rb_ref, upper_ref,
                         x1_ref, idx_ref, wts_ref, pos_ref, cnt_ref, run_ref):
    @pl.when(pl.program_id(0) == 0)
    def _():
        run_ref[...] = jnp.zeros_like(run_ref)

    y = DN_ALPHA * x_ref[...] + jnp.dot(h_ref[...], wo_ref[...], preferred_element_type=F32)
    x1 = _layer_norm(y, g_ref[...], b_ref[...])
    x1_ref[...] = x1
    _route(x1, wr_ref, rb_ref, upper_ref, run_ref, idx_ref, wts_ref, pos_ref)
    cnt_ref[...] = run_ref[...].astype(jnp.int32)


def _out_ln_route(x, h, w_out, ln_g, ln_b, w_router_t, router_bias, *, tm):
    n = x.shape[0]
    nt = n // tm
    t = np.arange(tm)
    upper = jnp.asarray((t[:, None] < t[None, :]).astype(np.float32), dtype=BF16)

    def rows(i):
        return (i, 0)

    def const(i):
        return (0, 0)

    small = pl.BlockSpec((1, TOP_K, tm), lambda i: (i, 0, 0))
    return pl.pallas_call(
        _out_ln_route_kernel,
        grid=(nt,),
        in_specs=[
            pl.BlockSpec((tm, D_MODEL), rows),
            pl.BlockSpec((tm, D_MODEL), rows),
            _resident((D_MODEL, D_MODEL), const),
            pl.BlockSpec((1, D_MODEL), const),
            pl.BlockSpec((1, D_MODEL), const),
            pl.BlockSpec((N_EXPERTS, D_MODEL), const),
            pl.BlockSpec((N_EXPERTS, 1), const),
            pl.BlockSpec((tm, tm), const),
        ],
        out_specs=[pl.BlockSpec((tm, D_MODEL), rows), small, small, small, pl.BlockSpec((N_EXPERTS, 128), const)],
        out_shape=[
            jax.ShapeDtypeStruct((n, D_MODEL), F32),
            jax.ShapeDtypeStruct((nt, TOP_K, tm), jnp.int32),
            jax.ShapeDtypeStruct((nt, TOP_K, tm), F32),
            jax.ShapeDtypeStruct((nt, TOP_K, tm), jnp.int32),
            jax.ShapeDtypeStruct((N_EXPERTS, 128), jnp.int32),
        ],
        scratch_shapes=[pltpu.VMEM((N_EXPERTS, 128), F32)],
        compiler_params=_cparams(("arbitrary",)),
        name="out_ln_route",
    )(x, h, w_out, ln_g, ln_b, w_router_t, router_bias, upper)


def _row_copy(src_hbm, row, dst, dst_row, sem):
    return pltpu.make_async_copy(src_hbm.at[pl.ds(row, 1)], dst.at[pl.ds(dst_row, 1)], sem)


def _moe_kernel(be_ref, bv_ref, tok_ref, x_hbm, wg_ref, wu_ref, wd_ref, o_ref, xbuf, sem):
    i = pl.program_id(0)
    m = xbuf.shape[0]

    @pl.when(bv_ref[i] != 0)
    def _():
        def issue(r, carry):
            _row_copy(x_hbm, tok_ref[0, 0, r], xbuf, r, sem).start()
            return carry

        lax.fori_loop(0, m, issue, 0, unroll=8)
        pltpu.make_async_copy(x_hbm.at[pl.ds(0, m)], xbuf, sem).wait()
        x = xbuf[...].astype(BF16)
        g = jnp.dot(x, wg_ref[0], preferred_element_type=F32)
        u = jnp.dot(x, wu_ref[0], preferred_element_type=F32)
        hmid = (g * _sigmoid(g) * u).astype(BF16)
        o_ref[...] = jnp.dot(hmid, wd_ref[0], preferred_element_type=F32)

    @pl.when(bv_ref[i] == 0)
    def _():
        o_ref[...] = jnp.zeros_like(o_ref)


def _moe(x1, blk_e, blk_valid, buf_t, w_gate, w_up, w_down):
    n_blocks = blk_e.shape[0]
    m = MOE_BLOCK
    grid_spec = pltpu.PrefetchScalarGridSpec(
        num_scalar_prefetch=2,
        grid=(n_blocks,),
        in_specs=[
            pl.BlockSpec((1, 1, m), lambda i, be, bv: (i, 0, 0), memory_space=pltpu.SMEM),
            pl.BlockSpec(memory_space=pl.ANY),
            pl.BlockSpec((1, D_MODEL, EXPERT_DIM), lambda i, be, bv: (be[i], 0, 0)),
            pl.BlockSpec((1, D_MODEL, EXPERT_DIM), lambda i, be, bv: (be[i], 0, 0)),
            pl.BlockSpec((1, EXPERT_DIM, D_MODEL), lambda i, be, bv: (be[i], 0, 0)),
        ],
        out_specs=pl.BlockSpec((m, D_MODEL), lambda i, be, bv: (i, 0)),
        scratch_shapes=[pltpu.VMEM((m, D_MODEL), F32), pltpu.SemaphoreType.DMA(())],
    )
    return pl.pallas_call(
        _moe_kernel,
        grid_spec=grid_spec,
        out_shape=jax.ShapeDtypeStruct((n_blocks * m, D_MODEL), F32),
        compiler_params=_cparams(("arbitrary",)),
        name="moe_experts",
    )(blk_e, blk_valid, buf_t.reshape(n_blocks, 1, m), x1, w_gate, w_up, w_down)


def _final_kernel(dest_ref, x1_ref, w_ref, yb_hbm, sg_ref, su_ref, sd_ref, g_ref, b_ref, o_ref, ybuf, sem):
    tm = x1_ref.shape[0]

    def issue(t, carry):
        for k in range(TOP_K):
            _row_copy(yb_hbm, dest_ref[0, k, t], ybuf.at[k], t, sem).start()
        return carry

    lax.fori_loop(0, tm, issue, 0, unroll=2)
    x1 = x1_ref[...]
    xb = x1.astype(BF16)
    g = jnp.dot(xb, sg_ref[...], preferred_element_type=F32)
    u = jnp.dot(xb, su_ref[...], preferred_element_type=F32)
    y = jnp.dot((g * _sigmoid(g) * u).astype(BF16), sd_ref[...], preferred_element_type=F32)
    w = w_ref[...]
    for k in range(TOP_K):
        pltpu.make_async_copy(yb_hbm.at[pl.ds(0, tm)], ybuf.at[k], sem).wait()
    routed = None
    for k in range(TOP_K):
        term = w[:, k:k + 1] * ybuf[k]
        routed = term if routed is None else routed + term
    o_ref[...] = _layer_norm(DN_ALPHA * x1 + (routed + y), g_ref[...], b_ref[...])


def _final(dest, x1, wts_t, yb, w_sg, w_su, w_sd, ln_g, ln_b, *, tm):
    n = x1.shape[0]
    nt = n // tm

    def rows(i):
        return (i, 0)

    def const(i):
        return (0, 0)

    return pl.pallas_call(
        _final_kernel,
        grid=(nt,),
        in_specs=[
            pl.BlockSpec((1, TOP_K, tm), lambda i: (i, 0, 0), memory_space=pltpu.SMEM),
            pl.BlockSpec((tm, D_MODEL), rows),
            pl.BlockSpec((tm, TOP_K), rows),
            pl.BlockSpec(memory_space=pl.ANY),
            _resident((D_MODEL, EXPERT_DIM), const),
            _resident((D_MODEL, EXPERT_DIM), const),
            _resident((EXPERT_DIM, D_MODEL), const),
            pl.BlockSpec((1, D_MODEL), const),
            pl.BlockSpec((1, D_MODEL), const),
        ],
        out_specs=pl.BlockSpec((tm, D_MODEL), rows),
        out_shape=jax.ShapeDtypeStruct((n, D_MODEL), F32),
        scratch_shapes=[pltpu.VMEM((TOP_K, tm, D_MODEL), F32), pltpu.SemaphoreType.DMA(())],
        compiler_params=_cparams(("arbitrary",)),
        name="combine_shared_ln2",
    )(dest, x1, wts_t, yb, w_sg, w_su, w_sd, ln_g, ln_b)


def kernel(x_prompt, x_sample, cache_win_k, cache_win_v, state_hgrn, cache_mem_k, cache_mem_v, mem_prompt,
           w_in, w_mem_kv, a_sinks, b_lb_logits, b_norm_g, w_branch, w_out, ln1_g, ln1_b,
           w_router, router_bias, w_exp_gate, w_exp_up, w_exp_down, w_sh_gate, w_sh_up, w_sh_down, ln2_g, ln2_b):
    assert w_in.shape[0] == DEPTH == 1
    batch, seq, _ = x_prompt.shape
    dbatch, dseq, _ = x_sample.shape
    n_p = batch * seq
    n_s = dbatch * dseq
    n = n_p + n_s
    l = 0

    x_all = jnp.concatenate([x_prompt.reshape(n_p, D_MODEL), x_sample.reshape(n_s, D_MODEL)], axis=0)
    xb = x_all.astype(BF16)
    win = w_in[l]
    o_k, o_v, o_b, o_c, o_g = A_WIDTH, A_WIDTH + A_KV_WIDTH, A_WIDTH + 2 * A_KV_WIDTH, None, None
    o_c = o_b + 4 * B_WIDTH
    o_g = o_c + C_WIDTH
    tm_in = 1280
    proj = functools.partial(_matmul, xb, tm=tm_in)
    aq = proj(win[:, :o_k].astype(BF16), tn=512, out_dtype=BF16, scale=A_HEAD_DIM ** -0.5, name="proj_aq")
    ak = proj(win[:, o_k:o_v].astype(BF16), tn=A_KV_WIDTH, out_dtype=F32, name="proj_ak")
    av = proj(win[:, o_v:o_b].astype(BF16), tn=A_KV_WIDTH, out_dtype=F32, name="proj_av")
    zb = proj(win[:, o_b:o_c].astype(BF16), tn=512, out_dtype=F32, name="proj_hgrn")
    cq = proj(win[:, o_c:o_g].astype(BF16), tn=512, out_dtype=BF16, scale=C_HEAD_DIM ** -0.5, name="proj_cq")
    gl = proj(win[:, o_g:].astype(BF16), tn=512, out_dtype=F32, name="proj_gate")

    a_p = _swa_prompt(aq, ak, av, a_sinks[l], batch=batch, seq=seq)
    lc = cache_win_k.shape[2]
    kc = cache_win_k[l].reshape(dbatch, lc, A_KV_WIDTH)
    vc = cache_win_v[l].reshape(dbatch, lc, A_KV_WIDTH)
    a_s = _swa_sample(aq, ak, av, kc, vc, a_sinks[l], row0=n_p, batch=dbatch, seq=dseq)

    lower = jnp.cumsum(jax.nn.softmax(b_lb_logits.astype(F32), axis=0), axis=0)[l]
    lbp = jnp.stack([jnp.log(lower), jnp.log1p(-lower), 1.0 - lower])
    ng = jnp.tile(b_norm_g[l].astype(F32), B_HEADS).reshape(1, B_WIDTH)
    s_zero = jnp.zeros((batch, B_HEADS, B_KEY_DIM, B_VAL_DIM), F32)
    b_p, hs_p = _hgrn(zb, lbp, ng, s_zero, row0=0, batch=batch, seq=seq, blk=CHUNK)
    b_s, hs_s = _hgrn(zb, lbp, ng, state_hgrn[l].astype(F32), row0=n_p, batch=dbatch, seq=dseq, blk=dseq)

    mem = mem_prompt.reshape(batch * MEM_LEN, D_MODEL).astype(BF16)
    mkv = _matmul(mem, w_mem_kv[l].astype(BF16), tm=batch * MEM_LEN, tn=512, out_dtype=F32, name="proj_mem")
    mk = mkv[:, :C_WIDTH].reshape(batch, MEM_LEN, C_WIDTH)
    mv = mkv[:, C_WIDTH:].reshape(batch, MEM_LEN, C_WIDTH)
    c_p = _cattn(cq, mk, mv, row0=0, batch=batch, seq=seq, tq=512)
    c_s = _cattn(cq, cache_mem_k[l].reshape(dbatch, MEM_LEN, C_WIDTH), cache_mem_v[l].reshape(dbatch, MEM_LEN, C_WIDTH),
                 row0=n_p, batch=dbatch, seq=dseq, tq=dseq)

    a_all = jnp.concatenate([a_p, a_s], axis=0)
    b_all = jnp.concatenate([b_p, b_s], axis=0)
    c_all = jnp.concatenate([c_p, c_s], axis=0)
    h = _merge(a_all, b_all, c_all, gl, w_branch[l].astype(BF16), tm=256)
    x1, idx, wts, pos, cnt = _out_ln_route(
        x_all, h, w_out[l].astype(BF16), ln1_g[l].reshape(1, D_MODEL), ln1_b[l].reshape(1, D_MODEL),
        w_router[l].T.astype(F32), router_bias[l].reshape(N_EXPERTS, 1).astype(F32), tm=256)

    m = MOE_BLOCK
    n_pairs = n * TOP_K
    n_blocks = (n_pairs + m - 1) // m + N_EXPERTS
    counts = cnt[:, 0]
    padded = (counts + m - 1) // m * m
    pad_end = jnp.cumsum(padded)
    pad_start = pad_end - padded
    idx_t = jnp.transpose(idx, (0, 2, 1)).reshape(n, TOP_K)
    pos_t = jnp.transpose(pos, (0, 2, 1)).reshape(n, TOP_K)
    wts_t = jnp.transpose(wts, (0, 2, 1)).reshape(n, TOP_K)
    dest = pad_start[idx_t] + pos_t
    tok = jnp.broadcast_to(jnp.arange(n, dtype=jnp.int32)[:, None], (n, TOP_K))
    buf_t = jnp.zeros((n_blocks * m,), jnp.int32).at[dest.reshape(-1)].set(tok.reshape(-1))
    blk_first = jnp.arange(n_blocks, dtype=jnp.int32) * m
    blk_e = jnp.minimum(jnp.searchsorted(pad_end, blk_first, side="right"), N_EXPERTS - 1).astype(jnp.int32)
    blk_valid = (blk_first < pad_end[-1]).astype(jnp.int32)

    yb = _moe(x1, blk_e, blk_valid, buf_t, w_exp_gate[l].astype(BF16), w_exp_up[l].astype(BF16),
              w_exp_down[l].astype(BF16))
    tm_f = 128
    dest_blk = jnp.transpose(dest.reshape(n // tm_f, tm_f, TOP_K), (0, 2, 1)).astype(jnp.int32)
    y = _final(dest_blk, x1, wts_t, yb, w_sh_gate[l].astype(BF16), w_sh_up[l].astype(BF16), w_sh_down[l].astype(BF16),
               ln2_g[l].reshape(1, D_MODEL), ln2_b[l].reshape(1, D_MODEL), tm=tm_f)

    y_p = y[:n_p].reshape(batch, seq, D_MODEL)
    y_s = y[n_p:].reshape(dbatch, dseq, D_MODEL)
    k_p = ak[:n_p].reshape(batch, seq, A_KV_HEADS, A_HEAD_DIM)[:, -lc:]
    v_p = av[:n_p].reshape(batch, seq, A_KV_HEADS, A_HEAD_DIM)[:, -lc:]
    k_s = ak[n_p:].reshape(dbatch, dseq, A_KV_HEADS, A_HEAD_DIM)
    v_s = av[n_p:].reshape(dbatch, dseq, A_KV_HEADS, A_HEAD_DIM)
    wk_s = jnp.concatenate([cache_win_k[l].astype(F32), k_s], axis=1)[:, -lc:]
    wv_s = jnp.concatenate([cache_win_v[l].astype(F32), v_s], axis=1)[:, -lc:]
    mk_o = mk.reshape(batch, MEM_LEN, C_HEADS, C_HEAD_DIM)
    mv_o = mv.reshape(batch, MEM_LEN, C_HEADS, C_HEAD_DIM)
    return (y_p, y_s, k_p[None], v_p[None], hs_p[None], mk_o[None], mv_o[None], wk_s[None], wv_s[None], hs_s[None])
```

```python
import functools

import jax
import jax.numpy as jnp
import numpy as np
from jax import lax
from jax.experimental import pallas as pl
from jax.experimental.pallas import tpu as pltpu

F32 = jnp.float32
BF16 = jnp.bfloat16

D_MODEL = 2048
DEPTH = 1
PAST_LEN = 2048
CHUNK = 64
A_HEADS = 16
A_KV_HEADS = 4
A_GROUP = A_HEADS // A_KV_HEADS
A_HEAD_DIM = 64
A_WIDTH = A_HEADS * A_HEAD_DIM
A_KV_WIDTH = A_KV_HEADS * A_HEAD_DIM
WINDOW = 128
WIN_CHUNKS = WINDOW // CHUNK
B_HEADS = 8
B_KEY_DIM = 128
B_VAL_DIM = 128
B_WIDTH = B_HEADS * B_VAL_DIM
SUB = 16
MEM_LEN = 256
C_HEADS = 4
C_HEAD_DIM = 256
C_WIDTH = C_HEADS * C_HEAD_DIM
N_EXPERTS = 64
N_GROUPS = 8
GROUP_SIZE = N_EXPERTS // N_GROUPS
TOPK_GROUPS = 4
TOP_K = 8
EXPERT_DIM = 512
ROUTED_SCALE = 2.5
DN_ALPHA = (2 * DEPTH) ** 0.25
LN_EPS = 1e-5
RMS_EPS = 1e-6

MOE_BLOCK = 256
VMEM_LIMIT = 56 * 1024 * 1024


def _cparams(sem):
    return pltpu.CompilerParams(dimension_semantics=sem, vmem_limit_bytes=VMEM_LIMIT)


def _resident(shape, index_map):
    return pl.BlockSpec(shape, index_map, pipeline_mode=pl.Buffered(1))


def _sigmoid(x):
    return 1.0 / (1.0 + jnp.exp(-x))


def _layer_norm(x, g, b):
    mu = jnp.mean(x, axis=-1, keepdims=True)
    xc = x - mu
    var = jnp.mean(xc * xc, axis=-1, keepdims=True)
    return xc * lax.rsqrt(var + LN_EPS) * g + b


def _mm_kernel(x_ref, w_ref, o_ref, *, scale):
    acc = jnp.dot(x_ref[...].astype(BF16), w_ref[...], preferred_element_type=F32)
    if scale != 1.0:
        acc = acc * scale
    o_ref[...] = acc.astype(o_ref.dtype)


def _matmul(x, w, *, tm, tn, out_dtype, scale=1.0, name):
    m, k = x.shape
    n = w.shape[1]
    return pl.pallas_call(
        functools.partial(_mm_kernel, scale=scale),
        grid=(m // tm, n // tn),
        in_specs=[pl.BlockSpec((tm, k), lambda i, j: (i, 0)), pl.BlockSpec((k, tn), lambda i, j: (0, j))],
        out_specs=pl.BlockSpec((tm, tn), lambda i, j: (i, j)),
        out_shape=jax.ShapeDtypeStruct((m, n), out_dtype),
        compiler_params=_cparams(("parallel", "parallel")),
        name=name,
    )(x, w)


def _swa_heads(q, kband, vband, bias_ref, sink_ref, valid):
    rows = q.shape[0]
    out_pairs = []
    for h in range(A_KV_HEADS):
        q4 = jnp.concatenate(
            [q[:, (A_GROUP * h + g) * A_HEAD_DIM:(A_GROUP * h + g + 1) * A_HEAD_DIM] for g in range(A_GROUP)], axis=0)
        kh = kband[:, h * A_HEAD_DIM:(h + 1) * A_HEAD_DIM]
        vh = vband[:, h * A_HEAD_DIM:(h + 1) * A_HEAD_DIM]
        s = lax.dot_general(q4, kh, (((1,), (1,)), ((), ())), preferred_element_type=F32)
        s = s - bias_ref[h]
        if valid is not None:
            s = jnp.where(valid, s, -jnp.inf)
        sink = sink_ref[h]
        m = jnp.maximum(jnp.max(s, axis=-1, keepdims=True), sink)
        p = jnp.exp(s - m)
        p = p / (jnp.sum(p, axis=-1, keepdims=True) + jnp.exp(sink - m))
        o = jnp.dot(p.astype(BF16), vh, preferred_element_type=F32)
        for g in range(0, A_GROUP, 2):
            out_pairs.append(jnp.concatenate([o[g * rows:(g + 1) * rows], o[(g + 1) * rows:(g + 2) * rows]], axis=1))
    return out_pairs


def _swa_prompt_kernel(q_ref, kc_ref, kp_ref, vc_ref, vp_ref, bias_ref, sink_ref, o_ref, *, n_chunks):
    i = pl.program_id(1)
    pad = WIN_CHUNKS * CHUNK
    n_band = pad + CHUNK
    kb = jnp.concatenate([kp_ref[...], kc_ref[...]], axis=0).astype(BF16)
    vb = jnp.concatenate([vp_ref[...], vc_ref[...]], axis=0).astype(BF16)
    s_idx = lax.broadcasted_iota(jnp.int32, (1, n_band), 1)
    for c in range(n_chunks):
        q = q_ref[c * CHUNK:(c + 1) * CHUNK, :]
        valid = (s_idx + (i * (n_chunks * CHUNK) + c * CHUNK - pad)) >= 0
        pairs = _swa_heads(q, kb[c * CHUNK:c * CHUNK + n_band], vb[c * CHUNK:c * CHUNK + n_band], bias_ref, sink_ref,
                           valid)
        for j, pr in enumerate(pairs):
            o_ref[c * CHUNK:(c + 1) * CHUNK, j * 128:(j + 1) * 128] = pr.astype(o_ref.dtype)


def _swa_sample_kernel(q_ref, kn_ref, kc_ref, vn_ref, vc_ref, bias_ref, sink_ref, o_ref):
    kb = jnp.concatenate([kc_ref[0], kn_ref[...]], axis=0).astype(BF16)
    vb = jnp.concatenate([vc_ref[0], vn_ref[...]], axis=0).astype(BF16)
    pairs = _swa_heads(q_ref[...], kb, vb, bias_ref, sink_ref, None)
    for j, pr in enumerate(pairs):
        o_ref[:, j * 128:(j + 1) * 128] = pr.astype(o_ref.dtype)


def _alibi_slopes():
    return (2.0 ** (-8.0 * np.arange(1, A_HEADS + 1) / A_HEADS)).astype(np.float32)


def _swa_bias(q_pos, k_pos, valid):
    dist = np.abs(q_pos[:, None] - k_pos[None, :]).astype(np.float32)
    slopes = _alibi_slopes().reshape(A_KV_HEADS, A_GROUP)
    bias = slopes[:, :, None, None] * dist[None, None]
    if valid is not None:
        bias = np.where(valid[None, None], bias, np.inf)
    return bias.reshape(A_KV_HEADS, A_GROUP * len(q_pos), len(k_pos)).astype(np.float32)


def _sink_cols(sinks, rows):
    s = sinks.astype(F32).reshape(A_KV_HEADS, A_GROUP, 1)
    return jnp.broadcast_to(s, (A_KV_HEADS, A_GROUP, rows)).reshape(A_KV_HEADS, A_GROUP * rows, 1)


def _swa_prompt(q, k, v, sinks, *, batch, seq, n_chunks=4):
    tq = n_chunks * CHUNK
    pad = WIN_CHUNKS * CHUNK
    nb = seq // tq
    bias = jnp.asarray(_swa_bias(pad + np.arange(CHUNK), np.arange(pad + CHUNK), None))
    sink = _sink_cols(sinks, CHUNK)
    prev_per_blk = tq // pad

    def cur(b, i):
        return (b * nb + i, 0)

    def prev(b, i):
        return (jnp.maximum((b * nb + i) * prev_per_blk - 1, b * nb * prev_per_blk), 0)

    return pl.pallas_call(
        functools.partial(_swa_prompt_kernel, n_chunks=n_chunks),
        grid=(batch, nb),
        in_specs=[
            pl.BlockSpec((tq, A_WIDTH), cur),
            pl.BlockSpec((tq, A_KV_WIDTH), cur),
            pl.BlockSpec((pad, A_KV_WIDTH), prev),
            pl.BlockSpec((tq, A_KV_WIDTH), cur),
            pl.BlockSpec((pad, A_KV_WIDTH), prev),
            pl.BlockSpec(bias.shape, lambda b, i: (0, 0, 0)),
            pl.BlockSpec(sink.shape, lambda b, i: (0, 0, 0)),
        ],
        out_specs=pl.BlockSpec((tq, A_WIDTH), cur),
        out_shape=jax.ShapeDtypeStruct((batch * seq, A_WIDTH), BF16),
        compiler_params=_cparams(("parallel", "arbitrary")),
        name="swa_prompt",
    )(q, k, k, v, v, bias, sink)


def _swa_sample(q, k, v, k_cache, v_cache, sinks, *, row0, batch, seq):
    lc = k_cache.shape[1]
    q_pos = PAST_LEN + np.arange(seq)
    k_pos = PAST_LEN - lc + np.arange(lc + seq)
    cdiff = q_pos[:, None] // CHUNK - k_pos[None, :] // CHUNK
    valid = (cdiff >= 0) & (cdiff <= WIN_CHUNKS)
    bias = jnp.asarray(_swa_bias(q_pos, k_pos, valid))
    sink = _sink_cols(sinks, seq)
    blk0 = row0 // seq

    def rows(b):
        return (blk0 + b, 0)

    return pl.pallas_call(
        _swa_sample_kernel,
        grid=(batch,),
        in_specs=[
            pl.BlockSpec((seq, A_WIDTH), rows),
            pl.BlockSpec((seq, A_KV_WIDTH), rows),
            pl.BlockSpec((1, lc, A_KV_WIDTH), lambda b: (b, 0, 0)),
            pl.BlockSpec((seq, A_KV_WIDTH), rows),
            pl.BlockSpec((1, lc, A_KV_WIDTH), lambda b: (b, 0, 0)),
            pl.BlockSpec(bias.shape, lambda b: (0, 0, 0)),
            pl.BlockSpec(sink.shape, lambda b: (0, 0, 0)),
        ],
        out_specs=pl.BlockSpec((seq, A_WIDTH), lambda b: (b, 0)),
        out_shape=jax.ShapeDtypeStruct((batch * seq, A_WIDTH), BF16),
        compiler_params=_cparams(("parallel",)),
        name="swa_sample",
    )(q, k, k_cache, v, v_cache, bias, sink)


def _split3(x):
    hi = x.astype(BF16)
    r1 = x - hi.astype(F32)
    mid = r1.astype(BF16)
    lo = (r1 - mid.astype(F32)).astype(BF16)
    return hi, mid, lo


def _dot3(mat, parts):
    acc = jnp.dot(mat, parts[0], preferred_element_type=F32)
    acc = acc + jnp.dot(mat, parts[1], preferred_element_type=F32)
    return acc + jnp.dot(mat, parts[2], preferred_element_type=F32)


def _hgrn_kernel(zq_ref, zf_ref, zi_ref, zg_ref, lb_ref, ng_ref, tri_ref, tsel_ref, s0_ref, o_ref, sfin_ref, s_scr,
                 *, blk):
    j = pl.program_id(1)

    @pl.when(j == 0)
    def _():
        s_scr[...] = s0_ref[0]

    bq = zq_ref[...]
    fl = zf_ref[...]
    v = zi_ref[...]
    bg = zg_ref[...]
    log_lb = lb_ref[0:1, :]
    log1m_lb = lb_ref[1:2, :]
    one_m_lb = lb_ref[2:3, :]

    q = bq * _sigmoid(bq) * (B_KEY_DIM ** -0.5)
    log_sig = jnp.minimum(fl, 0.0) - jnp.log1p(jnp.exp(-jnp.abs(fl)))
    c = log1m_lb + log_sig
    logf = jnp.maximum(log_lb, c) + jnp.log1p(jnp.exp(-jnp.abs(log_lb - c)))
    k = one_m_lb * _sigmoid(-fl)

    parts = _split3(logf)
    b = _dot3(tri_ref[...], parts)
    rq = _dot3(tsel_ref[...], parts)
    qt = q * jnp.exp(b - rq)
    qb = (q * jnp.exp(b)).astype(BF16)
    b_last = b[blk - 1:blk, :]
    khat = (k * jnp.exp(b_last - b)).astype(BF16)
    e_last = jnp.exp(b_last)
    vb = v.astype(BF16)
    row = lax.broadcasted_iota(jnp.int32, (blk, 1), 0)
    n_sub = blk // SUB
    kts = []
    for i in range(n_sub):
        r_i = rq[i * SUB:i * SUB + 1, :]
        kts.append(jnp.where(row < (i + 1) * SUB, k * jnp.exp(r_i - b), 0.0).astype(BF16))
    qt = qt.astype(BF16)
    tril = lax.broadcasted_iota(jnp.int32, (blk, blk), 0) >= lax.broadcasted_iota(jnp.int32, (blk, blk), 1)

    outs = []
    for h in range(B_HEADS):
        hs = slice(h * B_KEY_DIM, (h + 1) * B_KEY_DIM)
        a = jnp.concatenate(
            [lax.dot_general(qt[i * SUB:(i + 1) * SUB, hs], kts[i][:, hs], (((1,), (1,)), ((), ())),
                             preferred_element_type=F32) for i in range(n_sub)], axis=0)
        a = jnp.where(tril, a, 0.0).astype(BF16)
        s_h = s_scr[h]
        o = jnp.dot(a, vb[:, hs], preferred_element_type=F32)
        o = o + jnp.dot(qb[:, hs], s_h.astype(BF16), preferred_element_type=F32)
        ds = lax.dot_general(khat[:, hs], vb[:, hs], (((0,), (0,)), ((), ())), preferred_element_type=F32)
        decay = jnp.transpose(jnp.broadcast_to(e_last[:, hs], (B_KEY_DIM, B_KEY_DIM)))
        s_scr[h] = decay * s_h + ds
        o = o * lax.rsqrt(jnp.mean(o * o, axis=-1, keepdims=True) + RMS_EPS)
        outs.append(o)
    o_all = jnp.concatenate(outs, axis=1) * ng_ref[...] * (bg * _sigmoid(bg))
    o_ref[...] = o_all.astype(o_ref.dtype)

    @pl.when(j == pl.num_programs(1) - 1)
    def _():
        sfin_ref[0] = s_scr[...]


def _hgrn(zb, lbp, ng, s0, *, row0, batch, seq, blk):
    nb = seq // blk
    blk0 = row0 // blk
    t = np.arange(blk)
    tri = jnp.asarray((t[:, None] >= t[None, :]).astype(np.float32), dtype=BF16)
    tsel = jnp.asarray((t[None, :] < (t[:, None] // SUB) * SUB).astype(np.float32), dtype=BF16)

    def zspec(col):
        return pl.BlockSpec((blk, B_WIDTH), lambda b, j: (blk0 + b * nb + j, col))

    state_spec = pl.BlockSpec((1, B_HEADS, B_KEY_DIM, B_VAL_DIM), lambda b, j: (b, 0, 0, 0))
    return pl.pallas_call(
        functools.partial(_hgrn_kernel, blk=blk),
        grid=(batch, nb),
        in_specs=[
            zspec(0), zspec(1), zspec(2), zspec(3),
            pl.BlockSpec((3, B_WIDTH), lambda b, j: (0, 0)),
            pl.BlockSpec((1, B_WIDTH), lambda b, j: (0, 0)),
            pl.BlockSpec((blk, blk), lambda b, j: (0, 0)),
            pl.BlockSpec((blk, blk), lambda b, j: (0, 0)),
            state_spec,
        ],
        out_specs=[pl.BlockSpec((blk, B_WIDTH), lambda b, j: (b * nb + j, 0)), state_spec],
        out_shape=[jax.ShapeDtypeStruct((batch * seq, B_WIDTH), BF16),
                   jax.ShapeDtypeStruct((batch, B_HEADS, B_KEY_DIM, B_VAL_DIM), F32)],
        scratch_shapes=[pltpu.VMEM((B_HEADS, B_KEY_DIM, B_VAL_DIM), F32)],
        compiler_params=_cparams(("parallel", "arbitrary")),
        name=f"hgrn_blk{blk}",
    )(zb, zb, zb, zb, lbp, ng, tri, tsel, s0)


def _cattn_kernel(q_ref, mk_ref, mv_ref, o_ref):
    mk = mk_ref[0].astype(BF16)
    mv = mv_ref[0].astype(BF16)
    q = q_ref[...]
    for h in range(C_HEADS):
        hs = slice(h * C_HEAD_DIM, (h + 1) * C_HEAD_DIM)
        s = lax.dot_general(q[:, hs], mk[:, hs], (((1,), (1,)), ((), ())), preferred_element_type=F32)
        p = jnp.exp(s - jnp.max(s, axis=-1, keepdims=True))
        p = p / jnp.sum(p, axis=-1, keepdims=True)
        o_ref[:, hs] = jnp.dot(p.astype(BF16), mv[:, hs], preferred_element_type=F32).astype(o_ref.dtype)


def _cattn(q, mk, mv, *, row0, batch, seq, tq):
    nb = seq // tq
    blk0 = row0 // tq
    mem_spec = pl.BlockSpec((1, MEM_LEN, C_WIDTH), lambda b, i: (b, 0, 0))
    return pl.pallas_call(
        _cattn_kernel,
        grid=(batch, nb),
        in_specs=[pl.BlockSpec((tq, C_WIDTH), lambda b, i: (blk0 + b * nb + i, 0)), mem_spec, mem_spec],
        out_specs=pl.BlockSpec((tq, C_WIDTH), lambda b, i: (b * nb + i, 0)),
        out_shape=jax.ShapeDtypeStruct((batch * seq, C_WIDTH), BF16),
        compiler_params=_cparams(("parallel", "parallel")),
        name=f"cattn_tq{tq}",
    )(q, mk, mv)


def _two_segment_specs(block, n_first, n_second, col=0):
    first = pl.BlockSpec(block, lambda i: (jnp.minimum(i, n_first - 1), col))
    second = pl.BlockSpec(block, lambda i: (jnp.clip(i - n_first, 0, n_second - 1), col))
    return [first, second]


def _merge_kernel(ap_ref, as_ref, bp_ref, bs_ref, cp_ref, cs_ref, g0p_ref, g0s_ref, g1p_ref, g1s_ref, g2p_ref, g2s_ref,
                  pa_ref, pb_ref, pc_ref, o_ref, *, n_first):
    first = pl.program_id(0) < n_first

    def pick(p_ref, s_ref):
        return jnp.where(first, p_ref[...], s_ref[...])

    h = _sigmoid(pick(g0p_ref, g0s_ref)) * jnp.dot(pick(ap_ref, as_ref), pa_ref[...], preferred_element_type=F32)
    h = h + _sigmoid(pick(g1p_ref, g1s_ref)) * jnp.dot(pick(bp_ref, bs_ref), pb_ref[...], preferred_element_type=F32)
    h = h + _sigmoid(pick(g2p_ref, g2s_ref)) * jnp.dot(pick(cp_ref, cs_ref), pc_ref[...], preferred_element_type=F32)
    o_ref[...] = h.astype(o_ref.dtype)


def _merge(abc_p, abc_s, gl_p, gl_s, p, *, tm):
    n_p, width = abc_p[0].shape
    n_s = abc_s[0].shape[0]
    nf, ns = n_p // tm, n_s // tm
    in_specs = []
    args = []
    for x_p, x_s in zip(abc_p, abc_s):
        in_specs += _two_segment_specs((tm, width), nf, ns)
        args += [x_p, x_s]
    for col in range(3):
        in_specs += _two_segment_specs((tm, D_MODEL), nf, ns, col)
        args += [gl_p, gl_s]
    for blk in range(3):
        in_specs.append(_resident((width, D_MODEL), functools.partial(lambda i, b: (b, 0), b=blk)))
        args.append(p)
    return pl.pallas_call(
        functools.partial(_merge_kernel, n_first=nf),
        grid=(nf + ns,),
        in_specs=in_specs,
        out_specs=pl.BlockSpec((tm, D_MODEL), lambda i: (i, 0)),
        out_shape=jax.ShapeDtypeStruct((n_p + n_s, D_MODEL), BF16),
        compiler_params=_cparams(("parallel",)),
        name="merge",
    )(*args)


def _first_index(hit_src, m, iota, size, axis):
    return jnp.min(jnp.where(hit_src == m, iota, size), axis=axis, keepdims=True)


def _route(x1, wr_ref, rb_ref, upper_ref, run_ref, idx_ref, wts_ref, pos_ref):
    tm = x1.shape[0]
    logits = lax.dot_general(wr_ref[...], x1, (((1,), (1,)), ((), ())), precision=lax.Precision.HIGHEST,
                             preferred_element_type=F32)
    scores = _sigmoid(logits)
    choice = scores + rb_ref[...]
    g3 = choice.reshape(N_GROUPS, GROUP_SIZE, tm)
    mem_iota = lax.broadcasted_iota(jnp.int32, g3.shape, 1)
    m1 = jnp.max(g3, axis=1, keepdims=True)
    first = _first_index(g3, m1, mem_iota, GROUP_SIZE, 1)
    m2 = jnp.max(jnp.where(mem_iota == first, -jnp.inf, g3), axis=1, keepdims=True)
    gscore = (m1 + m2).reshape(N_GROUPS, tm)
    g_iota = lax.broadcasted_iota(jnp.int32, gscore.shape, 0)
    gsel = jnp.zeros(gscore.shape, F32)
    cur = gscore
    for _ in range(TOPK_GROUPS):
        m = jnp.max(cur, axis=0, keepdims=True)
        hit = g_iota == _first_index(cur, m, g_iota, N_GROUPS, 0)
        gsel = jnp.where(hit, 1.0, gsel)
        cur = jnp.where(hit, -jnp.inf, cur)
    masked = jnp.where(gsel.reshape(N_GROUPS, 1, tm) > 0.5, g3, -jnp.inf).reshape(N_EXPERTS, tm)
    e_iota = lax.broadcasted_iota(jnp.int32, masked.shape, 0)
    sel = jnp.zeros(masked.shape, F32)
    cur = masked
    idxs, ws = [], []
    for _ in range(TOP_K):
        m = jnp.max(cur, axis=0, keepdims=True)
        ei = _first_index(cur, m, e_iota, N_EXPERTS, 0)
        hit = e_iota == ei
        idxs.append(ei)
        ws.append(jnp.sum(jnp.where(hit, scores, 0.0), axis=0, keepdims=True))
        sel = jnp.where(hit, 1.0, sel)
        cur = jnp.where(hit, -jnp.inf, cur)
    w = jnp.concatenate(ws, axis=0)
    w = w / jnp.sum(w, axis=0, keepdims=True) * ROUTED_SCALE
    excl = jnp.dot(sel.astype(BF16), upper_ref[...], preferred_element_type=F32)
    posfull = excl + run_ref[:, 0:1]
    run_ref[...] = run_ref[...] + jnp.sum(sel, axis=1, keepdims=True)
    pos = [jnp.sum(jnp.where(e_iota == ei, posfull, 0.0), axis=0, keepdims=True) for ei in idxs]
    idx_ref[0] = jnp.concatenate(idxs, axis=0)
    wts_ref[0] = w
    pos_ref[0] = jnp.concatenate(pos, axis=0).astype(jnp.int32)


def _pack_halves(x):
    c = x.shape[1] // 2
    lo = lax.bitcast_convert_type(x[:, :c].astype(BF16).astype(F32), jnp.uint32)
    hi = lax.bitcast_convert_type(x[:, c:].astype(BF16).astype(F32), jnp.uint32)
    return (hi & jnp.uint32(0xFFFF0000)) | (lo >> 16)


def _unpack_halves(w):
    lo = lax.bitcast_convert_type(w << 16, F32)
    hi = lax.bitcast_convert_type(w & jnp.uint32(0xFFFF0000), F32)
    return lo, hi


def _out_ln_route_kernel(xp_ref, xs_ref, h_ref, wo_ref, g_ref, b_ref, wr_ref, rb_ref, upper_ref,
                         x1_ref, x1p_ref, idx_ref, wts_ref, pos_ref, cnt_ref, run_ref, *, n_first):
    i = pl.program_id(0)

    @pl.when(i == 0)
    def _():
        run_ref[...] = jnp.zeros_like(run_ref)

    x = jnp.where(i < n_first, xp_ref[...], xs_ref[...])
    y = DN_ALPHA * x + jnp.dot(h_ref[...], wo_ref[...], preferred_element_type=F32)
    x1 = _layer_norm(y, g_ref[...], b_ref[...])
    x1_ref[...] = x1
    x1p_ref[...] = _pack_halves(x1)
    _route(x1, wr_ref, rb_ref, upper_ref, run_ref, idx_ref, wts_ref, pos_ref)
    cnt_ref[...] = run_ref[...].astype(jnp.int32)


def _out_ln_route(x_p, x_s, h, w_out, ln_g, ln_b, w_router_t, router_bias, *, tm):
    n = h.shape[0]
    nt = n // tm
    nf, ns = x_p.shape[0] // tm, x_s.shape[0] // tm
    t = np.arange(tm)
    upper = jnp.asarray((t[:, None] < t[None, :]).astype(np.float32), dtype=BF16)

    def rows(i):
        return (i, 0)

    def const(i):
        return (0, 0)

    small = pl.BlockSpec((1, TOP_K, tm), lambda i: (i, 0, 0))
    return pl.pallas_call(
        functools.partial(_out_ln_route_kernel, n_first=nf),
        grid=(nt,),
        in_specs=_two_segment_specs((tm, D_MODEL), nf, ns) + [
            pl.BlockSpec((tm, D_MODEL), rows),
            _resident((D_MODEL, D_MODEL), const),
            pl.BlockSpec((1, D_MODEL), const),
            pl.BlockSpec((1, D_MODEL), const),
            pl.BlockSpec((N_EXPERTS, D_MODEL), const),
            pl.BlockSpec((N_EXPERTS, 1), const),
            pl.BlockSpec((tm, tm), const),
        ],
        out_specs=[pl.BlockSpec((tm, D_MODEL), rows), pl.BlockSpec((tm, D_MODEL // 2), rows), small, small, small,
                   pl.BlockSpec((N_EXPERTS, 128), const)],
        out_shape=[
            jax.ShapeDtypeStruct((n, D_MODEL), F32),
            jax.ShapeDtypeStruct((n, D_MODEL // 2), jnp.uint32),
            jax.ShapeDtypeStruct((nt, TOP_K, tm), jnp.int32),
            jax.ShapeDtypeStruct((nt, TOP_K, tm), F32),
            jax.ShapeDtypeStruct((nt, TOP_K, tm), jnp.int32),
            jax.ShapeDtypeStruct((N_EXPERTS, 128), jnp.int32),
        ],
        scratch_shapes=[pltpu.VMEM((N_EXPERTS, 128), F32)],
        compiler_params=_cparams(("arbitrary",)),
        name="out_ln_route",
    )(x_p, x_s, h, w_out, ln_g, ln_b, w_router_t, router_bias, upper)


def _row_copy(src_hbm, row, dst, dst_row, sem):
    return pltpu.make_async_copy(src_hbm.at[pl.ds(row, 1)], dst.at[pl.ds(dst_row, 1)], sem)


def _moe_kernel(be_ref, bv_ref, tok_a_ref, tok_b_ref, tok_n_ref, x_hbm, wga_ref, wua_ref, wda_ref, wgb_ref, wub_ref,
                wdb_ref, o_ref, buf_a, buf_b, sem_a, sem_b):
    j = pl.program_id(0)
    m = buf_a.shape[0]
    valid_a = bv_ref[2 * j] != 0
    valid_b = bv_ref[2 * j + 1] != 0

    def start_gather(tok_ref, buf, sem):
        for r in range(m):
            _row_copy(x_hbm, tok_ref[0, 0, r], buf, r, sem).start()

    def wait_gather(buf, sem):
        pltpu.make_async_copy(x_hbm.at[pl.ds(0, m)], buf, sem).wait()

    def compute(buf, wg_ref, wu_ref, wd_ref, row0):
        lo, hi = _unpack_halves(buf[...])
        x = jnp.concatenate([lo.astype(BF16), hi.astype(BF16)], axis=1)
        g = jnp.dot(x, wg_ref[0], preferred_element_type=F32)
        u = jnp.dot(x, wu_ref[0], preferred_element_type=F32)
        hmid = (g * _sigmoid(g) * u).astype(BF16)
        y = jnp.dot(hmid, wd_ref[0], preferred_element_type=F32)
        o_ref[row0:row0 + m, :] = _pack_halves(y)

    @pl.when(jnp.logical_and(j == 0, valid_a))
    def _():
        start_gather(tok_a_ref, buf_a, sem_a)

    @pl.when(valid_a)
    def _():
        wait_gather(buf_a, sem_a)
        start_gather(tok_b_ref, buf_b, sem_b)
        compute(buf_a, wga_ref, wua_ref, wda_ref, 0)

    @pl.when(valid_b)
    def _():
        wait_gather(buf_b, sem_b)
        start_gather(tok_n_ref, buf_a, sem_a)
        compute(buf_b, wgb_ref, wub_ref, wdb_ref, m)

    @pl.when(jnp.logical_and(valid_a, jnp.logical_not(valid_b)))
    def _():
        wait_gather(buf_b, sem_b)
        o_ref[m:2 * m, :] = jnp.zeros((m, o_ref.shape[1]), o_ref.dtype)

    @pl.when(jnp.logical_not(valid_a))
    def _():
        o_ref[...] = jnp.zeros_like(o_ref)

    prev_valid = jnp.logical_and(j > 0, bv_ref[jnp.maximum(2 * j - 1, 0)] != 0)

    @pl.when(jnp.logical_and(jnp.logical_not(valid_a), prev_valid))
    def _():
        wait_gather(buf_a, sem_a)


def _moe(x1p, blk_e, blk_valid, buf_t, w_gate, w_up, w_down):
    n_blocks = blk_e.shape[0]
    m = MOE_BLOCK
    assert n_blocks % 2 == 0
    half = D_MODEL // 2

    def wspec(shape, which):
        return pl.BlockSpec(shape, lambda j, be, bv: (be[2 * j + which], 0, 0))

    def tspec(index):
        return pl.BlockSpec((1, 1, m), lambda j, be, bv: (index(j), 0, 0), memory_space=pltpu.SMEM)

    w_specs = [wspec((1, D_MODEL, EXPERT_DIM), w) for w in (0, 0)] + [wspec((1, EXPERT_DIM, D_MODEL), 0)]
    w_specs += [wspec((1, D_MODEL, EXPERT_DIM), w) for w in (1, 1)] + [wspec((1, EXPERT_DIM, D_MODEL), 1)]
    grid_spec = pltpu.PrefetchScalarGridSpec(
        num_scalar_prefetch=2,
        grid=(n_blocks // 2,),
        in_specs=[
            tspec(lambda j: 2 * j),
            tspec(lambda j: 2 * j + 1),
            tspec(lambda j: jnp.minimum(2 * j + 2, n_blocks - 1)),
            pl.BlockSpec(memory_space=pl.ANY),
        ] + w_specs,
        out_specs=pl.BlockSpec((2 * m, half), lambda j, be, bv: (j, 0)),
        scratch_shapes=[pltpu.VMEM((m, half), jnp.uint32), pltpu.VMEM((m, half), jnp.uint32),
                        pltpu.SemaphoreType.DMA(()), pltpu.SemaphoreType.DMA(())],
    )
    tok = buf_t.reshape(n_blocks, 1, m)
    return pl.pallas_call(
        _moe_kernel,
        grid_spec=grid_spec,
        out_shape=jax.ShapeDtypeStruct((n_blocks * m, half), jnp.uint32),
        compiler_params=_cparams(("arbitrary",)),
        name="moe_experts",
    )(blk_e, blk_valid, tok, tok, tok, x1p, w_gate, w_up, w_down, w_gate, w_up, w_down)


def _final_kernel(dest_ref, x1_ref, w_ref, yb_hbm, sg_ref, su_ref, sd_ref, g_ref, b_ref, o_ref, ybuf, sem):
    tm = x1_ref.shape[0]

    def issue(t, carry):
        for k in range(TOP_K):
            _row_copy(yb_hbm, dest_ref[0, k, t], ybuf.at[k], t, sem).start()
        return carry

    lax.fori_loop(0, tm, issue, 0, unroll=2)
    x1 = x1_ref[...]
    xb = x1.astype(BF16)
    g = jnp.dot(xb, sg_ref[...], preferred_element_type=F32)
    u = jnp.dot(xb, su_ref[...], preferred_element_type=F32)
    y = jnp.dot((g * _sigmoid(g) * u).astype(BF16), sd_ref[...], preferred_element_type=F32)
    w = w_ref[...]
    for k in range(TOP_K):
        pltpu.make_async_copy(yb_hbm.at[pl.ds(0, tm)], ybuf.at[k], sem).wait()
    r_lo = r_hi = None
    for k in range(TOP_K):
        lo, hi = _unpack_halves(ybuf[k])
        wk = w[:, k:k + 1]
        r_lo = wk * lo if r_lo is None else r_lo + wk * lo
        r_hi = wk * hi if r_hi is None else r_hi + wk * hi
    routed = jnp.concatenate([r_lo, r_hi], axis=1)
    o_ref[...] = _layer_norm(DN_ALPHA * x1 + (routed + y), g_ref[...], b_ref[...])


def _final(dest, x1, wts_t, yb, w_sg, w_su, w_sd, ln_g, ln_b, *, tm, row0, rows, name):
    blk0 = row0 // tm

    def tile(i):
        return (blk0 + i, 0)

    def const(i):
        return (0, 0)

    return pl.pallas_call(
        _final_kernel,
        grid=(rows // tm,),
        in_specs=[
            pl.BlockSpec((1, TOP_K, tm), lambda i: (blk0 + i, 0, 0), memory_space=pltpu.SMEM),
            pl.BlockSpec((tm, D_MODEL), tile),
            pl.BlockSpec((tm, TOP_K), tile),
            pl.BlockSpec(memory_space=pl.ANY),
            _resident((D_MODEL, EXPERT_DIM), const),
            _resident((D_MODEL, EXPERT_DIM), const),
            _resident((EXPERT_DIM, D_MODEL), const),
            pl.BlockSpec((1, D_MODEL), const),
            pl.BlockSpec((1, D_MODEL), const),
        ],
        out_specs=pl.BlockSpec((tm, D_MODEL), lambda i: (i, 0)),
        out_shape=jax.ShapeDtypeStruct((rows, D_MODEL), F32),
        scratch_shapes=[pltpu.VMEM((TOP_K, tm, D_MODEL // 2), jnp.uint32), pltpu.SemaphoreType.DMA(())],
        compiler_params=_cparams(("arbitrary",)),
        name=name,
    )(dest, x1, wts_t, yb, w_sg, w_su, w_sd, ln_g, ln_b)


def kernel(x_prompt, x_sample, cache_win_k, cache_win_v, state_hgrn, cache_mem_k, cache_mem_v, mem_prompt,
           w_in, w_mem_kv, a_sinks, b_lb_logits, b_norm_g, w_branch, w_out, ln1_g, ln1_b,
           w_router, router_bias, w_exp_gate, w_exp_up, w_exp_down, w_sh_gate, w_sh_up, w_sh_down, ln2_g, ln2_b):
    assert w_in.shape[0] == DEPTH == 1
    batch, seq, _ = x_prompt.shape
    dbatch, dseq, _ = x_sample.shape
    n_p = batch * seq
    n_s = dbatch * dseq
    n = n_p + n_s
    l = 0

    xp2 = x_prompt.reshape(n_p, D_MODEL)
    xs2 = x_sample.reshape(n_s, D_MODEL)
    win = w_in[l]
    o_k, o_v, o_b = A_WIDTH, A_WIDTH + A_KV_WIDTH, A_WIDTH + 2 * A_KV_WIDTH
    o_c = o_b + 4 * B_WIDTH
    o_g = o_c + C_WIDTH
    groups = [
        ("aq", 0, o_k, 512, BF16, A_HEAD_DIM ** -0.5),
        ("ak", o_k, o_v, A_KV_WIDTH, F32, 1.0),
        ("av", o_v, o_b, A_KV_WIDTH, F32, 1.0),
        ("hgrn", o_b, o_c, 512, F32, 1.0),
        ("cq", o_c, o_g, 512, BF16, C_HEAD_DIM ** -0.5),
        ("gate", o_g, win.shape[1], 512, F32, 1.0),
    ]
    zp, zs = {}, {}
    for gname, c0, c1, tn, dt, scale in groups:
        wslice = win[:, c0:c1].astype(BF16)
        zp[gname] = _matmul(xp2, wslice, tm=1024, tn=tn, out_dtype=dt, scale=scale, name=f"proj_{gname}_p")
        zs[gname] = _matmul(xs2, wslice, tm=n_s, tn=tn, out_dtype=dt, scale=scale, name=f"proj_{gname}_s")

    a_p = _swa_prompt(zp["aq"], zp["ak"], zp["av"], a_sinks[l], batch=batch, seq=seq)
    lc = cache_win_k.shape[2]
    kc = cache_win_k[l].reshape(dbatch, lc, A_KV_WIDTH)
    vc = cache_win_v[l].reshape(dbatch, lc, A_KV_WIDTH)
    a_s = _swa_sample(zs["aq"], zs["ak"], zs["av"], kc, vc, a_sinks[l], row0=0, batch=dbatch, seq=dseq)

    lower = jnp.cumsum(jax.nn.softmax(b_lb_logits.astype(F32), axis=0), axis=0)[l]
    lbp = jnp.stack([jnp.log(lower), jnp.log1p(-lower), 1.0 - lower])
    ng = jnp.tile(b_norm_g[l].astype(F32), B_HEADS).reshape(1, B_WIDTH)
    s_zero = jnp.zeros((batch, B_HEADS, B_KEY_DIM, B_VAL_DIM), F32)
    b_p, hs_p = _hgrn(zp["hgrn"], lbp, ng, s_zero, row0=0, batch=batch, seq=seq, blk=CHUNK)
    b_s, hs_s = _hgrn(zs["hgrn"], lbp, ng, state_hgrn[l].astype(F32), row0=0, batch=dbatch, seq=dseq, blk=dseq)

    mem = mem_prompt.reshape(batch * MEM_LEN, D_MODEL)
    wmem = w_mem_kv[l]
    mk = _matmul(mem, wmem[:, :C_WIDTH].astype(BF16), tm=batch * MEM_LEN, tn=512, out_dtype=F32, name="proj_mem_k")
    mv = _matmul(mem, wmem[:, C_WIDTH:].astype(BF16), tm=batch * MEM_LEN, tn=512, out_dtype=F32, name="proj_mem_v")
    mk = mk.reshape(batch, MEM_LEN, C_WIDTH)
    mv = mv.reshape(batch, MEM_LEN, C_WIDTH)
    c_p = _cattn(zp["cq"], mk, mv, row0=0, batch=batch, seq=seq, tq=512)
    c_s = _cattn(zs["cq"], cache_mem_k[l].reshape(dbatch, MEM_LEN, C_WIDTH),
                 cache_mem_v[l].reshape(dbatch, MEM_LEN, C_WIDTH), row0=0, batch=dbatch, seq=dseq, tq=dseq)

    tm_r = 256
    h = _merge((a_p, b_p, c_p), (a_s, b_s, c_s), zp["gate"], zs["gate"], w_branch[l].astype(BF16), tm=tm_r)
    x1, x1p, idx, wts, pos, cnt = _out_ln_route(
        xp2, xs2, h, w_out[l].astype(BF16), ln1_g[l].reshape(1, D_MODEL), ln1_b[l].reshape(1, D_MODEL),
        w_router[l].T.astype(F32), router_bias[l].reshape(N_EXPERTS, 1).astype(F32), tm=tm_r)

    m = MOE_BLOCK
    n_pairs = n * TOP_K
    n_blocks = (n_pairs + m - 1) // m + N_EXPERTS
    n_blocks += n_blocks % 2
    assert n_blocks * m > n_pairs + N_EXPERTS * (m - 1)
    counts = cnt[:, 0]
    padded = (counts + m - 1) // m * m
    pad_end = jnp.cumsum(padded)
    pad_start = pad_end - padded
    idx_t = jnp.transpose(idx, (0, 2, 1)).reshape(n, TOP_K)
    pos_t = jnp.transpose(pos, (0, 2, 1)).reshape(n, TOP_K)
    wts_t = jnp.transpose(wts, (0, 2, 1)).reshape(n, TOP_K)
    dest = pad_start[idx_t] + pos_t
    tok = jnp.broadcast_to(jnp.arange(n, dtype=jnp.int32)[:, None], (n, TOP_K))
    buf_t = jnp.zeros((n_blocks * m,), jnp.int32).at[dest.reshape(-1)].set(tok.reshape(-1))
    blk_first = jnp.arange(n_blocks, dtype=jnp.int32) * m
    blk_e = jnp.sum((blk_first[:, None] >= pad_end[None, :]).astype(jnp.int32), axis=1)
    blk_e = jnp.minimum(blk_e, N_EXPERTS - 1)
    blk_valid = (blk_first < pad_end[-1]).astype(jnp.int32)

    yb = _moe(x1p, blk_e, blk_valid, buf_t, w_exp_gate[l].astype(BF16), w_exp_up[l].astype(BF16),
              w_exp_down[l].astype(BF16))
    tm_f = 128
    dest_blk = jnp.transpose(dest.reshape(n // tm_f, tm_f, TOP_K), (0, 2, 1)).astype(jnp.int32)
    fin = functools.partial(_final, dest_blk, x1, wts_t, yb, w_sh_gate[l].astype(BF16), w_sh_up[l].astype(BF16),
                            w_sh_down[l].astype(BF16), ln2_g[l].reshape(1, D_MODEL), ln2_b[l].reshape(1, D_MODEL),
                            tm=tm_f)
    y_p = fin(row0=0, rows=n_p, name="combine_shared_ln2_p").reshape(batch, seq, D_MODEL)
    y_s = fin(row0=n_p, rows=n_s, name="combine_shared_ln2_s").reshape(dbatch, dseq, D_MODEL)

    k_p = zp["ak"].reshape(batch, seq, A_KV_HEADS, A_HEAD_DIM)[:, -lc:]
    v_p = zp["av"].reshape(batch, seq, A_KV_HEADS, A_HEAD_DIM)[:, -lc:]
    k_s = zs["ak"].reshape(dbatch, dseq, A_KV_HEADS, A_HEAD_DIM)
    v_s = zs["av"].reshape(dbatch, dseq, A_KV_HEADS, A_HEAD_DIM)
    wk_s = jnp.concatenate([cache_win_k[l].astype(F32), k_s], axis=1)[:, -lc:]
    wv_s = jnp.concatenate([cache_win_v[l].astype(F32), v_s], axis=1)[:, -lc:]
    mk_o = mk.reshape(batch, MEM_LEN, C_HEADS, C_HEAD_DIM)
    mv_o = mv.reshape(batch, MEM_LEN, C_HEADS, C_HEAD_DIM)
    return (y_p, y_s, k_p[None], v_p[None], hs_p[None], mk_o[None], mv_o[None], wk_s[None], wv_s[None], hs_s[None])
```

```python
import functools

import jax
import jax.numpy as jnp
import numpy as np
from jax import lax
from jax.experimental import pallas as pl
from jax.experimental.pallas import tpu as pltpu

F32 = jnp.float32
BF16 = jnp.bfloat16

D_MODEL = 2048
DEPTH = 1
PAST_LEN = 2048
CHUNK = 64
A_HEADS = 16
A_KV_HEADS = 4
A_GROUP = A_HEADS // A_KV_HEADS
A_HEAD_DIM = 64
A_WIDTH = A_HEADS * A_HEAD_DIM
A_KV_WIDTH = A_KV_HEADS * A_HEAD_DIM
WINDOW = 128
WIN_CHUNKS = WINDOW // CHUNK
B_HEADS = 8
B_KEY_DIM = 128
B_VAL_DIM = 128
B_WIDTH = B_HEADS * B_VAL_DIM
SUB = 16
MEM_LEN = 256
C_HEADS = 4
C_HEAD_DIM = 256
C_WIDTH = C_HEADS * C_HEAD_DIM
N_EXPERTS = 64
N_GROUPS = 8
GROUP_SIZE = N_EXPERTS // N_GROUPS
TOPK_GROUPS = 4
TOP_K = 8
EXPERT_DIM = 512
ROUTED_SCALE = 2.5
DN_ALPHA = (2 * DEPTH) ** 0.25
LN_EPS = 1e-5
RMS_EPS = 1e-6

MOE_BLOCK = 256
VMEM_LIMIT = 56 * 1024 * 1024


def _cparams(sem):
    return pltpu.CompilerParams(dimension_semantics=sem, vmem_limit_bytes=VMEM_LIMIT)


def _resident(shape, index_map):
    return pl.BlockSpec(shape, index_map, pipeline_mode=pl.Buffered(1))


def _sigmoid(x):
    return 1.0 / (1.0 + jnp.exp(-x))


def _layer_norm(x, g, b):
    mu = jnp.mean(x, axis=-1, keepdims=True)
    xc = x - mu
    var = jnp.mean(xc * xc, axis=-1, keepdims=True)
    return xc * lax.rsqrt(var + LN_EPS) * g + b


def _mm_kernel(x_ref, w_ref, o_ref, *, scale):
    acc = jnp.dot(x_ref[...].astype(BF16), w_ref[...], preferred_element_type=F32)
    if scale != 1.0:
        acc = acc * scale
    o_ref[...] = acc.astype(o_ref.dtype)


def _matmul(x, w, *, tm, tn, out_dtype, scale=1.0, name):
    m, k = x.shape
    n = w.shape[1]
    return pl.pallas_call(
        functools.partial(_mm_kernel, scale=scale),
        grid=(m // tm, n // tn),
        in_specs=[pl.BlockSpec((tm, k), lambda i, j: (i, 0)), pl.BlockSpec((k, tn), lambda i, j: (0, j))],
        out_specs=pl.BlockSpec((tm, tn), lambda i, j: (i, j)),
        out_shape=jax.ShapeDtypeStruct((m, n), out_dtype),
        compiler_params=_cparams(("parallel", "parallel")),
        name=name,
    )(x, w)


def _swa_heads(q, kband, vband, bias_ref, sink_ref, valid):
    rows = q.shape[0]
    out_pairs = []
    for h in range(A_KV_HEADS):
        q4 = jnp.concatenate(
            [q[:, (A_GROUP * h + g) * A_HEAD_DIM:(A_GROUP * h + g + 1) * A_HEAD_DIM] for g in range(A_GROUP)], axis=0)
        kh = kband[:, h * A_HEAD_DIM:(h + 1) * A_HEAD_DIM]
        vh = vband[:, h * A_HEAD_DIM:(h + 1) * A_HEAD_DIM]
        s = lax.dot_general(q4, kh, (((1,), (1,)), ((), ())), preferred_element_type=F32)
        s = s - bias_ref[h]
        if valid is not None:
            s = jnp.where(valid, s, -jnp.inf)
        sink = sink_ref[h]
        m = jnp.maximum(jnp.max(s, axis=-1, keepdims=True), sink)
        p = jnp.exp(s - m)
        p = p / (jnp.sum(p, axis=-1, keepdims=True) + jnp.exp(sink - m))
        o = jnp.dot(p.astype(BF16), vh, preferred_element_type=F32)
        for g in range(0, A_GROUP, 2):
            out_pairs.append(jnp.concatenate([o[g * rows:(g + 1) * rows], o[(g + 1) * rows:(g + 2) * rows]], axis=1))
    return out_pairs


def _swa_prompt_kernel(q_ref, kc_ref, kp_ref, vc_ref, vp_ref, bias_ref, sink_ref, o_ref, *, n_chunks):
    i = pl.program_id(1)
    pad = WIN_CHUNKS * CHUNK
    n_band = pad + CHUNK
    kb = jnp.concatenate([kp_ref[...], kc_ref[...]], axis=0).astype(BF16)
    vb = jnp.concatenate([vp_ref[...], vc_ref[...]], axis=0).astype(BF16)
    s_idx = lax.broadcasted_iota(jnp.int32, (1, n_band), 1)
    for c in range(n_chunks):
        q = q_ref[c * CHUNK:(c + 1) * CHUNK, :]
        valid = (s_idx + (i * (n_chunks * CHUNK) + c * CHUNK - pad)) >= 0
        pairs = _swa_heads(q, kb[c * CHUNK:c * CHUNK + n_band], vb[c * CHUNK:c * CHUNK + n_band], bias_ref, sink_ref,
                           valid)
        for j, pr in enumerate(pairs):
            o_ref[c * CHUNK:(c + 1) * CHUNK, j * 128:(j + 1) * 128] = pr.astype(o_ref.dtype)


def _swa_sample_kernel(q_ref, kn_ref, kc_ref, vn_ref, vc_ref, bias_ref, sink_ref, o_ref):
    kb = jnp.concatenate([kc_ref[0], kn_ref[...]], axis=0).astype(BF16)
    vb = jnp.concatenate([vc_ref[0], vn_ref[...]], axis=0).astype(BF16)
    pairs = _swa_heads(q_ref[...], kb, vb, bias_ref, sink_ref, None)
    for j, pr in enumerate(pairs):
        o_ref[:, j * 128:(j + 1) * 128] = pr.astype(o_ref.dtype)


def _alibi_slopes():
    return (2.0 ** (-8.0 * np.arange(1, A_HEADS + 1) / A_HEADS)).astype(np.float32)


def _swa_bias(q_pos, k_pos, valid):
    dist = np.abs(q_pos[:, None] - k_pos[None, :]).astype(np.float32)
    slopes = _alibi_slopes().reshape(A_KV_HEADS, A_GROUP)
    bias = slopes[:, :, None, None] * dist[None, None]
    if valid is not None:
        bias = np.where(valid[None, None], bias, np.inf)
    return bias.reshape(A_KV_HEADS, A_GROUP * len(q_pos), len(k_pos)).astype(np.float32)


def _sink_cols(sinks, rows):
    s = sinks.astype(F32).reshape(A_KV_HEADS, A_GROUP, 1)
    return jnp.broadcast_to(s, (A_KV_HEADS, A_GROUP, rows)).reshape(A_KV_HEADS, A_GROUP * rows, 1)


def _swa_prompt(q, k, v, sinks, *, batch, seq, n_chunks=4):
    tq = n_chunks * CHUNK
    pad = WIN_CHUNKS * CHUNK
    nb = seq // tq
    bias = jnp.asarray(_swa_bias(pad + np.arange(CHUNK), np.arange(pad + CHUNK), None))
    sink = _sink_cols(sinks, CHUNK)
    prev_per_blk = tq // pad

    def cur(b, i):
        return (b * nb + i, 0)

    def prev(b, i):
        return (jnp.maximum((b * nb + i) * prev_per_blk - 1, b * nb * prev_per_blk), 0)

    return pl.pallas_call(
        functools.partial(_swa_prompt_kernel, n_chunks=n_chunks),
        grid=(batch, nb),
        in_specs=[
            pl.BlockSpec((tq, A_WIDTH), cur),
            pl.BlockSpec((tq, A_KV_WIDTH), cur),
            pl.BlockSpec((pad, A_KV_WIDTH), prev),
            pl.BlockSpec((tq, A_KV_WIDTH), cur),
            pl.BlockSpec((pad, A_KV_WIDTH), prev),
            pl.BlockSpec(bias.shape, lambda b, i: (0, 0, 0)),
            pl.BlockSpec(sink.shape, lambda b, i: (0, 0, 0)),
        ],
        out_specs=pl.BlockSpec((tq, A_WIDTH), cur),
        out_shape=jax.ShapeDtypeStruct((batch * seq, A_WIDTH), BF16),
        compiler_params=_cparams(("parallel", "arbitrary")),
        name="swa_prompt",
    )(q, k, k, v, v, bias, sink)


def _swa_sample(q, k, v, k_cache, v_cache, sinks, *, row0, batch, seq):
    lc = k_cache.shape[1]
    q_pos = PAST_LEN + np.arange(seq)
    k_pos = PAST_LEN - lc + np.arange(lc + seq)
    cdiff = q_pos[:, None] // CHUNK - k_pos[None, :] // CHUNK
    valid = (cdiff >= 0) & (cdiff <= WIN_CHUNKS)
    bias = jnp.asarray(_swa_bias(q_pos, k_pos, valid))
    sink = _sink_cols(sinks, seq)
    blk0 = row0 // seq

    def rows(b):
        return (blk0 + b, 0)

    return pl.pallas_call(
        _swa_sample_kernel,
        grid=(batch,),
        in_specs=[
            pl.BlockSpec((seq, A_WIDTH), rows),
            pl.BlockSpec((seq, A_KV_WIDTH), rows),
            pl.BlockSpec((1, lc, A_KV_WIDTH), lambda b: (b, 0, 0)),
            pl.BlockSpec((seq, A_KV_WIDTH), rows),
            pl.BlockSpec((1, lc, A_KV_WIDTH), lambda b: (b, 0, 0)),
            pl.BlockSpec(bias.shape, lambda b: (0, 0, 0)),
            pl.BlockSpec(sink.shape, lambda b: (0, 0, 0)),
        ],
        out_specs=pl.BlockSpec((seq, A_WIDTH), lambda b: (b, 0)),
        out_shape=jax.ShapeDtypeStruct((batch * seq, A_WIDTH), BF16),
        compiler_params=_cparams(("parallel",)),
        name="swa_sample",
    )(q, k, k_cache, v, v_cache, bias, sink)


def _split3(x):
    hi = x.astype(BF16)
    r1 = x - hi.astype(F32)
    mid = r1.astype(BF16)
    lo = (r1 - mid.astype(F32)).astype(BF16)
    return hi, mid, lo


def _dot3(mat, parts):
    acc = jnp.dot(mat, parts[0], preferred_element_type=F32)
    acc = acc + jnp.dot(mat, parts[1], preferred_element_type=F32)
    return acc + jnp.dot(mat, parts[2], preferred_element_type=F32)


def _hgrn_kernel(zq_ref, zf_ref, zi_ref, zg_ref, lb_ref, ng_ref, tri_ref, tsel_ref, s0_ref, o_ref, sfin_ref, s_scr,
                 *, blk):
    j = pl.program_id(1)

    @pl.when(j == 0)
    def _():
        s_scr[...] = s0_ref[0]

    bq = zq_ref[...]
    fl = zf_ref[...]
    v = zi_ref[...]
    bg = zg_ref[...]
    log_lb = lb_ref[0:1, :]
    log1m_lb = lb_ref[1:2, :]
    one_m_lb = lb_ref[2:3, :]

    q = bq * _sigmoid(bq) * (B_KEY_DIM ** -0.5)
    log_sig = jnp.minimum(fl, 0.0) - jnp.log1p(jnp.exp(-jnp.abs(fl)))
    c = log1m_lb + log_sig
    logf = jnp.maximum(log_lb, c) + jnp.log1p(jnp.exp(-jnp.abs(log_lb - c)))
    k = one_m_lb * _sigmoid(-fl)

    parts = _split3(logf)
    b = _dot3(tri_ref[...], parts)
    rq = _dot3(tsel_ref[...], parts)
    qt = q * jnp.exp(b - rq)
    qb = (q * jnp.exp(b)).astype(BF16)
    b_last = b[blk - 1:blk, :]
    khat = (k * jnp.exp(b_last - b)).astype(BF16)
    e_last = jnp.exp(b_last)
    vb = v.astype(BF16)
    row = lax.broadcasted_iota(jnp.int32, (blk, 1), 0)
    n_sub = blk // SUB
    kts = []
    for i in range(n_sub):
        r_i = rq[i * SUB:i * SUB + 1, :]
        kts.append(jnp.where(row < (i + 1) * SUB, k * jnp.exp(r_i - b), 0.0).astype(BF16))
    qt = qt.astype(BF16)
    tril = lax.broadcasted_iota(jnp.int32, (blk, blk), 0) >= lax.broadcasted_iota(jnp.int32, (blk, blk), 1)

    outs = []
    for h in range(B_HEADS):
        hs = slice(h * B_KEY_DIM, (h + 1) * B_KEY_DIM)
        a = jnp.concatenate(
            [lax.dot_general(qt[i * SUB:(i + 1) * SUB, hs], kts[i][:, hs], (((1,), (1,)), ((), ())),
                             preferred_element_type=F32) for i in range(n_sub)], axis=0)
        a = jnp.where(tril, a, 0.0).astype(BF16)
        s_h = s_scr[h]
        o = jnp.dot(a, vb[:, hs], preferred_element_type=F32)
        o = o + jnp.dot(qb[:, hs], s_h.astype(BF16), preferred_element_type=F32)
        ds = lax.dot_general(khat[:, hs], vb[:, hs], (((0,), (0,)), ((), ())), preferred_element_type=F32)
        decay = jnp.transpose(jnp.broadcast_to(e_last[:, hs], (B_KEY_DIM, B_KEY_DIM)))
        s_scr[h] = decay * s_h + ds
        o = o * lax.rsqrt(jnp.mean(o * o, axis=-1, keepdims=True) + RMS_EPS)
        outs.append(o)
    o_all = jnp.concatenate(outs, axis=1) * ng_ref[...] * (bg * _sigmoid(bg))
    o_ref[...] = o_all.astype(o_ref.dtype)

    @pl.when(j == pl.num_programs(1) - 1)
    def _():
        sfin_ref[0] = s_scr[...]


def _hgrn(zb, lbp, ng, s0, *, row0, batch, seq, blk):
    nb = seq // blk
    blk0 = row0 // blk
    t = np.arange(blk)
    tri = jnp.asarray((t[:, None] >= t[None, :]).astype(np.float32), dtype=BF16)
    tsel = jnp.asarray((t[None, :] < (t[:, None] // SUB) * SUB).astype(np.float32), dtype=BF16)

    def zspec(col):
        return pl.BlockSpec((blk, B_WIDTH), lambda b, j: (blk0 + b * nb + j, col))

    state_spec = pl.BlockSpec((1, B_HEADS, B_KEY_DIM, B_VAL_DIM), lambda b, j: (b, 0, 0, 0))
    return pl.pallas_call(
        functools.partial(_hgrn_kernel, blk=blk),
        grid=(batch, nb),
        in_specs=[
            zspec(0), zspec(1), zspec(2), zspec(3),
            pl.BlockSpec((3, B_WIDTH), lambda b, j: (0, 0)),
            pl.BlockSpec((1, B_WIDTH), lambda b, j: (0, 0)),
            pl.BlockSpec((blk, blk), lambda b, j: (0, 0)),
            pl.BlockSpec((blk, blk), lambda b, j: (0, 0)),
            state_spec,
        ],
        out_specs=[pl.BlockSpec((blk, B_WIDTH), lambda b, j: (b * nb + j, 0)), state_spec],
        out_shape=[jax.ShapeDtypeStruct((batch * seq, B_WIDTH), BF16),
                   jax.ShapeDtypeStruct((batch, B_HEADS, B_KEY_DIM, B_VAL_DIM), F32)],
        scratch_shapes=[pltpu.VMEM((B_HEADS, B_KEY_DIM, B_VAL_DIM), F32)],
        compiler_params=_cparams(("parallel", "arbitrary")),
        name=f"hgrn_blk{blk}",
    )(zb, zb, zb, zb, lbp, ng, tri, tsel, s0)


def _cattn_kernel(q_ref, mk_ref, mv_ref, o_ref):
    mk = mk_ref[0].astype(BF16)
    mv = mv_ref[0].astype(BF16)
    q = q_ref[...]
    for h in range(C_HEADS):
        hs = slice(h * C_HEAD_DIM, (h + 1) * C_HEAD_DIM)
        s = lax.dot_general(q[:, hs], mk[:, hs], (((1,), (1,)), ((), ())), preferred_element_type=F32)
        p = jnp.exp(s - jnp.max(s, axis=-1, keepdims=True))
        p = p / jnp.sum(p, axis=-1, keepdims=True)
        o_ref[:, hs] = jnp.dot(p.astype(BF16), mv[:, hs], preferred_element_type=F32).astype(o_ref.dtype)


def _cattn(q, mk, mv, *, row0, batch, seq, tq):
    nb = seq // tq
    blk0 = row0 // tq
    mem_spec = pl.BlockSpec((1, MEM_LEN, C_WIDTH), lambda b, i: (b, 0, 0))
    return pl.pallas_call(
        _cattn_kernel,
        grid=(batch, nb),
        in_specs=[pl.BlockSpec((tq, C_WIDTH), lambda b, i: (blk0 + b * nb + i, 0)), mem_spec, mem_spec],
        out_specs=pl.BlockSpec((tq, C_WIDTH), lambda b, i: (b * nb + i, 0)),
        out_shape=jax.ShapeDtypeStruct((batch * seq, C_WIDTH), BF16),
        compiler_params=_cparams(("parallel", "parallel")),
        name=f"cattn_tq{tq}",
    )(q, mk, mv)


def _two_segment_specs(block, n_first, n_second, col=0):
    first = pl.BlockSpec(block, lambda i: (jnp.minimum(i, n_first - 1), col))
    second = pl.BlockSpec(block, lambda i: (jnp.clip(i - n_first, 0, n_second - 1), col))
    return [first, second]


def _merge_kernel(ap_ref, as_ref, bp_ref, bs_ref, cp_ref, cs_ref, g0p_ref, g0s_ref, g1p_ref, g1s_ref, g2p_ref, g2s_ref,
                  pa_ref, pb_ref, pc_ref, o_ref, *, n_first):
    first = pl.program_id(0) < n_first

    def pick(p_ref, s_ref):
        return jnp.where(first, p_ref[...], s_ref[...])

    h = _sigmoid(pick(g0p_ref, g0s_ref)) * jnp.dot(pick(ap_ref, as_ref), pa_ref[...], preferred_element_type=F32)
    h = h + _sigmoid(pick(g1p_ref, g1s_ref)) * jnp.dot(pick(bp_ref, bs_ref), pb_ref[...], preferred_element_type=F32)
    h = h + _sigmoid(pick(g2p_ref, g2s_ref)) * jnp.dot(pick(cp_ref, cs_ref), pc_ref[...], preferred_element_type=F32)
    o_ref[...] = h.astype(o_ref.dtype)


def _merge(abc_p, abc_s, gl_p, gl_s, p, *, tm):
    n_p, width = abc_p[0].shape
    n_s = abc_s[0].shape[0]
    nf, ns = n_p // tm, n_s // tm
    in_specs = []
    args = []
    for x_p, x_s in zip(abc_p, abc_s):
        in_specs += _two_segment_specs((tm, width), nf, ns)
        args += [x_p, x_s]
    for col in range(3):
        in_specs += _two_segment_specs((tm, D_MODEL), nf, ns, col)
        args += [gl_p, gl_s]
    for blk in range(3):
        in_specs.append(_resident((width, D_MODEL), functools.partial(lambda i, b: (b, 0), b=blk)))
        args.append(p)
    return pl.pallas_call(
        functools.partial(_merge_kernel, n_first=nf),
        grid=(nf + ns,),
        in_specs=in_specs,
        out_specs=pl.BlockSpec((tm, D_MODEL), lambda i: (i, 0)),
        out_shape=jax.ShapeDtypeStruct((n_p + n_s, D_MODEL), BF16),
        compiler_params=_cparams(("parallel",)),
        name="merge",
    )(*args)


def _first_index(hit_src, m, iota, size, axis):
    return jnp.min(jnp.where(hit_src == m, iota, size), axis=axis, keepdims=True)


def _route(x1, wr_ref, rb_ref, upper_ref, run_ref, idx_ref, wts_ref, pos_ref):
    tm = x1.shape[0]
    logits = lax.dot_general(wr_ref[...], x1, (((1,), (1,)), ((), ())), precision=lax.Precision.HIGHEST,
                             preferred_element_type=F32)
    scores = _sigmoid(logits)
    choice = scores + rb_ref[...]
    g3 = choice.reshape(N_GROUPS, GROUP_SIZE, tm)
    mem_iota = lax.broadcasted_iota(jnp.int32, g3.shape, 1)
    m1 = jnp.max(g3, axis=1, keepdims=True)
    first = _first_index(g3, m1, mem_iota, GROUP_SIZE, 1)
    m2 = jnp.max(jnp.where(mem_iota == first, -jnp.inf, g3), axis=1, keepdims=True)
    gscore = (m1 + m2).reshape(N_GROUPS, tm)
    g_iota = lax.broadcasted_iota(jnp.int32, gscore.shape, 0)
    gsel = jnp.zeros(gscore.shape, F32)
    cur = gscore
    for _ in range(TOPK_GROUPS):
        m = jnp.max(cur, axis=0, keepdims=True)
        hit = g_iota == _first_index(cur, m, g_iota, N_GROUPS, 0)
        gsel = jnp.where(hit, 1.0, gsel)
        cur = jnp.where(hit, -jnp.inf, cur)
    masked = jnp.where(gsel.reshape(N_GROUPS, 1, tm) > 0.5, g3, -jnp.inf).reshape(N_EXPERTS, tm)
    e_iota = lax.broadcasted_iota(jnp.int32, masked.shape, 0)
    sel = jnp.zeros(masked.shape, F32)
    cur = masked
    idxs, ws = [], []
    for _ in range(TOP_K):
        m = jnp.max(cur, axis=0, keepdims=True)
        ei = _first_index(cur, m, e_iota, N_EXPERTS, 0)
        hit = e_iota == ei
        idxs.append(ei)
        ws.append(jnp.sum(jnp.where(hit, scores, 0.0), axis=0, keepdims=True))
        sel = jnp.where(hit, 1.0, sel)
        cur = jnp.where(hit, -jnp.inf, cur)
    w = jnp.concatenate(ws, axis=0)
    w = w / jnp.sum(w, axis=0, keepdims=True) * ROUTED_SCALE
    excl = jnp.dot(sel.astype(BF16), upper_ref[...], preferred_element_type=F32)
    posfull = excl + run_ref[:, 0:1]
    run_ref[...] = run_ref[...] + jnp.sum(sel, axis=1, keepdims=True)
    pos = [jnp.sum(jnp.where(e_iota == ei, posfull, 0.0), axis=0, keepdims=True) for ei in idxs]
    idx_ref[0] = jnp.concatenate(idxs, axis=0)
    wts_ref[0] = w
    pos_ref[0] = jnp.concatenate(pos, axis=0).astype(jnp.int32)


def _pack_halves(x):
    c = x.shape[1] // 2
    lo = lax.bitcast_convert_type(x[:, :c].astype(BF16).astype(F32), jnp.uint32)
    hi = lax.bitcast_convert_type(x[:, c:].astype(BF16).astype(F32), jnp.uint32)
    return (hi & jnp.uint32(0xFFFF0000)) | (lo >> 16)


def _unpack_halves(w):
    lo = lax.bitcast_convert_type(w << 16, F32)
    hi = lax.bitcast_convert_type(w & jnp.uint32(0xFFFF0000), F32)
    return lo, hi


ROW_TILE = 8
PACKED_LANES = D_MODEL // 2 // ROW_TILE


def _store_row_tiles(ref, val):
    rows = val.shape[0]
    for s in range(ROW_TILE):
        ref[pl.ds(s, rows, stride=ROW_TILE), :] = val[:, s * PACKED_LANES:(s + 1) * PACKED_LANES]


def _load_row_tiles(ref, rows):
    return [ref[pl.ds(s, rows, stride=ROW_TILE), :] for s in range(ROW_TILE)]


def _out_ln_route_kernel(xp_ref, xs_ref, h_ref, wo_ref, g_ref, b_ref, wr_ref, rb_ref, upper_ref,
                         x1_ref, x1p_ref, idx_ref, wts_ref, pos_ref, cnt_ref, run_ref, *, n_first):
    i = pl.program_id(0)

    @pl.when(i == 0)
    def _():
        run_ref[...] = jnp.zeros_like(run_ref)

    x = jnp.where(i < n_first, xp_ref[...], xs_ref[...])
    y = DN_ALPHA * x + jnp.dot(h_ref[...], wo_ref[...], preferred_element_type=F32)
    x1 = _layer_norm(y, g_ref[...], b_ref[...])
    x1_ref[...] = x1
    _store_row_tiles(x1p_ref, _pack_halves(x1))
    _route(x1, wr_ref, rb_ref, upper_ref, run_ref, idx_ref, wts_ref, pos_ref)
    cnt_ref[...] = run_ref[...].astype(jnp.int32)


def _out_ln_route(x_p, x_s, h, w_out, ln_g, ln_b, w_router_t, router_bias, *, tm):
    n = h.shape[0]
    nt = n // tm
    nf, ns = x_p.shape[0] // tm, x_s.shape[0] // tm
    t = np.arange(tm)
    upper = jnp.asarray((t[:, None] < t[None, :]).astype(np.float32), dtype=BF16)

    def rows(i):
        return (i, 0)

    def const(i):
        return (0, 0)

    small = pl.BlockSpec((1, TOP_K, tm), lambda i: (i, 0, 0))
    return pl.pallas_call(
        functools.partial(_out_ln_route_kernel, n_first=nf),
        grid=(nt,),
        in_specs=_two_segment_specs((tm, D_MODEL), nf, ns) + [
            pl.BlockSpec((tm, D_MODEL), rows),
            _resident((D_MODEL, D_MODEL), const),
            pl.BlockSpec((1, D_MODEL), const),
            pl.BlockSpec((1, D_MODEL), const),
            pl.BlockSpec((N_EXPERTS, D_MODEL), const),
            pl.BlockSpec((N_EXPERTS, 1), const),
            pl.BlockSpec((tm, tm), const),
        ],
        out_specs=[pl.BlockSpec((tm, D_MODEL), rows), pl.BlockSpec((tm * ROW_TILE, PACKED_LANES), rows), small, small,
                   small, pl.BlockSpec((N_EXPERTS, 128), const)],
        out_shape=[
            jax.ShapeDtypeStruct((n, D_MODEL), F32),
            jax.ShapeDtypeStruct((n * ROW_TILE, PACKED_LANES), jnp.uint32),
            jax.ShapeDtypeStruct((nt, TOP_K, tm), jnp.int32),
            jax.ShapeDtypeStruct((nt, TOP_K, tm), F32),
            jax.ShapeDtypeStruct((nt, TOP_K, tm), jnp.int32),
            jax.ShapeDtypeStruct((N_EXPERTS, 128), jnp.int32),
        ],
        scratch_shapes=[pltpu.VMEM((N_EXPERTS, 128), F32)],
        compiler_params=_cparams(("arbitrary",)),
        name="out_ln_route",
    )(x_p, x_s, h, w_out, ln_g, ln_b, w_router_t, router_bias, upper)


def _row_copy(src_hbm, row, dst, dst_row, sem):
    return pltpu.make_async_copy(src_hbm.at[pl.ds(row, 1)], dst.at[pl.ds(dst_row, 1)], sem)


def _tile_copy(src, src_row, dst, dst_row, sem):
    src_at = src.at[pl.ds(pl.multiple_of(src_row * ROW_TILE, ROW_TILE), ROW_TILE)]
    dst_at = dst.at[pl.ds(pl.multiple_of(dst_row * ROW_TILE, ROW_TILE), ROW_TILE)]
    return pltpu.make_async_copy(src_at, dst_at, sem)


def _dispatch_kernel(zrow_ref, dest_ref, x_ref, o_hbm, zbuf, sem, zsem, *, block_rows):
    tm = x_ref.shape[0] // ROW_TILE
    n_zero = zrow_ref.shape[0]

    def zero_copy(b):
        start = pl.multiple_of(jnp.maximum(zrow_ref[b], 0) * ROW_TILE, ROW_TILE)
        return pltpu.make_async_copy(zbuf, o_hbm.at[pl.ds(start, block_rows * ROW_TILE)], zsem)

    @pl.when(pl.program_id(0) == 0)
    def _():
        zbuf[...] = jnp.zeros_like(zbuf)

        def start(b, carry):
            @pl.when(zrow_ref[b] >= 0)
            def _():
                zero_copy(b).start()
            return carry

        def wait(b, carry):
            @pl.when(zrow_ref[b] >= 0)
            def _():
                zero_copy(b).wait()
            return carry

        lax.fori_loop(0, n_zero, start, 0)
        lax.fori_loop(0, n_zero, wait, 0)

    def issue(t, carry):
        for k in range(TOP_K):
            _tile_copy(x_ref, t, o_hbm, dest_ref[0, k, t], sem).start(priority=k % 2)
        return carry

    lax.fori_loop(0, tm, issue, 0, unroll=4)
    for k in range(TOP_K):
        pltpu.make_async_copy(x_ref, o_hbm.at[pl.ds(0, tm * ROW_TILE)], sem).wait()


def _dispatch(zrows, dest_blk, x1p, *, n_rows, tm, block_rows):
    n = x1p.shape[0] // ROW_TILE
    grid_spec = pltpu.PrefetchScalarGridSpec(
        num_scalar_prefetch=1,
        grid=(n // tm,),
        in_specs=[
            pl.BlockSpec((1, TOP_K, tm), lambda i, z: (i, 0, 0), memory_space=pltpu.SMEM),
            pl.BlockSpec((tm * ROW_TILE, PACKED_LANES), lambda i, z: (i, 0)),
        ],
        out_specs=pl.BlockSpec(memory_space=pl.ANY),
        scratch_shapes=[pltpu.VMEM((block_rows * ROW_TILE, PACKED_LANES), jnp.uint32),
                        pltpu.SemaphoreType.DMA(()), pltpu.SemaphoreType.DMA(())],
    )
    return pl.pallas_call(
        functools.partial(_dispatch_kernel, block_rows=block_rows),
        grid_spec=grid_spec,
        out_shape=jax.ShapeDtypeStruct((n_rows * ROW_TILE, PACKED_LANES), jnp.uint32),
        compiler_params=_cparams(("arbitrary",)),
        name="dispatch",
    )(zrows, dest_blk, x1p)


def _moe_kernel(be_ref, nv_ref, x_ref, wg_ref, wu_ref, wd_ref, o_ref):
    m = x_ref.shape[0] // ROW_TILE

    @pl.when(pl.program_id(0) < nv_ref[0])
    def _():
        lo, hi = _unpack_halves(jnp.concatenate(_load_row_tiles(x_ref, m), axis=1))
        x = jnp.concatenate([lo.astype(BF16), hi.astype(BF16)], axis=1)
        g = jnp.dot(x, wg_ref[0], preferred_element_type=F32)
        u = jnp.dot(x, wu_ref[0], preferred_element_type=F32)
        hmid = (g * _sigmoid(g) * u).astype(BF16)
        y = jnp.dot(hmid, wd_ref[0], preferred_element_type=F32)
        _store_row_tiles(o_ref, _pack_halves(y))

    @pl.when(pl.program_id(0) >= nv_ref[0])
    def _():
        o_ref[...] = jnp.zeros_like(o_ref)


def _moe(xs, blk_e, n_valid, w_gate, w_up, w_down):
    n_blocks = blk_e.shape[0]
    m = MOE_BLOCK

    def rows(k, be, nv):
        return (jnp.minimum(k, nv[0] - 1), 0)

    def wspec(shape):
        return pl.BlockSpec(shape, lambda k, be, nv: (be[k], 0, 0))

    grid_spec = pltpu.PrefetchScalarGridSpec(
        num_scalar_prefetch=2,
        grid=(n_blocks,),
        in_specs=[
            pl.BlockSpec((m * ROW_TILE, PACKED_LANES), rows),
            wspec((1, D_MODEL, EXPERT_DIM)), wspec((1, D_MODEL, EXPERT_DIM)), wspec((1, EXPERT_DIM, D_MODEL)),
        ],
        out_specs=pl.BlockSpec((m * ROW_TILE, PACKED_LANES), lambda k, be, nv: (k, 0)),
    )
    return pl.pallas_call(
        _moe_kernel,
        grid_spec=grid_spec,
        out_shape=jax.ShapeDtypeStruct(xs.shape, jnp.uint32),
        compiler_params=_cparams(("arbitrary",)),
        name="moe_experts",
    )(blk_e, n_valid, xs, w_gate, w_up, w_down)


def _final_kernel(dest_ref, x1_ref, w_ref, yb_hbm, sg_ref, su_ref, sd_ref, g_ref, b_ref, o_ref, ybuf, sem):
    tm = x1_ref.shape[0]

    def issue(t, carry):
        for k in range(TOP_K):
            _tile_copy(yb_hbm, dest_ref[0, k, t], ybuf.at[k], t, sem).start(priority=k % 2)
        return carry

    lax.fori_loop(0, tm, issue, 0, unroll=4)
    x1 = x1_ref[...]
    xb = x1.astype(BF16)
    g = jnp.dot(xb, sg_ref[...], preferred_element_type=F32)
    u = jnp.dot(xb, su_ref[...], preferred_element_type=F32)
    y = jnp.dot((g * _sigmoid(g) * u).astype(BF16), sd_ref[...], preferred_element_type=F32)
    w = w_ref[...]
    for k in range(TOP_K):
        pltpu.make_async_copy(yb_hbm.at[pl.ds(0, tm * ROW_TILE)], ybuf.at[k], sem).wait()
    r_lo = [None] * ROW_TILE
    r_hi = [None] * ROW_TILE
    for k in range(TOP_K):
        wk = jnp.broadcast_to(w[:, k:k + 1], (tm, PACKED_LANES))
        for s, piece in enumerate(_load_row_tiles(ybuf.at[k], tm)):
            lo, hi = _unpack_halves(piece)
            r_lo[s] = wk * lo if r_lo[s] is None else r_lo[s] + wk * lo
            r_hi[s] = wk * hi if r_hi[s] is None else r_hi[s] + wk * hi
    routed = jnp.concatenate(r_lo + r_hi, axis=1)
    o_ref[...] = _layer_norm(DN_ALPHA * x1 + (routed + y), g_ref[...], b_ref[...])


def _final(dest, x1, wts_t, yb, w_sg, w_su, w_sd, ln_g, ln_b, *, tm, row0, rows, name):
    blk0 = row0 // tm

    def tile(i):
        return (blk0 + i, 0)

    def const(i):
        return (0, 0)

    return pl.pallas_call(
        _final_kernel,
        grid=(rows // tm,),
        in_specs=[
            pl.BlockSpec((1, TOP_K, tm), lambda i: (blk0 + i, 0, 0), memory_space=pltpu.SMEM),
            pl.BlockSpec((tm, D_MODEL), tile),
            pl.BlockSpec((tm, TOP_K), tile),
            pl.BlockSpec(memory_space=pl.ANY),
            _resident((D_MODEL, EXPERT_DIM), const),
            _resident((D_MODEL, EXPERT_DIM), const),
            _resident((EXPERT_DIM, D_MODEL), const),
            pl.BlockSpec((1, D_MODEL), const),
            pl.BlockSpec((1, D_MODEL), const),
        ],
        out_specs=pl.BlockSpec((tm, D_MODEL), lambda i: (i, 0)),
        out_shape=jax.ShapeDtypeStruct((rows, D_MODEL), F32),
        scratch_shapes=[pltpu.VMEM((TOP_K, tm * ROW_TILE, PACKED_LANES), jnp.uint32), pltpu.SemaphoreType.DMA(())],
        compiler_params=_cparams(("arbitrary",)),
        name=name,
    )(dest, x1, wts_t, yb, w_sg, w_su, w_sd, ln_g, ln_b)


def kernel(x_prompt, x_sample, cache_win_k, cache_win_v, state_hgrn, cache_mem_k, cache_mem_v, mem_prompt,
           w_in, w_mem_kv, a_sinks, b_lb_logits, b_norm_g, w_branch, w_out, ln1_g, ln1_b,
           w_router, router_bias, w_exp_gate, w_exp_up, w_exp_down, w_sh_gate, w_sh_up, w_sh_down, ln2_g, ln2_b):
    assert w_in.shape[0] == DEPTH == 1
    batch, seq, _ = x_prompt.shape
    dbatch, dseq, _ = x_sample.shape
    n_p = batch * seq
    n_s = dbatch * dseq
    n = n_p + n_s
    l = 0

    xp2 = x_prompt.reshape(n_p, D_MODEL)
    xs2 = x_sample.reshape(n_s, D_MODEL)
    win = w_in[l]
    o_k, o_v, o_b = A_WIDTH, A_WIDTH + A_KV_WIDTH, A_WIDTH + 2 * A_KV_WIDTH
    o_c = o_b + 4 * B_WIDTH
    o_g = o_c + C_WIDTH
    groups = [
        ("aq", 0, o_k, 512, BF16, A_HEAD_DIM ** -0.5),
        ("ak", o_k, o_v, A_KV_WIDTH, F32, 1.0),
        ("av", o_v, o_b, A_KV_WIDTH, F32, 1.0),
        ("hgrn", o_b, o_c, 512, F32, 1.0),
        ("cq", o_c, o_g, 512, BF16, C_HEAD_DIM ** -0.5),
        ("gate", o_g, win.shape[1], 512, F32, 1.0),
    ]
    zp, zs = {}, {}
    for gname, c0, c1, tn, dt, scale in groups:
        wslice = win[:, c0:c1].astype(BF16)
        zp[gname] = _matmul(xp2, wslice, tm=1024, tn=tn, out_dtype=dt, scale=scale, name=f"proj_{gname}_p")
        zs[gname] = _matmul(xs2, wslice, tm=n_s, tn=tn, out_dtype=dt, scale=scale, name=f"proj_{gname}_s")

    a_p = _swa_prompt(zp["aq"], zp["ak"], zp["av"], a_sinks[l], batch=batch, seq=seq)
    lc = cache_win_k.shape[2]
    kc = cache_win_k[l].reshape(dbatch, lc, A_KV_WIDTH)
    vc = cache_win_v[l].reshape(dbatch, lc, A_KV_WIDTH)
    a_s = _swa_sample(zs["aq"], zs["ak"], zs["av"], kc, vc, a_sinks[l], row0=0, batch=dbatch, seq=dseq)

    lower = jnp.cumsum(jax.nn.softmax(b_lb_logits.astype(F32), axis=0), axis=0)[l]
    lbp = jnp.stack([jnp.log(lower), jnp.log1p(-lower), 1.0 - lower])
    ng = jnp.tile(b_norm_g[l].astype(F32), B_HEADS).reshape(1, B_WIDTH)
    s_zero = jnp.zeros((batch, B_HEADS, B_KEY_DIM, B_VAL_DIM), F32)
    b_p, hs_p = _hgrn(zp["hgrn"], lbp, ng, s_zero, row0=0, batch=batch, seq=seq, blk=CHUNK)
    b_s, hs_s = _hgrn(zs["hgrn"], lbp, ng, state_hgrn[l].astype(F32), row0=0, batch=dbatch, seq=dseq, blk=dseq)

    mem = mem_prompt.reshape(batch * MEM_LEN, D_MODEL)
    wmem = w_mem_kv[l]
    mk = _matmul(mem, wmem[:, :C_WIDTH].astype(BF16), tm=batch * MEM_LEN, tn=512, out_dtype=F32, name="proj_mem_k")
    mv = _matmul(mem, wmem[:, C_WIDTH:].astype(BF16), tm=batch * MEM_LEN, tn=512, out_dtype=F32, name="proj_mem_v")
    mk = mk.reshape(batch, MEM_LEN, C_WIDTH)
    mv = mv.reshape(batch, MEM_LEN, C_WIDTH)
    c_p = _cattn(zp["cq"], mk, mv, row0=0, batch=batch, seq=seq, tq=512)
    c_s = _cattn(zs["cq"], cache_mem_k[l].reshape(dbatch, MEM_LEN, C_WIDTH),
                 cache_mem_v[l].reshape(dbatch, MEM_LEN, C_WIDTH), row0=0, batch=dbatch, seq=dseq, tq=dseq)

    tm_r = 256
    h = _merge((a_p, b_p, c_p), (a_s, b_s, c_s), zp["gate"], zs["gate"], w_branch[l].astype(BF16), tm=tm_r)
    x1, x1p, idx, wts, pos, cnt = _out_ln_route(
        xp2, xs2, h, w_out[l].astype(BF16), ln1_g[l].reshape(1, D_MODEL), ln1_b[l].reshape(1, D_MODEL),
        w_router[l].T.astype(F32), router_bias[l].reshape(N_EXPERTS, 1).astype(F32), tm=tm_r)

    m = MOE_BLOCK
    n_pairs = n * TOP_K
    n_blocks = (n_pairs + m - 1) // m + N_EXPERTS
    counts = cnt[:, 0]
    padded = (counts + m - 1) // m * m
    pad_end = jnp.cumsum(padded)
    pad_start = pad_end - padded
    n_valid = pad_end[-1:] // m
    wts_t = jnp.transpose(wts, (0, 2, 1)).reshape(n, TOP_K)
    dest = (pad_start[idx] + pos).astype(jnp.int32)
    blk_first = jnp.arange(n_blocks, dtype=jnp.int32) * m
    blk_e = jnp.sum((blk_first[:, None] >= pad_end[None, :]).astype(jnp.int32), axis=1)
    blk_e = jnp.minimum(blk_e, N_EXPERTS - 1)
    z_pad = jnp.where(padded > counts, pad_end - m, -1)
    z_tail = jnp.where(blk_first >= pad_end[-1], blk_first, -1)
    zrows = jnp.concatenate([z_pad, z_tail]).astype(jnp.int32)

    xs = _dispatch(zrows, dest, x1p, n_rows=n_blocks * m, tm=tm_r, block_rows=m)
    yb = _moe(xs, blk_e, n_valid.astype(jnp.int32), w_exp_gate[l].astype(BF16), w_exp_up[l].astype(BF16),
              w_exp_down[l].astype(BF16))
    tm_f = 128
    dest_blk = jnp.transpose(dest.reshape(n // tm_r, TOP_K, tm_r // tm_f, tm_f),
                             (0, 2, 1, 3)).reshape(n // tm_f, TOP_K, tm_f)
    fin = functools.partial(_final, dest_blk, x1, wts_t, yb, w_sh_gate[l].astype(BF16), w_sh_up[l].astype(BF16),
                            w_sh_down[l].astype(BF16), ln2_g[l].reshape(1, D_MODEL), ln2_b[l].reshape(1, D_MODEL),
                            tm=tm_f)
    y_p = fin(row0=0, rows=n_p, name="combine_shared_ln2_p").reshape(batch, seq, D_MODEL)
    y_s = fin(row0=n_p, rows=n_s, name="combine_shared_ln2_s").reshape(dbatch, dseq, D_MODEL)

    k_p = zp["ak"].reshape(batch, seq, A_KV_HEADS, A_HEAD_DIM)[:, -lc:]
    v_p = zp["av"].reshape(batch, seq, A_KV_HEADS, A_HEAD_DIM)[:, -lc:]
    k_s = zs["ak"].reshape(dbatch, dseq, A_KV_HEADS, A_HEAD_DIM)
    v_s = zs["av"].reshape(dbatch, dseq, A_KV_HEADS, A_HEAD_DIM)
    wk_s = jnp.concatenate([cache_win_k[l].astype(F32), k_s], axis=1)[:, -lc:]
    wv_s = jnp.concatenate([cache_win_v[l].astype(F32), v_s], axis=1)[:, -lc:]
    mk_o = mk.reshape(batch, MEM_LEN, C_HEADS, C_HEAD_DIM)
    mv_o = mv.reshape(batch, MEM_LEN, C_HEADS, C_HEAD_DIM)
    return (y_p, y_s, k_p[None], v_p[None], hs_p[None], mk_o[None], mv_o[None], wk_s[None], wv_s[None], hs_s[None])
```

```python
import functools

import jax
import jax.numpy as jnp
import numpy as np
from jax import lax
from jax.experimental import pallas as pl
from jax.experimental.pallas import tpu as pltpu

F32 = jnp.float32
BF16 = jnp.bfloat16

D_MODEL = 2048
DEPTH = 1
PAST_LEN = 2048
CHUNK = 64
A_HEADS = 16
A_KV_HEADS = 4
A_GROUP = A_HEADS // A_KV_HEADS
A_HEAD_DIM = 64
A_WIDTH = A_HEADS * A_HEAD_DIM
A_KV_WIDTH = A_KV_HEADS * A_HEAD_DIM
WINDOW = 128
WIN_CHUNKS = WINDOW // CHUNK
B_HEADS = 8
B_KEY_DIM = 128
B_VAL_DIM = 128
B_WIDTH = B_HEADS * B_VAL_DIM
SUB = 16
MEM_LEN = 256
C_HEADS = 4
C_HEAD_DIM = 256
C_WIDTH = C_HEADS * C_HEAD_DIM
N_EXPERTS = 64
N_GROUPS = 8
GROUP_SIZE = N_EXPERTS // N_GROUPS
TOPK_GROUPS = 4
TOP_K = 8
EXPERT_DIM = 512
ROUTED_SCALE = 2.5
DN_ALPHA = (2 * DEPTH) ** 0.25
LN_EPS = 1e-5
RMS_EPS = 1e-6

MOE_BLOCK = 256
VMEM_LIMIT = 56 * 1024 * 1024


def _cparams(sem):
    return pltpu.CompilerParams(dimension_semantics=sem, vmem_limit_bytes=VMEM_LIMIT)


def _resident(shape, index_map):
    return pl.BlockSpec(shape, index_map, pipeline_mode=pl.Buffered(1))


def _sigmoid(x):
    return 1.0 / (1.0 + jnp.exp(-x))


def _layer_norm(x, g, b):
    mu = jnp.mean(x, axis=-1, keepdims=True)
    xc = x - mu
    var = jnp.mean(xc * xc, axis=-1, keepdims=True)
    return xc * lax.rsqrt(var + LN_EPS) * g + b


def _mm_kernel(x_ref, w_ref, o_ref, *, scale):
    acc = jnp.dot(x_ref[...].astype(BF16), w_ref[...], preferred_element_type=F32)
    if scale != 1.0:
        acc = acc * scale
    o_ref[...] = acc.astype(o_ref.dtype)


def _matmul(x, w, *, tm, tn, out_dtype, scale=1.0, name):
    m, k = x.shape
    n = w.shape[1]
    return pl.pallas_call(
        functools.partial(_mm_kernel, scale=scale),
        grid=(m // tm, n // tn),
        in_specs=[pl.BlockSpec((tm, k), lambda i, j: (i, 0)), pl.BlockSpec((k, tn), lambda i, j: (0, j))],
        out_specs=pl.BlockSpec((tm, tn), lambda i, j: (i, j)),
        out_shape=jax.ShapeDtypeStruct((m, n), out_dtype),
        compiler_params=_cparams(("parallel", "parallel")),
        name=name,
    )(x, w)


SWA_KEYS = 256
HEAD_PAIR = 2 * A_HEAD_DIM


def _kv_head_planes(x):
    lane = lax.broadcasted_iota(jnp.int32, (1, HEAD_PAIR), 1)
    low = lane < A_HEAD_DIM
    planes = []
    for pair in range(A_KV_HEADS // 2):
        own = x[:, pair * HEAD_PAIR:(pair + 1) * HEAD_PAIR]
        swapped = pltpu.roll(own, A_HEAD_DIM, 1)
        planes.append((jnp.where(low, own, 0.0).astype(BF16), jnp.where(low, 0.0, swapped).astype(BF16)))
        planes.append((jnp.where(low, swapped, 0.0).astype(BF16), jnp.where(low, 0.0, own).astype(BF16)))
    return planes


SWA_LOOKAHEAD = 8


def _swa_chunks(q_ref, rows, chunk_rows, k_planes, v_planes, n_band, bias_ref, valid_of, o_ref):
    zeros_k = jnp.zeros((SWA_KEYS - n_band, HEAD_PAIR), BF16)
    ones_v = jnp.ones((SWA_KEYS, HEAD_PAIR), BF16)
    tiles = [(c, h, x) for c in range(len(chunk_rows)) for h in range(A_KV_HEADS) for x in range(2)]

    def band(plane, c):
        return jnp.concatenate([plane[chunk_rows[c]:chunk_rows[c] + n_band], zeros_k], axis=0)

    def scores(t):
        c, h, x = tiles[t]
        q2 = jnp.concatenate([q_ref[c * rows:(c + 1) * rows, (2 * h + j) * HEAD_PAIR:(2 * h + j + 1) * HEAD_PAIR]
                              for j in range(2)], axis=0).astype(BF16)
        s = lax.dot_general(q2, band(k_planes[h][x], c), (((1,), (1,)), ((), ())), preferred_element_type=F32)
        s = s - bias_ref[h, x]
        valid = valid_of(c)
        return s if valid is None else jnp.where(valid, s, -jnp.inf)

    pending = {t: scores(t) for t in range(min(SWA_LOOKAHEAD, len(tiles)))}
    even = None
    for t, (c, h, x) in enumerate(tiles):
        s = pending.pop(t)
        p = jnp.exp(s - jnp.max(s, axis=-1, keepdims=True)).astype(BF16)
        if t + SWA_LOOKAHEAD < len(tiles):
            pending[t + SWA_LOOKAHEAD] = scores(t + SWA_LOOKAHEAD)
        vw = jnp.concatenate([band(v_planes[h][x], c), ones_v], axis=1)
        o = jnp.dot(p, vw, preferred_element_type=F32)
        o = o[:, :HEAD_PAIR] / o[:, HEAD_PAIR:]
        if x == 0:
            even = o
        else:
            out = even + o
            for j in range(2):
                o_ref[c * rows:(c + 1) * rows, (2 * h + j) * HEAD_PAIR:(2 * h + j + 1) * HEAD_PAIR] = (
                    out[j * rows:(j + 1) * rows].astype(o_ref.dtype))


def _swa_prompt_kernel(q_ref, kc_ref, kp_ref, vc_ref, vp_ref, bias_ref, o_ref, *, n_chunks):
    i = pl.program_id(1)
    pad = WIN_CHUNKS * CHUNK
    n_band = pad + CHUNK
    k_planes = _kv_head_planes(jnp.concatenate([kp_ref[...], kc_ref[...]], axis=0))
    v_planes = _kv_head_planes(jnp.concatenate([vp_ref[...], vc_ref[...]], axis=0))
    s_idx = lax.broadcasted_iota(jnp.int32, (1, SWA_KEYS), 1)

    def valid_of(c):
        return jnp.logical_or(s_idx + (i * (n_chunks * CHUNK) + c * CHUNK - pad) >= 0, s_idx >= n_band)

    _swa_chunks(q_ref, CHUNK, [c * CHUNK for c in range(n_chunks)], k_planes, v_planes, n_band, bias_ref, valid_of,
                o_ref)


def _swa_sample_kernel(q_ref, kn_ref, kc_ref, vn_ref, vc_ref, bias_ref, o_ref):
    k_planes = _kv_head_planes(jnp.concatenate([kc_ref[0], kn_ref[...]], axis=0))
    v_planes = _kv_head_planes(jnp.concatenate([vc_ref[0], vn_ref[...]], axis=0))
    n_band = kc_ref.shape[1] + kn_ref.shape[0]
    _swa_chunks(q_ref, q_ref.shape[0], [0], k_planes, v_planes, n_band, bias_ref, lambda c: None, o_ref)


def _alibi_slopes():
    return (2.0 ** (-8.0 * np.arange(1, A_HEADS + 1) / A_HEADS)).astype(np.float32)


def _swa_bias(q_pos, k_pos, valid, sinks):
    n_q, n_k = len(q_pos), len(k_pos)
    assert n_k < SWA_KEYS
    dist = np.abs(q_pos[:, None] - k_pos[None, :]).astype(np.float32)
    band = _alibi_slopes()[:, None, None] * dist[None]
    if valid is not None:
        band = np.where(valid[None], band, np.inf)
    pad = np.full((A_HEADS, n_q, SWA_KEYS - n_k - 1), np.inf, np.float32)
    sink = jnp.broadcast_to(-sinks.astype(F32)[:, None, None], (A_HEADS, n_q, 1))
    bias = jnp.concatenate([jnp.asarray(band.astype(np.float32)), sink, jnp.asarray(pad)], axis=2)
    bias = bias.reshape(A_KV_HEADS, 2, 2, n_q, SWA_KEYS).transpose(0, 2, 1, 3, 4)
    return bias.reshape(A_KV_HEADS, 2, 2 * n_q, SWA_KEYS)


def _swa_prompt(q, k, v, sinks, *, batch, seq, n_chunks=4):
    tq = n_chunks * CHUNK
    pad = WIN_CHUNKS * CHUNK
    nb = seq // tq
    bias = _swa_bias(pad + np.arange(CHUNK), np.arange(pad + CHUNK), None, sinks)
    prev_per_blk = tq // pad

    def cur(b, i):
        return (b * nb + i, 0)

    def prev(b, i):
        return (jnp.maximum((b * nb + i) * prev_per_blk - 1, b * nb * prev_per_blk), 0)

    return pl.pallas_call(
        functools.partial(_swa_prompt_kernel, n_chunks=n_chunks),
        grid=(batch, nb),
        in_specs=[
            pl.BlockSpec((tq, A_WIDTH), cur),
            pl.BlockSpec((tq, A_KV_WIDTH), cur),
            pl.BlockSpec((pad, A_KV_WIDTH), prev),
            pl.BlockSpec((tq, A_KV_WIDTH), cur),
            pl.BlockSpec((pad, A_KV_WIDTH), prev),
            pl.BlockSpec(bias.shape, lambda b, i: (0, 0, 0, 0)),
        ],
        out_specs=pl.BlockSpec((tq, A_WIDTH), cur),
        out_shape=jax.ShapeDtypeStruct((batch * seq, A_WIDTH), BF16),
        compiler_params=_cparams(("parallel", "arbitrary")),
        name="swa_prompt",
    )(q, k, k, v, v, bias)


def _swa_sample(q, k, v, k_cache, v_cache, sinks, *, row0, batch, seq):
    lc = k_cache.shape[1]
    q_pos = PAST_LEN + np.arange(seq)
    k_pos = PAST_LEN - lc + np.arange(lc + seq)
    cdiff = q_pos[:, None] // CHUNK - k_pos[None, :] // CHUNK
    valid = (cdiff >= 0) & (cdiff <= WIN_CHUNKS)
    bias = _swa_bias(q_pos, k_pos, valid, sinks)
    blk0 = row0 // seq

    def rows(b):
        return (blk0 + b, 0)

    return pl.pallas_call(
        _swa_sample_kernel,
        grid=(batch,),
        in_specs=[
            pl.BlockSpec((seq, A_WIDTH), rows),
            pl.BlockSpec((seq, A_KV_WIDTH), rows),
            pl.BlockSpec((1, lc, A_KV_WIDTH), lambda b: (b, 0, 0)),
            pl.BlockSpec((seq, A_KV_WIDTH), rows),
            pl.BlockSpec((1, lc, A_KV_WIDTH), lambda b: (b, 0, 0)),
            pl.BlockSpec(bias.shape, lambda b: (0, 0, 0, 0)),
        ],
        out_specs=pl.BlockSpec((seq, A_WIDTH), lambda b: (b, 0)),
        out_shape=jax.ShapeDtypeStruct((batch * seq, A_WIDTH), BF16),
        compiler_params=_cparams(("parallel",)),
        name="swa_sample",
    )(q, k, k_cache, v, v_cache, bias)


def _split3(x):
    hi = x.astype(BF16)
    r1 = x - hi.astype(F32)
    mid = r1.astype(BF16)
    lo = (r1 - mid.astype(F32)).astype(BF16)
    return hi, mid, lo


def _dot3(mat, parts):
    acc = jnp.dot(mat, parts[0], preferred_element_type=F32)
    acc = acc + jnp.dot(mat, parts[1], preferred_element_type=F32)
    return acc + jnp.dot(mat, parts[2], preferred_element_type=F32)


def _hgrn_kernel(zq_ref, zf_ref, zi_ref, zg_ref, lb_ref, ng_ref, tri_ref, tsel_ref, s0_ref, o_ref, sfin_ref, s_scr,
                 *, blk):
    j = pl.program_id(1)

    @pl.when(j == 0)
    def _():
        s_scr[...] = s0_ref[0]

    bq = zq_ref[...]
    fl = zf_ref[...]
    v = zi_ref[...]
    bg = zg_ref[...]
    log_lb = lb_ref[0:1, :]
    log1m_lb = lb_ref[1:2, :]
    one_m_lb = lb_ref[2:3, :]

    q = bq * _sigmoid(bq) * (B_KEY_DIM ** -0.5)
    log_sig = jnp.minimum(fl, 0.0) - jnp.log1p(jnp.exp(-jnp.abs(fl)))
    c = log1m_lb + log_sig
    logf = jnp.maximum(log_lb, c) + jnp.log1p(jnp.exp(-jnp.abs(log_lb - c)))
    k = one_m_lb * _sigmoid(-fl)

    parts = _split3(logf)
    b = _dot3(tri_ref[...], parts)
    rq = _dot3(tsel_ref[...], parts)
    qt = q * jnp.exp(b - rq)
    qb = (q * jnp.exp(b)).astype(BF16)
    b_last = b[blk - 1:blk, :]
    khat = (k * jnp.exp(b_last - b)).astype(BF16)
    e_last = jnp.exp(b_last)
    vb = v.astype(BF16)
    row = lax.broadcasted_iota(jnp.int32, (blk, 1), 0)
    n_sub = blk // SUB
    kts = []
    for i in range(n_sub):
        r_i = rq[i * SUB:i * SUB + 1, :]
        kts.append(jnp.where(row < (i + 1) * SUB, k * jnp.exp(r_i - b), 0.0).astype(BF16))
    qt = qt.astype(BF16)
    tril = lax.broadcasted_iota(jnp.int32, (blk, blk), 0) >= lax.broadcasted_iota(jnp.int32, (blk, blk), 1)

    heads = [slice(h * B_KEY_DIM, (h + 1) * B_KEY_DIM) for h in range(B_HEADS)]
    s_old = [s_scr[h] for h in range(B_HEADS)]
    a_raw, o_state, ds = [], [], []
    for h, hs in enumerate(heads):
        a_raw.append(jnp.concatenate(
            [lax.dot_general(qt[i * SUB:(i + 1) * SUB, hs], kts[i][:, hs], (((1,), (1,)), ((), ())),
                             preferred_element_type=F32) for i in range(n_sub)], axis=0))
        o_state.append(jnp.dot(qb[:, hs], s_old[h].astype(BF16), preferred_element_type=F32))
        ds.append(lax.dot_general(khat[:, hs], vb[:, hs], (((0,), (0,)), ((), ())), preferred_element_type=F32))
    outs = []
    for h, hs in enumerate(heads):
        a = jnp.where(tril, a_raw[h], 0.0).astype(BF16)
        o = jnp.dot(a, vb[:, hs], preferred_element_type=F32) + o_state[h]
        decay = jnp.transpose(jnp.broadcast_to(e_last[:, hs], (B_KEY_DIM, B_KEY_DIM)))
        s_scr[h] = decay * s_old[h] + ds[h]
        o = o * lax.rsqrt(jnp.mean(o * o, axis=-1, keepdims=True) + RMS_EPS)
        outs.append(o)
    o_all = jnp.concatenate(outs, axis=1) * ng_ref[...] * (bg * _sigmoid(bg))
    o_ref[...] = o_all.astype(o_ref.dtype)

    @pl.when(j == pl.num_programs(1) - 1)
    def _():
        sfin_ref[0] = s_scr[...]


def _hgrn(zb, lbp, ng, s0, *, row0, batch, seq, blk):
    nb = seq // blk
    blk0 = row0 // blk
    t = np.arange(blk)
    tri = jnp.asarray((t[:, None] >= t[None, :]).astype(np.float32), dtype=BF16)
    tsel = jnp.asarray((t[None, :] < (t[:, None] // SUB) * SUB).astype(np.float32), dtype=BF16)

    def zspec(col):
        return pl.BlockSpec((blk, B_WIDTH), lambda b, j: (blk0 + b * nb + j, col))

    state_spec = pl.BlockSpec((1, B_HEADS, B_KEY_DIM, B_VAL_DIM), lambda b, j: (b, 0, 0, 0))
    return pl.pallas_call(
        functools.partial(_hgrn_kernel, blk=blk),
        grid=(batch, nb),
        in_specs=[
            zspec(0), zspec(1), zspec(2), zspec(3),
            pl.BlockSpec((3, B_WIDTH), lambda b, j: (0, 0)),
            pl.BlockSpec((1, B_WIDTH), lambda b, j: (0, 0)),
            pl.BlockSpec((blk, blk), lambda b, j: (0, 0)),
            pl.BlockSpec((blk, blk), lambda b, j: (0, 0)),
            state_spec,
        ],
        out_specs=[pl.BlockSpec((blk, B_WIDTH), lambda b, j: (b * nb + j, 0)), state_spec],
        out_shape=[jax.ShapeDtypeStruct((batch * seq, B_WIDTH), BF16),
                   jax.ShapeDtypeStruct((batch, B_HEADS, B_KEY_DIM, B_VAL_DIM), F32)],
        scratch_shapes=[pltpu.VMEM((B_HEADS, B_KEY_DIM, B_VAL_DIM), F32)],
        compiler_params=_cparams(("parallel", "arbitrary")),
        name=f"hgrn_blk{blk}",
    )(zb, zb, zb, zb, lbp, ng, tri, tsel, s0)


def _cattn_kernel(q_ref, mk_ref, mv_ref, o_ref):
    mk = mk_ref[0].astype(BF16)
    mv = mv_ref[0].astype(BF16)
    q = q_ref[...]
    for h in range(C_HEADS):
        hs = slice(h * C_HEAD_DIM, (h + 1) * C_HEAD_DIM)
        s = lax.dot_general(q[:, hs], mk[:, hs], (((1,), (1,)), ((), ())), preferred_element_type=F32)
        p = jnp.exp(s - jnp.max(s, axis=-1, keepdims=True))
        p = p / jnp.sum(p, axis=-1, keepdims=True)
        o_ref[:, hs] = jnp.dot(p.astype(BF16), mv[:, hs], preferred_element_type=F32).astype(o_ref.dtype)


def _cattn(q, mk, mv, *, row0, batch, seq, tq):
    nb = seq // tq
    blk0 = row0 // tq
    mem_spec = pl.BlockSpec((1, MEM_LEN, C_WIDTH), lambda b, i: (b, 0, 0))
    return pl.pallas_call(
        _cattn_kernel,
        grid=(batch, nb),
        in_specs=[pl.BlockSpec((tq, C_WIDTH), lambda b, i: (blk0 + b * nb + i, 0)), mem_spec, mem_spec],
        out_specs=pl.BlockSpec((tq, C_WIDTH), lambda b, i: (b * nb + i, 0)),
        out_shape=jax.ShapeDtypeStruct((batch * seq, C_WIDTH), BF16),
        compiler_params=_cparams(("parallel", "parallel")),
        name=f"cattn_tq{tq}",
    )(q, mk, mv)


def _two_segment_specs(block, n_first, n_second, col=0):
    first = pl.BlockSpec(block, lambda i: (jnp.minimum(i, n_first - 1), col))
    second = pl.BlockSpec(block, lambda i: (jnp.clip(i - n_first, 0, n_second - 1), col))
    return [first, second]


def _merge_kernel(ap_ref, as_ref, bp_ref, bs_ref, cp_ref, cs_ref, g0p_ref, g0s_ref, g1p_ref, g1s_ref, g2p_ref, g2s_ref,
                  pa_ref, pb_ref, pc_ref, o_ref, *, n_first):
    first = pl.program_id(0) < n_first

    def pick(p_ref, s_ref):
        return jnp.where(first, p_ref[...], s_ref[...])

    h = _sigmoid(pick(g0p_ref, g0s_ref)) * jnp.dot(pick(ap_ref, as_ref), pa_ref[...], preferred_element_type=F32)
    h = h + _sigmoid(pick(g1p_ref, g1s_ref)) * jnp.dot(pick(bp_ref, bs_ref), pb_ref[...], preferred_element_type=F32)
    h = h + _sigmoid(pick(g2p_ref, g2s_ref)) * jnp.dot(pick(cp_ref, cs_ref), pc_ref[...], preferred_element_type=F32)
    o_ref[...] = h.astype(o_ref.dtype)


def _merge(abc_p, abc_s, gl_p, gl_s, p, *, tm):
    n_p, width = abc_p[0].shape
    n_s = abc_s[0].shape[0]
    nf, ns = n_p // tm, n_s // tm
    in_specs = []
    args = []
    for x_p, x_s in zip(abc_p, abc_s):
        in_specs += _two_segment_specs((tm, width), nf, ns)
        args += [x_p, x_s]
    for col in range(3):
        in_specs += _two_segment_specs((tm, D_MODEL), nf, ns, col)
        args += [gl_p, gl_s]
    for blk in range(3):
        in_specs.append(_resident((width, D_MODEL), functools.partial(lambda i, b: (b, 0), b=blk)))
        args.append(p)
    return pl.pallas_call(
        functools.partial(_merge_kernel, n_first=nf),
        grid=(nf + ns,),
        in_specs=in_specs,
        out_specs=pl.BlockSpec((tm, D_MODEL), lambda i: (i, 0)),
        out_shape=jax.ShapeDtypeStruct((n_p + n_s, D_MODEL), BF16),
        compiler_params=_cparams(("parallel",)),
        name="merge",
    )(*args)


def _first_index(hit_src, m, iota, size, axis):
    return jnp.min(jnp.where(hit_src == m, iota, size), axis=axis, keepdims=True)


def _route(x1, wr_ref, rb_ref, upper_ref, run_ref, idx_ref, wts_ref, pos_ref):
    tm = x1.shape[0]
    logits = lax.dot_general(wr_ref[...], x1, (((1,), (1,)), ((), ())), precision=lax.Precision.HIGHEST,
                             preferred_element_type=F32)
    scores = _sigmoid(logits)
    choice = scores + rb_ref[...]
    g3 = choice.reshape(N_GROUPS, GROUP_SIZE, tm)
    mem_iota = lax.broadcasted_iota(jnp.int32, g3.shape, 1)
    m1 = jnp.max(g3, axis=1, keepdims=True)
    first = _first_index(g3, m1, mem_iota, GROUP_SIZE, 1)
    m2 = jnp.max(jnp.where(mem_iota == first, -jnp.inf, g3), axis=1, keepdims=True)
    gscore = (m1 + m2).reshape(N_GROUPS, tm)
    g_iota = lax.broadcasted_iota(jnp.int32, gscore.shape, 0)
    gsel = jnp.zeros(gscore.shape, F32)
    cur = gscore
    for _ in range(TOPK_GROUPS):
        m = jnp.max(cur, axis=0, keepdims=True)
        hit = g_iota == _first_index(cur, m, g_iota, N_GROUPS, 0)
        gsel = jnp.where(hit, 1.0, gsel)
        cur = jnp.where(hit, -jnp.inf, cur)
    masked = jnp.where(gsel.reshape(N_GROUPS, 1, tm) > 0.5, g3, -jnp.inf).reshape(N_EXPERTS, tm)
    e_iota = lax.broadcasted_iota(jnp.int32, masked.shape, 0)
    sel = jnp.zeros(masked.shape, F32)
    cur = masked
    idxs, ws = [], []
    for _ in range(TOP_K):
        m = jnp.max(cur, axis=0, keepdims=True)
        ei = _first_index(cur, m, e_iota, N_EXPERTS, 0)
        hit = e_iota == ei
        idxs.append(ei)
        ws.append(jnp.sum(jnp.where(hit, scores, 0.0), axis=0, keepdims=True))
        sel = jnp.where(hit, 1.0, sel)
        cur = jnp.where(hit, -jnp.inf, cur)
    w = jnp.concatenate(ws, axis=0)
    w = w / jnp.sum(w, axis=0, keepdims=True) * ROUTED_SCALE
    excl = jnp.dot(sel.astype(BF16), upper_ref[...], preferred_element_type=F32)
    posfull = excl + run_ref[:, 0:1]
    run_ref[...] = run_ref[...] + jnp.sum(sel, axis=1, keepdims=True)
    pos = [jnp.sum(jnp.where(e_iota == ei, posfull, 0.0), axis=0, keepdims=True) for ei in idxs]
    idx_ref[0] = jnp.concatenate(idxs, axis=0)
    wts_ref[0] = w
    pos_ref[0] = jnp.concatenate(pos, axis=0).astype(jnp.int32)


def _pack_halves(x):
    c = x.shape[1] // 2
    lo = lax.bitcast_convert_type(x[:, :c].astype(BF16).astype(F32), jnp.uint32)
    hi = lax.bitcast_convert_type(x[:, c:].astype(BF16).astype(F32), jnp.uint32)
    return (hi & jnp.uint32(0xFFFF0000)) | (lo >> 16)


def _unpack_halves(w):
    lo = lax.bitcast_convert_type(w << 16, F32)
    hi = lax.bitcast_convert_type(w & jnp.uint32(0xFFFF0000), F32)
    return lo, hi


ROW_TILE = 8
PACKED_LANES = D_MODEL // 2 // ROW_TILE


def _store_row_tiles(ref, val):
    rows = val.shape[0]
    for s in range(ROW_TILE):
        ref[pl.ds(s, rows, stride=ROW_TILE), :] = val[:, s * PACKED_LANES:(s + 1) * PACKED_LANES]


def _load_row_tiles(ref, rows):
    return [ref[pl.ds(s, rows, stride=ROW_TILE), :] for s in range(ROW_TILE)]


def _out_ln_route_kernel(xp_ref, xs_ref, h_ref, wo_ref, g_ref, b_ref, wr_ref, rb_ref, upper_ref,
                         x1_ref, x1p_ref, idx_ref, wts_ref, pos_ref, cnt_ref, run_ref, *, n_first):
    i = pl.program_id(0)

    @pl.when(i == 0)
    def _():
        run_ref[...] = jnp.zeros_like(run_ref)

    x = jnp.where(i < n_first, xp_ref[...], xs_ref[...])
    y = DN_ALPHA * x + jnp.dot(h_ref[...], wo_ref[...], preferred_element_type=F32)
    x1 = _layer_norm(y, g_ref[...], b_ref[...])
    x1_ref[...] = x1
    _store_row_tiles(x1p_ref, _pack_halves(x1))
    _route(x1, wr_ref, rb_ref, upper_ref, run_ref, idx_ref, wts_ref, pos_ref)
    cnt_ref[...] = run_ref[...].astype(jnp.int32)


def _out_ln_route(x_p, x_s, h, w_out, ln_g, ln_b, w_router_t, router_bias, *, tm):
    n = h.shape[0]
    nt = n // tm
    nf, ns = x_p.shape[0] // tm, x_s.shape[0] // tm
    t = np.arange(tm)
    upper = jnp.asarray((t[:, None] < t[None, :]).astype(np.float32), dtype=BF16)

    def rows(i):
        return (i, 0)

    def const(i):
        return (0, 0)

    small = pl.BlockSpec((1, TOP_K, tm), lambda i: (i, 0, 0))
    return pl.pallas_call(
        functools.partial(_out_ln_route_kernel, n_first=nf),
        grid=(nt,),
        in_specs=_two_segment_specs((tm, D_MODEL), nf, ns) + [
            pl.BlockSpec((tm, D_MODEL), rows),
            _resident((D_MODEL, D_MODEL), const),
            pl.BlockSpec((1, D_MODEL), const),
            pl.BlockSpec((1, D_MODEL), const),
            pl.BlockSpec((N_EXPERTS, D_MODEL), const),
            pl.BlockSpec((N_EXPERTS, 1), const),
            pl.BlockSpec((tm, tm), const),
        ],
        out_specs=[pl.BlockSpec((tm, D_MODEL), rows), pl.BlockSpec((tm * ROW_TILE, PACKED_LANES), rows), small, small,
                   small, pl.BlockSpec((N_EXPERTS, 128), const)],
        out_shape=[
            jax.ShapeDtypeStruct((n, D_MODEL), F32),
            jax.ShapeDtypeStruct((n * ROW_TILE, PACKED_LANES), jnp.uint32),
            jax.ShapeDtypeStruct((nt, TOP_K, tm), jnp.int32),
            jax.ShapeDtypeStruct((nt, TOP_K, tm), F32),
            jax.ShapeDtypeStruct((nt, TOP_K, tm), jnp.int32),
            jax.ShapeDtypeStruct((N_EXPERTS, 128), jnp.int32),
        ],
        scratch_shapes=[pltpu.VMEM((N_EXPERTS, 128), F32)],
        compiler_params=_cparams(("arbitrary",)),
        name="out_ln_route",
    )(x_p, x_s, h, w_out, ln_g, ln_b, w_router_t, router_bias, upper)


def _row_copy(src_hbm, row, dst, dst_row, sem):
    return pltpu.make_async_copy(src_hbm.at[pl.ds(row, 1)], dst.at[pl.ds(dst_row, 1)], sem)


def _tile_copy(src, src_row, dst, dst_row, sem):
    src_at = src.at[pl.ds(pl.multiple_of(src_row * ROW_TILE, ROW_TILE), ROW_TILE)]
    dst_at = dst.at[pl.ds(pl.multiple_of(dst_row * ROW_TILE, ROW_TILE), ROW_TILE)]
    return pltpu.make_async_copy(src_at, dst_at, sem)


def _dispatch_kernel(zrow_ref, dest_ref, x_ref, o_hbm, zbuf, sem, zsem, *, block_rows):
    tm = x_ref.shape[0] // ROW_TILE
    n_zero = zrow_ref.shape[0]

    def zero_copy(b):
        start = pl.multiple_of(jnp.maximum(zrow_ref[b], 0) * ROW_TILE, ROW_TILE)
        return pltpu.make_async_copy(zbuf, o_hbm.at[pl.ds(start, block_rows * ROW_TILE)], zsem)

    @pl.when(pl.program_id(0) == 0)
    def _():
        zbuf[...] = jnp.zeros_like(zbuf)

        def start(b, carry):
            @pl.when(zrow_ref[b] >= 0)
            def _():
                zero_copy(b).start()
            return carry

        def wait(b, carry):
            @pl.when(zrow_ref[b] >= 0)
            def _():
                zero_copy(b).wait()
            return carry

        lax.fori_loop(0, n_zero, start, 0)
        lax.fori_loop(0, n_zero, wait, 0)

    def issue(t, carry):
        for k in range(TOP_K):
            _tile_copy(x_ref, t, o_hbm, dest_ref[0, k, t], sem).start(priority=k % 2)
        return carry

    lax.fori_loop(0, tm, issue, 0, unroll=4)
    for k in range(TOP_K):
        pltpu.make_async_copy(x_ref, o_hbm.at[pl.ds(0, tm * ROW_TILE)], sem).wait()


def _dispatch(zrows, dest_blk, x1p, *, n_rows, tm, block_rows):
    n = x1p.shape[0] // ROW_TILE
    grid_spec = pltpu.PrefetchScalarGridSpec(
        num_scalar_prefetch=1,
        grid=(n // tm,),
        in_specs=[
            pl.BlockSpec((1, TOP_K, tm), lambda i, z: (i, 0, 0), memory_space=pltpu.SMEM),
            pl.BlockSpec((tm * ROW_TILE, PACKED_LANES), lambda i, z: (i, 0)),
        ],
        out_specs=pl.BlockSpec(memory_space=pl.ANY),
        scratch_shapes=[pltpu.VMEM((block_rows * ROW_TILE, PACKED_LANES), jnp.uint32),
                        pltpu.SemaphoreType.DMA(()), pltpu.SemaphoreType.DMA(())],
    )
    return pl.pallas_call(
        functools.partial(_dispatch_kernel, block_rows=block_rows),
        grid_spec=grid_spec,
        out_shape=jax.ShapeDtypeStruct((n_rows * ROW_TILE, PACKED_LANES), jnp.uint32),
        compiler_params=_cparams(("arbitrary",)),
        name="dispatch",
    )(zrows, dest_blk, x1p)


def _moe_kernel(be_ref, nv_ref, x_ref, wg_ref, wu_ref, wd_ref, o_ref):
    m = x_ref.shape[0] // ROW_TILE

    @pl.when(pl.program_id(0) < nv_ref[0])
    def _():
        lo, hi = _unpack_halves(jnp.concatenate(_load_row_tiles(x_ref, m), axis=1))
        x = jnp.concatenate([lo.astype(BF16), hi.astype(BF16)], axis=1)
        g = jnp.dot(x, wg_ref[0], preferred_element_type=F32)
        u = jnp.dot(x, wu_ref[0], preferred_element_type=F32)
        hmid = (g * _sigmoid(g) * u).astype(BF16)
        y = jnp.dot(hmid, wd_ref[0], preferred_element_type=F32)
        _store_row_tiles(o_ref, _pack_halves(y))

    @pl.when(pl.program_id(0) >= nv_ref[0])
    def _():
        o_ref[...] = jnp.zeros_like(o_ref)


def _moe(xs, blk_e, n_valid, w_gate, w_up, w_down):
    n_blocks = blk_e.shape[0]
    m = MOE_BLOCK

    def rows(k, be, nv):
        return (jnp.minimum(k, nv[0] - 1), 0)

    def wspec(shape):
        return pl.BlockSpec(shape, lambda k, be, nv: (be[k], 0, 0))

    grid_spec = pltpu.PrefetchScalarGridSpec(
        num_scalar_prefetch=2,
        grid=(n_blocks,),
        in_specs=[
            pl.BlockSpec((m * ROW_TILE, PACKED_LANES), rows),
            wspec((1, D_MODEL, EXPERT_DIM)), wspec((1, D_MODEL, EXPERT_DIM)), wspec((1, EXPERT_DIM, D_MODEL)),
        ],
        out_specs=pl.BlockSpec((m * ROW_TILE, PACKED_LANES), lambda k, be, nv: (k, 0)),
    )
    return pl.pallas_call(
        _moe_kernel,
        grid_spec=grid_spec,
        out_shape=jax.ShapeDtypeStruct(xs.shape, jnp.uint32),
        compiler_params=_cparams(("arbitrary",)),
        name="moe_experts",
    )(blk_e, n_valid, xs, w_gate, w_up, w_down)


def _final_kernel(dest_ref, x1_ref, w_ref, yb_hbm, sg_ref, su_ref, sd_ref, g_ref, b_ref, o_ref, ybuf, sem):
    tm = x1_ref.shape[0]

    def issue(t, carry):
        for k in range(TOP_K):
            _tile_copy(yb_hbm, dest_ref[0, k, t], ybuf.at[k], t, sem).start(priority=k % 2)
        return carry

    lax.fori_loop(0, tm, issue, 0, unroll=4)
    x1 = x1_ref[...]
    xb = x1.astype(BF16)
    g = jnp.dot(xb, sg_ref[...], preferred_element_type=F32)
    u = jnp.dot(xb, su_ref[...], preferred_element_type=F32)
    y = jnp.dot((g * _sigmoid(g) * u).astype(BF16), sd_ref[...], preferred_element_type=F32)
    w = w_ref[...]
    for k in range(TOP_K):
        pltpu.make_async_copy(yb_hbm.at[pl.ds(0, tm * ROW_TILE)], ybuf.at[k], sem).wait()
    r_lo = [None] * ROW_TILE
    r_hi = [None] * ROW_TILE
    for k in range(TOP_K):
        wk = jnp.broadcast_to(w[:, k:k + 1], (tm, PACKED_LANES))
        for s, piece in enumerate(_load_row_tiles(ybuf.at[k], tm)):
            lo, hi = _unpack_halves(piece)
            r_lo[s] = wk * lo if r_lo[s] is None else r_lo[s] + wk * lo
            r_hi[s] = wk * hi if r_hi[s] is None else r_hi[s] + wk * hi
    routed = jnp.concatenate(r_lo + r_hi, axis=1)
    o_ref[...] = _layer_norm(DN_ALPHA * x1 + (routed + y), g_ref[...], b_ref[...])


def _final(dest, x1, wts_t, yb, w_sg, w_su, w_sd, ln_g, ln_b, *, tm, row0, rows, name):
    blk0 = row0 // tm

    def tile(i):
        return (blk0 + i, 0)

    def const(i):
        return (0, 0)

    return pl.pallas_call(
        _final_kernel,
        grid=(rows // tm,),
        in_specs=[
            pl.BlockSpec((1, TOP_K, tm), lambda i: (blk0 + i, 0, 0), memory_space=pltpu.SMEM),
            pl.BlockSpec((tm, D_MODEL), tile),
            pl.BlockSpec((tm, TOP_K), tile),
            pl.BlockSpec(memory_space=pl.ANY),
            _resident((D_MODEL, EXPERT_DIM), const),
            _resident((D_MODEL, EXPERT_DIM), const),
            _resident((EXPERT_DIM, D_MODEL), const),
            pl.BlockSpec((1, D_MODEL), const),
            pl.BlockSpec((1, D_MODEL), const),
        ],
        out_specs=pl.BlockSpec((tm, D_MODEL), lambda i: (i, 0)),
        out_shape=jax.ShapeDtypeStruct((rows, D_MODEL), F32),
        scratch_shapes=[pltpu.VMEM((TOP_K, tm * ROW_TILE, PACKED_LANES), jnp.uint32), pltpu.SemaphoreType.DMA(())],
        compiler_params=_cparams(("arbitrary",)),
        name=name,
    )(dest, x1, wts_t, yb, w_sg, w_su, w_sd, ln_g, ln_b)


def kernel(x_prompt, x_sample, cache_win_k, cache_win_v, state_hgrn, cache_mem_k, cache_mem_v, mem_prompt,
           w_in, w_mem_kv, a_sinks, b_lb_logits, b_norm_g, w_branch, w_out, ln1_g, ln1_b,
           w_router, router_bias, w_exp_gate, w_exp_up, w_exp_down, w_sh_gate, w_sh_up, w_sh_down, ln2_g, ln2_b):
    assert w_in.shape[0] == DEPTH == 1
    batch, seq, _ = x_prompt.shape
    dbatch, dseq, _ = x_sample.shape
    n_p = batch * seq
    n_s = dbatch * dseq
    n = n_p + n_s
    l = 0

    xp2 = x_prompt.reshape(n_p, D_MODEL)
    xs2 = x_sample.reshape(n_s, D_MODEL)
    win = w_in[l]
    o_k, o_v, o_b = A_WIDTH, A_WIDTH + A_KV_WIDTH, A_WIDTH + 2 * A_KV_WIDTH
    o_c = o_b + 4 * B_WIDTH
    o_g = o_c + C_WIDTH
    groups = [
        ("aq", 0, o_k, 1024, BF16, A_HEAD_DIM ** -0.5),
        ("ak", o_k, o_v, A_KV_WIDTH, F32, 1.0),
        ("av", o_v, o_b, A_KV_WIDTH, F32, 1.0),
        ("hgrn", o_b, o_c, 1024, F32, 1.0),
        ("cq", o_c, o_g, 1024, BF16, C_HEAD_DIM ** -0.5),
        ("gate", o_g, win.shape[1], 1024, F32, 1.0),
    ]
    zp, zs = {}, {}
    for gname, c0, c1, tn, dt, scale in groups:
        wslice = win[:, c0:c1].astype(BF16)
        zp[gname] = _matmul(xp2, wslice, tm=1024, tn=tn, out_dtype=dt, scale=scale, name=f"proj_{gname}_p")
        zs[gname] = _matmul(xs2, wslice, tm=n_s, tn=tn, out_dtype=dt, scale=scale, name=f"proj_{gname}_s")

    a_p = _swa_prompt(zp["aq"], zp["ak"], zp["av"], a_sinks[l], batch=batch, seq=seq)
    lc = cache_win_k.shape[2]
    kc = cache_win_k[l].reshape(dbatch, lc, A_KV_WIDTH)
    vc = cache_win_v[l].reshape(dbatch, lc, A_KV_WIDTH)
    a_s = _swa_sample(zs["aq"], zs["ak"], zs["av"], kc, vc, a_sinks[l], row0=0, batch=dbatch, seq=dseq)

    lower = jnp.cumsum(jax.nn.softmax(b_lb_logits.astype(F32), axis=0), axis=0)[l]
    lbp = jnp.stack([jnp.log(lower), jnp.log1p(-lower), 1.0 - lower])
    ng = jnp.tile(b_norm_g[l].astype(F32), B_HEADS).reshape(1, B_WIDTH)
    s_zero = jnp.zeros((batch, B_HEADS, B_KEY_DIM, B_VAL_DIM), F32)
    b_p, hs_p = _hgrn(zp["hgrn"], lbp, ng, s_zero, row0=0, batch=batch, seq=seq, blk=CHUNK)
    b_s, hs_s = _hgrn(zs["hgrn"], lbp, ng, state_hgrn[l].astype(F32), row0=0, batch=dbatch, seq=dseq, blk=dseq)

    mem = mem_prompt.reshape(batch * MEM_LEN, D_MODEL)
    wmem = w_mem_kv[l]
    mk = _matmul(mem, wmem[:, :C_WIDTH].astype(BF16), tm=batch * MEM_LEN, tn=512, out_dtype=F32, name="proj_mem_k")
    mv = _matmul(mem, wmem[:, C_WIDTH:].astype(BF16), tm=batch * MEM_LEN, tn=512, out_dtype=F32, name="proj_mem_v")
    mk = mk.reshape(batch, MEM_LEN, C_WIDTH)
    mv = mv.reshape(batch, MEM_LEN, C_WIDTH)
    c_p = _cattn(zp["cq"], mk, mv, row0=0, batch=batch, seq=seq, tq=512)
    c_s = _cattn(zs["cq"], cache_mem_k[l].reshape(dbatch, MEM_LEN, C_WIDTH),
                 cache_mem_v[l].reshape(dbatch, MEM_LEN, C_WIDTH), row0=0, batch=dbatch, seq=dseq, tq=dseq)

    tm_r = 256
    h = _merge((a_p, b_p, c_p), (a_s, b_s, c_s), zp["gate"], zs["gate"], w_branch[l].astype(BF16), tm=tm_r)
    x1, x1p, idx, wts, pos, cnt = _out_ln_route(
        xp2, xs2, h, w_out[l].astype(BF16), ln1_g[l].reshape(1, D_MODEL), ln1_b[l].reshape(1, D_MODEL),
        w_router[l].T.astype(F32), router_bias[l].reshape(N_EXPERTS, 1).astype(F32), tm=tm_r)

    m = MOE_BLOCK
    n_pairs = n * TOP_K
    n_blocks = (n_pairs + m - 1) // m + N_EXPERTS
    counts = cnt[:, 0]
    padded = (counts + m - 1) // m * m
    pad_end = jnp.cumsum(padded)
    pad_start = pad_end - padded
    n_valid = pad_end[-1:] // m
    wts_t = jnp.transpose(wts, (0, 2, 1)).reshape(n, TOP_K)
    e_ids = jnp.arange(N_EXPERTS, dtype=jnp.int32)
    first_row = jnp.sum(jnp.where(idx[..., None] == e_ids, pad_start.astype(jnp.int32), 0), axis=-1)
    dest = first_row + pos
    blk_first = jnp.arange(n_blocks, dtype=jnp.int32) * m
    blk_e = jnp.sum((blk_first[:, None] >= pad_end[None, :]).astype(jnp.int32), axis=1)
    blk_e = jnp.minimum(blk_e, N_EXPERTS - 1)
    z_pad = jnp.where(padded > counts, pad_end - m, -1)
    z_tail = jnp.where(blk_first >= pad_end[-1], blk_first, -1)
    zrows = jnp.concatenate([z_pad, z_tail]).astype(jnp.int32)

    xs = _dispatch(zrows, dest, x1p, n_rows=n_blocks * m, tm=tm_r, block_rows=m)
    yb = _moe(xs, blk_e, n_valid.astype(jnp.int32), w_exp_gate[l].astype(BF16), w_exp_up[l].astype(BF16),
              w_exp_down[l].astype(BF16))
    tm_f = 128
    dest_blk = jnp.transpose(dest.reshape(n // tm_r, TOP_K, tm_r // tm_f, tm_f),
                             (0, 2, 1, 3)).reshape(n // tm_f, TOP_K, tm_f)
    fin = functools.partial(_final, dest_blk, x1, wts_t, yb, w_sh_gate[l].astype(BF16), w_sh_up[l].astype(BF16),
                            w_sh_down[l].astype(BF16), ln2_g[l].reshape(1, D_MODEL), ln2_b[l].reshape(1, D_MODEL),
                            tm=tm_f)
    y_p = fin(row0=0, rows=n_p, name="combine_shared_ln2_p").reshape(batch, seq, D_MODEL)
    y_s = fin(row0=n_p, rows=n_s, name="combine_shared_ln2_s").reshape(dbatch, dseq, D_MODEL)

    k_p = zp["ak"].reshape(batch, seq, A_KV_HEADS, A_HEAD_DIM)[:, -lc:]
    v_p = zp["av"].reshape(batch, seq, A_KV_HEADS, A_HEAD_DIM)[:, -lc:]
    k_s = zs["ak"].reshape(dbatch, dseq, A_KV_HEADS, A_HEAD_DIM)
    v_s = zs["av"].reshape(dbatch, dseq, A_KV_HEADS, A_HEAD_DIM)
    wk_s = jnp.concatenate([cache_win_k[l].astype(F32), k_s], axis=1)[:, -lc:]
    wv_s = jnp.concatenate([cache_win_v[l].astype(F32), v_s], axis=1)[:, -lc:]
    mk_o = mk.reshape(batch, MEM_LEN, C_HEADS, C_HEAD_DIM)
    mv_o = mv.reshape(batch, MEM_LEN, C_HEADS, C_HEAD_DIM)
    return (y_p, y_s, k_p[None], v_p[None], hs_p[None], mk_o[None], mv_o[None], wk_s[None], wv_s[None], hs_s[None])
```

```python
import functools

import jax
import jax.numpy as jnp
import numpy as np
from jax import lax
from jax.experimental import pallas as pl
from jax.experimental.pallas import tpu as pltpu

F32 = jnp.float32
BF16 = jnp.bfloat16

D_MODEL = 2048
DEPTH = 1
PAST_LEN = 2048
CHUNK = 64
A_HEADS = 16
A_KV_HEADS = 4
A_GROUP = A_HEADS // A_KV_HEADS
A_HEAD_DIM = 64
A_WIDTH = A_HEADS * A_HEAD_DIM
A_KV_WIDTH = A_KV_HEADS * A_HEAD_DIM
WINDOW = 128
WIN_CHUNKS = WINDOW // CHUNK
B_HEADS = 8
B_KEY_DIM = 128
B_VAL_DIM = 128
B_WIDTH = B_HEADS * B_VAL_DIM
SUB = 16
MEM_LEN = 256
C_HEADS = 4
C_HEAD_DIM = 256
C_WIDTH = C_HEADS * C_HEAD_DIM
N_EXPERTS = 64
N_GROUPS = 8
GROUP_SIZE = N_EXPERTS // N_GROUPS
TOPK_GROUPS = 4
TOP_K = 8
EXPERT_DIM = 512
ROUTED_SCALE = 2.5
DN_ALPHA = (2 * DEPTH) ** 0.25
LN_EPS = 1e-5
RMS_EPS = 1e-6

MOE_BLOCK = 256
VMEM_LIMIT = 56 * 1024 * 1024


def _cparams(sem):
    return pltpu.CompilerParams(dimension_semantics=sem, vmem_limit_bytes=VMEM_LIMIT)


def _resident(shape, index_map):
    return pl.BlockSpec(shape, index_map, pipeline_mode=pl.Buffered(1))


def _sigmoid(x):
    return 1.0 / (1.0 + jnp.exp(-x))


def _layer_norm(x, g, b):
    mu = jnp.mean(x, axis=-1, keepdims=True)
    xc = x - mu
    var = jnp.mean(xc * xc, axis=-1, keepdims=True)
    return xc * lax.rsqrt(var + LN_EPS) * g + b


def _mm_kernel(x_ref, w_ref, o_ref, *, scale):
    acc = jnp.dot(x_ref[...].astype(BF16), w_ref[...], preferred_element_type=F32)
    if scale != 1.0:
        acc = acc * scale
    o_ref[...] = acc.astype(o_ref.dtype)


def _matmul(x, w, *, tm, tn, out_dtype, scale=1.0, name):
    m, k = x.shape
    n = w.shape[1]
    return pl.pallas_call(
        functools.partial(_mm_kernel, scale=scale),
        grid=(m // tm, n // tn),
        in_specs=[pl.BlockSpec((tm, k), lambda i, j: (i, 0)), pl.BlockSpec((k, tn), lambda i, j: (0, j))],
        out_specs=pl.BlockSpec((tm, tn), lambda i, j: (i, j)),
        out_shape=jax.ShapeDtypeStruct((m, n), out_dtype),
        compiler_params=_cparams(("parallel", "parallel")),
        name=name,
    )(x, w)


SWA_KEYS = 256
HEAD_PAIR = 2 * A_HEAD_DIM


def _kv_head_planes(x):
    lane = lax.broadcasted_iota(jnp.int32, (1, HEAD_PAIR), 1)
    low = lane < A_HEAD_DIM
    planes = []
    for pair in range(A_KV_HEADS // 2):
        own = x[:, pair * HEAD_PAIR:(pair + 1) * HEAD_PAIR]
        swapped = pltpu.roll(own, A_HEAD_DIM, 1)
        planes.append((jnp.where(low, own, 0.0).astype(BF16), jnp.where(low, 0.0, swapped).astype(BF16)))
        planes.append((jnp.where(low, swapped, 0.0).astype(BF16), jnp.where(low, 0.0, own).astype(BF16)))
    return planes


SWA_LOOKAHEAD = 8


def _swa_chunks(q_ref, rows, chunk_rows, k_planes, v_planes, n_band, bias_ref, valid_of, o_ref):
    zeros_k = jnp.zeros((SWA_KEYS - n_band, HEAD_PAIR), BF16)
    ones_v = jnp.ones((SWA_KEYS, HEAD_PAIR), BF16)
    tiles = [(c, h, x) for c in range(len(chunk_rows)) for h in range(A_KV_HEADS) for x in range(2)]

    def band(plane, c):
        return jnp.concatenate([plane[chunk_rows[c]:chunk_rows[c] + n_band], zeros_k], axis=0)

    def scores(t):
        c, h, x = tiles[t]
        q2 = jnp.concatenate([q_ref[c * rows:(c + 1) * rows, (2 * h + j) * HEAD_PAIR:(2 * h + j + 1) * HEAD_PAIR]
                              for j in range(2)], axis=0).astype(BF16)
        s = lax.dot_general(q2, band(k_planes[h][x], c), (((1,), (1,)), ((), ())), preferred_element_type=F32)
        s = s - bias_ref[h, x]
        valid = valid_of(c)
        return s if valid is None else jnp.where(valid, s, -jnp.inf)

    pending = {t: scores(t) for t in range(min(SWA_LOOKAHEAD, len(tiles)))}
    even = None
    for t, (c, h, x) in enumerate(tiles):
        s = pending.pop(t)
        p = jnp.exp(s - jnp.max(s, axis=-1, keepdims=True)).astype(BF16)
        if t + SWA_LOOKAHEAD < len(tiles):
            pending[t + SWA_LOOKAHEAD] = scores(t + SWA_LOOKAHEAD)
        vw = jnp.concatenate([band(v_planes[h][x], c), ones_v], axis=1)
        o = jnp.dot(p, vw, preferred_element_type=F32)
        o = o[:, :HEAD_PAIR] / o[:, HEAD_PAIR:]
        if x == 0:
            even = o
        else:
            out = even + o
            for j in range(2):
                o_ref[c * rows:(c + 1) * rows, (2 * h + j) * HEAD_PAIR:(2 * h + j + 1) * HEAD_PAIR] = (
                    out[j * rows:(j + 1) * rows].astype(o_ref.dtype))


def _swa_prompt_kernel(q_ref, kc_ref, kp_ref, vc_ref, vp_ref, bias_ref, o_ref, *, n_chunks):
    i = pl.program_id(1)
    pad = WIN_CHUNKS * CHUNK
    n_band = pad + CHUNK
    k_planes = _kv_head_planes(jnp.concatenate([kp_ref[...], kc_ref[...]], axis=0))
    v_planes = _kv_head_planes(jnp.concatenate([vp_ref[...], vc_ref[...]], axis=0))
    s_idx = lax.broadcasted_iota(jnp.int32, (1, SWA_KEYS), 1)

    def valid_of(c):
        return jnp.logical_or(s_idx + (i * (n_chunks * CHUNK) + c * CHUNK - pad) >= 0, s_idx >= n_band)

    _swa_chunks(q_ref, CHUNK, [c * CHUNK for c in range(n_chunks)], k_planes, v_planes, n_band, bias_ref, valid_of,
                o_ref)


def _swa_sample_kernel(q_ref, kn_ref, kc_ref, vn_ref, vc_ref, bias_ref, o_ref):
    k_planes = _kv_head_planes(jnp.concatenate([kc_ref[0], kn_ref[...]], axis=0))
    v_planes = _kv_head_planes(jnp.concatenate([vc_ref[0], vn_ref[...]], axis=0))
    n_band = kc_ref.shape[1] + kn_ref.shape[0]
    _swa_chunks(q_ref, q_ref.shape[0], [0], k_planes, v_planes, n_band, bias_ref, lambda c: None, o_ref)


def _alibi_slopes():
    return (2.0 ** (-8.0 * np.arange(1, A_HEADS + 1) / A_HEADS)).astype(np.float32)


def _swa_bias(q_pos, k_pos, valid, sinks):
    n_q, n_k = len(q_pos), len(k_pos)
    assert n_k < SWA_KEYS
    dist = np.abs(q_pos[:, None] - k_pos[None, :]).astype(np.float32)
    band = _alibi_slopes()[:, None, None] * dist[None]
    if valid is not None:
        band = np.where(valid[None], band, np.inf)
    pad = np.full((A_HEADS, n_q, SWA_KEYS - n_k - 1), np.inf, np.float32)
    sink = jnp.broadcast_to(-sinks.astype(F32)[:, None, None], (A_HEADS, n_q, 1))
    bias = jnp.concatenate([jnp.asarray(band.astype(np.float32)), sink, jnp.asarray(pad)], axis=2)
    bias = bias.reshape(A_KV_HEADS, 2, 2, n_q, SWA_KEYS).transpose(0, 2, 1, 3, 4)
    return bias.reshape(A_KV_HEADS, 2, 2 * n_q, SWA_KEYS)


def _swa_prompt(q, k, v, sinks, *, batch, seq, n_chunks=4):
    tq = n_chunks * CHUNK
    pad = WIN_CHUNKS * CHUNK
    nb = seq // tq
    bias = _swa_bias(pad + np.arange(CHUNK), np.arange(pad + CHUNK), None, sinks)
    prev_per_blk = tq // pad

    def cur(b, i):
        return (b * nb + i, 0)

    def prev(b, i):
        return (jnp.maximum((b * nb + i) * prev_per_blk - 1, b * nb * prev_per_blk), 0)

    return pl.pallas_call(
        functools.partial(_swa_prompt_kernel, n_chunks=n_chunks),
        grid=(batch, nb),
        in_specs=[
            pl.BlockSpec((tq, A_WIDTH), cur),
            pl.BlockSpec((tq, A_KV_WIDTH), cur),
            pl.BlockSpec((pad, A_KV_WIDTH), prev),
            pl.BlockSpec((tq, A_KV_WIDTH), cur),
            pl.BlockSpec((pad, A_KV_WIDTH), prev),
            pl.BlockSpec(bias.shape, lambda b, i: (0, 0, 0, 0)),
        ],
        out_specs=pl.BlockSpec((tq, A_WIDTH), cur),
        out_shape=jax.ShapeDtypeStruct((batch * seq, A_WIDTH), BF16),
        compiler_params=_cparams(("parallel", "arbitrary")),
        name="swa_prompt",
    )(q, k, k, v, v, bias)


def _swa_sample(q, k, v, k_cache, v_cache, sinks, *, row0, batch, seq):
    lc = k_cache.shape[1]
    q_pos = PAST_LEN + np.arange(seq)
    k_pos = PAST_LEN - lc + np.arange(lc + seq)
    cdiff = q_pos[:, None] // CHUNK - k_pos[None, :] // CHUNK
    valid = (cdiff >= 0) & (cdiff <= WIN_CHUNKS)
    bias = _swa_bias(q_pos, k_pos, valid, sinks)
    blk0 = row0 // seq

    def rows(b):
        return (blk0 + b, 0)

    return pl.pallas_call(
        _swa_sample_kernel,
        grid=(batch,),
        in_specs=[
            pl.BlockSpec((seq, A_WIDTH), rows),
            pl.BlockSpec((seq, A_KV_WIDTH), rows),
            pl.BlockSpec((1, lc, A_KV_WIDTH), lambda b: (b, 0, 0)),
            pl.BlockSpec((seq, A_KV_WIDTH), rows),
            pl.BlockSpec((1, lc, A_KV_WIDTH), lambda b: (b, 0, 0)),
            pl.BlockSpec(bias.shape, lambda b: (0, 0, 0, 0)),
        ],
        out_specs=pl.BlockSpec((seq, A_WIDTH), lambda b: (b, 0)),
        out_shape=jax.ShapeDtypeStruct((batch * seq, A_WIDTH), BF16),
        compiler_params=_cparams(("parallel",)),
        name="swa_sample",
    )(q, k, k_cache, v, v_cache, bias)


def _split3(x):
    hi = x.astype(BF16)
    r1 = x - hi.astype(F32)
    mid = r1.astype(BF16)
    lo = (r1 - mid.astype(F32)).astype(BF16)
    return hi, mid, lo


def _dot3(mat, parts):
    acc = jnp.dot(mat, parts[0], preferred_element_type=F32)
    acc = acc + jnp.dot(mat, parts[1], preferred_element_type=F32)
    return acc + jnp.dot(mat, parts[2], preferred_element_type=F32)


def _hgrn_kernel(zq_ref, zf_ref, zi_ref, zg_ref, lb_ref, ng_ref, tri_ref, tsel_ref, s0_ref, o_ref, sfin_ref, s_scr,
                 *, blk):
    j = pl.program_id(1)

    @pl.when(j == 0)
    def _():
        s_scr[...] = s0_ref[0]

    bq = zq_ref[...]
    fl = zf_ref[...]
    v = zi_ref[...]
    bg = zg_ref[...]
    log_lb = lb_ref[0:1, :]
    log1m_lb = lb_ref[1:2, :]
    one_m_lb = lb_ref[2:3, :]

    q = bq * _sigmoid(bq) * (B_KEY_DIM ** -0.5)
    log_sig = jnp.minimum(fl, 0.0) - jnp.log1p(jnp.exp(-jnp.abs(fl)))
    c = log1m_lb + log_sig
    logf = jnp.maximum(log_lb, c) + jnp.log1p(jnp.exp(-jnp.abs(log_lb - c)))
    k = one_m_lb * _sigmoid(-fl)

    parts = _split3(logf)
    b = _dot3(tri_ref[...], parts)
    rq = _dot3(tsel_ref[...], parts)
    qt = q * jnp.exp(b - rq)
    qb = (q * jnp.exp(b)).astype(BF16)
    b_last = b[blk - 1:blk, :]
    khat = (k * jnp.exp(b_last - b)).astype(BF16)
    e_last = jnp.exp(b_last)
    vb = v.astype(BF16)
    row = lax.broadcasted_iota(jnp.int32, (blk, 1), 0)
    n_sub = blk // SUB
    kts = []
    for i in range(n_sub):
        r_i = rq[i * SUB:i * SUB + 1, :]
        kts.append(jnp.where(row < (i + 1) * SUB, k * jnp.exp(r_i - b), 0.0).astype(BF16))
    qt = qt.astype(BF16)
    tril = lax.broadcasted_iota(jnp.int32, (blk, blk), 0) >= lax.broadcasted_iota(jnp.int32, (blk, blk), 1)

    heads = [slice(h * B_KEY_DIM, (h + 1) * B_KEY_DIM) for h in range(B_HEADS)]
    s_old = [s_scr[h] for h in range(B_HEADS)]
    a_raw, o_state, ds = [], [], []
    for h, hs in enumerate(heads):
        a_raw.append(jnp.concatenate(
            [lax.dot_general(qt[i * SUB:(i + 1) * SUB, hs], kts[i][:, hs], (((1,), (1,)), ((), ())),
                             preferred_element_type=F32) for i in range(n_sub)], axis=0))
        o_state.append(jnp.dot(qb[:, hs], s_old[h].astype(BF16), preferred_element_type=F32))
        ds.append(lax.dot_general(khat[:, hs], vb[:, hs], (((0,), (0,)), ((), ())), preferred_element_type=F32))
    outs = []
    for h, hs in enumerate(heads):
        a = jnp.where(tril, a_raw[h], 0.0).astype(BF16)
        o = jnp.dot(a, vb[:, hs], preferred_element_type=F32) + o_state[h]
        decay = jnp.transpose(jnp.broadcast_to(e_last[:, hs], (B_KEY_DIM, B_KEY_DIM)))
        s_scr[h] = decay * s_old[h] + ds[h]
        o = o * lax.rsqrt(jnp.mean(o * o, axis=-1, keepdims=True) + RMS_EPS)
        outs.append(o)
    o_all = jnp.concatenate(outs, axis=1) * ng_ref[...] * (bg * _sigmoid(bg))
    o_ref[...] = o_all.astype(o_ref.dtype)

    @pl.when(j == pl.num_programs(1) - 1)
    def _():
        sfin_ref[0] = s_scr[...]


def _hgrn(zb, lbp, ng, s0, *, row0, batch, seq, blk):
    nb = seq // blk
    blk0 = row0 // blk
    t = np.arange(blk)
    tri = jnp.asarray((t[:, None] >= t[None, :]).astype(np.float32), dtype=BF16)
    tsel = jnp.asarray((t[None, :] < (t[:, None] // SUB) * SUB).astype(np.float32), dtype=BF16)

    def zspec(col):
        return pl.BlockSpec((blk, B_WIDTH), lambda b, j: (blk0 + b * nb + j, col))

    state_spec = pl.BlockSpec((1, B_HEADS, B_KEY_DIM, B_VAL_DIM), lambda b, j: (b, 0, 0, 0))
    return pl.pallas_call(
        functools.partial(_hgrn_kernel, blk=blk),
        grid=(batch, nb),
        in_specs=[
            zspec(0), zspec(1), zspec(2), zspec(3),
            pl.BlockSpec((3, B_WIDTH), lambda b, j: (0, 0)),
            pl.BlockSpec((1, B_WIDTH), lambda b, j: (0, 0)),
            pl.BlockSpec((blk, blk), lambda b, j: (0, 0)),
            pl.BlockSpec((blk, blk), lambda b, j: (0, 0)),
            state_spec,
        ],
        out_specs=[pl.BlockSpec((blk, B_WIDTH), lambda b, j: (b * nb + j, 0)), state_spec],
        out_shape=[jax.ShapeDtypeStruct((batch * seq, B_WIDTH), BF16),
                   jax.ShapeDtypeStruct((batch, B_HEADS, B_KEY_DIM, B_VAL_DIM), F32)],
        scratch_shapes=[pltpu.VMEM((B_HEADS, B_KEY_DIM, B_VAL_DIM), F32)],
        compiler_params=_cparams(("parallel", "arbitrary")),
        name=f"hgrn_blk{blk}",
    )(zb, zb, zb, zb, lbp, ng, tri, tsel, s0)


def _cattn_kernel(q_ref, mk_ref, mv_ref, o_ref):
    mk = mk_ref[0].astype(BF16)
    mv = mv_ref[0].astype(BF16)
    q = q_ref[...]
    for h in range(C_HEADS):
        hs = slice(h * C_HEAD_DIM, (h + 1) * C_HEAD_DIM)
        s = lax.dot_general(q[:, hs], mk[:, hs], (((1,), (1,)), ((), ())), preferred_element_type=F32)
        p = jnp.exp(s - jnp.max(s, axis=-1, keepdims=True))
        p = p / jnp.sum(p, axis=-1, keepdims=True)
        o_ref[:, hs] = jnp.dot(p.astype(BF16), mv[:, hs], preferred_element_type=F32).astype(o_ref.dtype)


def _cattn(q, mk, mv, *, row0, batch, seq, tq):
    nb = seq // tq
    blk0 = row0 // tq
    mem_spec = pl.BlockSpec((1, MEM_LEN, C_WIDTH), lambda b, i: (b, 0, 0))
    return pl.pallas_call(
        _cattn_kernel,
        grid=(batch, nb),
        in_specs=[pl.BlockSpec((tq, C_WIDTH), lambda b, i: (blk0 + b * nb + i, 0)), mem_spec, mem_spec],
        out_specs=pl.BlockSpec((tq, C_WIDTH), lambda b, i: (b * nb + i, 0)),
        out_shape=jax.ShapeDtypeStruct((batch * seq, C_WIDTH), BF16),
        compiler_params=_cparams(("parallel", "parallel")),
        name=f"cattn_tq{tq}",
    )(q, mk, mv)


def _two_segment_specs(block, n_first, n_second, col=0):
    first = pl.BlockSpec(block, lambda i: (jnp.minimum(i, n_first - 1), col))
    second = pl.BlockSpec(block, lambda i: (jnp.clip(i - n_first, 0, n_second - 1), col))
    return [first, second]


def _merge_kernel(ap_ref, as_ref, bp_ref, bs_ref, cp_ref, cs_ref, g0p_ref, g0s_ref, g1p_ref, g1s_ref, g2p_ref, g2s_ref,
                  pa_ref, pb_ref, pc_ref, o_ref, *, n_first):
    first = pl.program_id(0) < n_first

    def pick(p_ref, s_ref):
        return jnp.where(first, p_ref[...], s_ref[...])

    h = _sigmoid(pick(g0p_ref, g0s_ref)) * jnp.dot(pick(ap_ref, as_ref), pa_ref[...], preferred_element_type=F32)
    h = h + _sigmoid(pick(g1p_ref, g1s_ref)) * jnp.dot(pick(bp_ref, bs_ref), pb_ref[...], preferred_element_type=F32)
    h = h + _sigmoid(pick(g2p_ref, g2s_ref)) * jnp.dot(pick(cp_ref, cs_ref), pc_ref[...], preferred_element_type=F32)
    o_ref[...] = h.astype(o_ref.dtype)


def _merge(abc_p, abc_s, gl_p, gl_s, p, *, tm):
    n_p, width = abc_p[0].shape
    n_s = abc_s[0].shape[0]
    nf, ns = n_p // tm, n_s // tm
    in_specs = []
    args = []
    for x_p, x_s in zip(abc_p, abc_s):
        in_specs += _two_segment_specs((tm, width), nf, ns)
        args += [x_p, x_s]
    for col in range(3):
        in_specs += _two_segment_specs((tm, D_MODEL), nf, ns, col)
        args += [gl_p, gl_s]
    for blk in range(3):
        in_specs.append(_resident((width, D_MODEL), functools.partial(lambda i, b: (b, 0), b=blk)))
        args.append(p)
    return pl.pallas_call(
        functools.partial(_merge_kernel, n_first=nf),
        grid=(nf + ns,),
        in_specs=in_specs,
        out_specs=pl.BlockSpec((tm, D_MODEL), lambda i: (i, 0)),
        out_shape=jax.ShapeDtypeStruct((n_p + n_s, D_MODEL), BF16),
        compiler_params=_cparams(("parallel",)),
        name="merge",
    )(*args)


def _first_index(hit_src, m, iota, size, axis):
    return jnp.min(jnp.where(hit_src == m, iota, size), axis=axis, keepdims=True)


def _route(x1, wr_ref, rb_ref, upper_ref, run_ref, idx_ref, wts_ref, pos_ref):
    tm = x1.shape[0]
    logits = lax.dot_general(wr_ref[...], x1, (((1,), (1,)), ((), ())), precision=lax.Precision.HIGHEST,
                             preferred_element_type=F32)
    scores = _sigmoid(logits)
    choice = scores + rb_ref[...]
    g3 = choice.reshape(N_GROUPS, GROUP_SIZE, tm)
    mem_iota = lax.broadcasted_iota(jnp.int32, g3.shape, 1)
    m1 = jnp.max(g3, axis=1, keepdims=True)
    first = _first_index(g3, m1, mem_iota, GROUP_SIZE, 1)
    m2 = jnp.max(jnp.where(mem_iota == first, -jnp.inf, g3), axis=1, keepdims=True)
    gscore = (m1 + m2).reshape(N_GROUPS, tm)
    g_iota = lax.broadcasted_iota(jnp.int32, gscore.shape, 0)
    gsel = jnp.zeros(gscore.shape, F32)
    cur = gscore
    for _ in range(TOPK_GROUPS):
        m = jnp.max(cur, axis=0, keepdims=True)
        hit = g_iota == _first_index(cur, m, g_iota, N_GROUPS, 0)
        gsel = jnp.where(hit, 1.0, gsel)
        cur = jnp.where(hit, -jnp.inf, cur)
    masked = jnp.where(gsel.reshape(N_GROUPS, 1, tm) > 0.5, g3, -jnp.inf).reshape(N_EXPERTS, tm)
    e_iota = lax.broadcasted_iota(jnp.int32, masked.shape, 0)
    sel = jnp.zeros(masked.shape, F32)
    cur = masked
    idxs, ws = [], []
    for _ in range(TOP_K):
        m = jnp.max(cur, axis=0, keepdims=True)
        ei = _first_index(cur, m, e_iota, N_EXPERTS, 0)
        hit = e_iota == ei
        idxs.append(ei)
        ws.append(jnp.sum(jnp.where(hit, scores, 0.0), axis=0, keepdims=True))
        sel = jnp.where(hit, 1.0, sel)
        cur = jnp.where(hit, -jnp.inf, cur)
    w = jnp.concatenate(ws, axis=0)
    w = w / jnp.sum(w, axis=0, keepdims=True) * ROUTED_SCALE
    excl = jnp.dot(sel.astype(BF16), upper_ref[...], preferred_element_type=F32)
    posfull = excl + run_ref[:, 0:1]
    run_ref[...] = run_ref[...] + jnp.sum(sel, axis=1, keepdims=True)
    pos = [jnp.sum(jnp.where(e_iota == ei, posfull, 0.0), axis=0, keepdims=True) for ei in idxs]
    idx_ref[0] = jnp.concatenate(idxs, axis=0)
    wts_ref[0] = w
    pos_ref[0] = jnp.concatenate(pos, axis=0).astype(jnp.int32)


def _pack_halves(x):
    c = x.shape[1] // 2
    lo = lax.bitcast_convert_type(x[:, :c].astype(BF16).astype(F32), jnp.uint32)
    hi = lax.bitcast_convert_type(x[:, c:].astype(BF16).astype(F32), jnp.uint32)
    return (hi & jnp.uint32(0xFFFF0000)) | (lo >> 16)


def _unpack_halves(w):
    lo = lax.bitcast_convert_type(w << 16, F32)
    hi = lax.bitcast_convert_type(w & jnp.uint32(0xFFFF0000), F32)
    return lo, hi


ROW_TILE = 8
PACKED_LANES = D_MODEL // 2 // ROW_TILE


def _store_row_tiles(ref, val):
    rows = val.shape[0]
    for s in range(ROW_TILE):
        ref[pl.ds(s, rows, stride=ROW_TILE), :] = val[:, s * PACKED_LANES:(s + 1) * PACKED_LANES]


def _load_row_tiles(ref, rows):
    return [ref[pl.ds(s, rows, stride=ROW_TILE), :] for s in range(ROW_TILE)]


def _out_ln_route_kernel(xp_ref, xs_ref, h_ref, wo_ref, g_ref, b_ref, wr_ref, rb_ref, upper_ref,
                         x1_ref, x1p_ref, idx_ref, wts_ref, pos_ref, cnt_ref, run_ref, *, n_first):
    i = pl.program_id(0)

    @pl.when(i == 0)
    def _():
        run_ref[...] = jnp.zeros_like(run_ref)

    x = jnp.where(i < n_first, xp_ref[...], xs_ref[...])
    y = DN_ALPHA * x + jnp.dot(h_ref[...], wo_ref[...], preferred_element_type=F32)
    x1 = _layer_norm(y, g_ref[...], b_ref[...])
    x1_ref[...] = x1
    _store_row_tiles(x1p_ref, _pack_halves(x1))
    _route(x1, wr_ref, rb_ref, upper_ref, run_ref, idx_ref, wts_ref, pos_ref)
    cnt_ref[...] = run_ref[...].astype(jnp.int32)


def _out_ln_route(x_p, x_s, h, w_out, ln_g, ln_b, w_router_t, router_bias, *, tm):
    n = h.shape[0]
    nt = n // tm
    nf, ns = x_p.shape[0] // tm, x_s.shape[0] // tm
    t = np.arange(tm)
    upper = jnp.asarray((t[:, None] < t[None, :]).astype(np.float32), dtype=BF16)

    def rows(i):
        return (i, 0)

    def const(i):
        return (0, 0)

    small = pl.BlockSpec((1, TOP_K, tm), lambda i: (i, 0, 0))
    return pl.pallas_call(
        functools.partial(_out_ln_route_kernel, n_first=nf),
        grid=(nt,),
        in_specs=_two_segment_specs((tm, D_MODEL), nf, ns) + [
            pl.BlockSpec((tm, D_MODEL), rows),
            _resident((D_MODEL, D_MODEL), const),
            pl.BlockSpec((1, D_MODEL), const),
            pl.BlockSpec((1, D_MODEL), const),
            pl.BlockSpec((N_EXPERTS, D_MODEL), const),
            pl.BlockSpec((N_EXPERTS, 1), const),
            pl.BlockSpec((tm, tm), const),
        ],
        out_specs=[pl.BlockSpec((tm, D_MODEL), rows), pl.BlockSpec((tm * ROW_TILE, PACKED_LANES), rows), small, small,
                   small, pl.BlockSpec((N_EXPERTS, 128), const)],
        out_shape=[
            jax.ShapeDtypeStruct((n, D_MODEL), F32),
            jax.ShapeDtypeStruct((n * ROW_TILE, PACKED_LANES), jnp.uint32),
            jax.ShapeDtypeStruct((nt, TOP_K, tm), jnp.int32),
            jax.ShapeDtypeStruct((nt, TOP_K, tm), F32),
            jax.ShapeDtypeStruct((nt, TOP_K, tm), jnp.int32),
            jax.ShapeDtypeStruct((N_EXPERTS, 128), jnp.int32),
        ],
        scratch_shapes=[pltpu.VMEM((N_EXPERTS, 128), F32)],
        compiler_params=_cparams(("arbitrary",)),
        name="out_ln_route",
    )(x_p, x_s, h, w_out, ln_g, ln_b, w_router_t, router_bias, upper)


def _row_copy(src_hbm, row, dst, dst_row, sem):
    return pltpu.make_async_copy(src_hbm.at[pl.ds(row, 1)], dst.at[pl.ds(dst_row, 1)], sem)


def _tile_copy(src, src_row, dst, dst_row, sem):
    def tile(row):
        first = row * ROW_TILE
        return pl.ds(first if isinstance(row, int) else pl.multiple_of(first, ROW_TILE), ROW_TILE)

    return pltpu.make_async_copy(src.at[tile(src_row)], dst.at[tile(dst_row)], sem)


def _dispatch_kernel(zrow_ref, dest_ref, x_ref, o_hbm, zbuf, sem, zsem, *, block_rows):
    tm = x_ref.shape[0] // ROW_TILE
    n_zero = zrow_ref.shape[0]

    def zero_copy(b):
        start = pl.multiple_of(jnp.maximum(zrow_ref[b], 0) * ROW_TILE, ROW_TILE)
        return pltpu.make_async_copy(zbuf, o_hbm.at[pl.ds(start, block_rows * ROW_TILE)], zsem)

    @pl.when(pl.program_id(0) == 0)
    def _():
        zbuf[...] = jnp.zeros_like(zbuf)

        def start(b, carry):
            @pl.when(zrow_ref[b] >= 0)
            def _():
                zero_copy(b).start()
            return carry

        def wait(b, carry):
            @pl.when(zrow_ref[b] >= 0)
            def _():
                zero_copy(b).wait()
            return carry

        lax.fori_loop(0, n_zero, start, 0)
        lax.fori_loop(0, n_zero, wait, 0)

    for t in range(tm):
        for k in range(TOP_K):
            _tile_copy(x_ref, t, o_hbm, dest_ref[0, k, t], sem).start(priority=k % 2)
    for k in range(TOP_K):
        pltpu.make_async_copy(x_ref, o_hbm.at[pl.ds(0, tm * ROW_TILE)], sem).wait()


def _dispatch(zrows, dest_blk, x1p, *, n_rows, tm, block_rows):
    n = x1p.shape[0] // ROW_TILE
    grid_spec = pltpu.PrefetchScalarGridSpec(
        num_scalar_prefetch=1,
        grid=(n // tm,),
        in_specs=[
            pl.BlockSpec((1, TOP_K, tm), lambda i, z: (i, 0, 0), memory_space=pltpu.SMEM),
            pl.BlockSpec((tm * ROW_TILE, PACKED_LANES), lambda i, z: (i, 0)),
        ],
        out_specs=pl.BlockSpec(memory_space=pl.ANY),
        scratch_shapes=[pltpu.VMEM((block_rows * ROW_TILE, PACKED_LANES), jnp.uint32),
                        pltpu.SemaphoreType.DMA(()), pltpu.SemaphoreType.DMA(())],
    )
    return pl.pallas_call(
        functools.partial(_dispatch_kernel, block_rows=block_rows),
        grid_spec=grid_spec,
        out_shape=jax.ShapeDtypeStruct((n_rows * ROW_TILE, PACKED_LANES), jnp.uint32),
        compiler_params=_cparams(("arbitrary",)),
        name="dispatch",
    )(zrows, dest_blk, x1p)


def _moe_kernel(be_ref, nv_ref, x_ref, wg_ref, wu_ref, wd_ref, o_ref, wg_b, wu_b, wd_b):
    k = pl.program_id(0)
    m = x_ref.shape[0] // ROW_TILE
    live = k < nv_ref[0]
    new_expert = jnp.logical_or(k == 0, be_ref[k] != be_ref[jnp.maximum(k - 1, 0)])

    @pl.when(jnp.logical_and(live, new_expert))
    def _():
        wg_b[...] = wg_ref[0].astype(BF16)
        wu_b[...] = wu_ref[0].astype(BF16)
        wd_b[...] = wd_ref[0].astype(BF16)

    @pl.when(live)
    def _():
        lo, hi = _unpack_halves(jnp.concatenate(_load_row_tiles(x_ref, m), axis=1))
        x = jnp.concatenate([lo.astype(BF16), hi.astype(BF16)], axis=1)
        g = jnp.dot(x, wg_b[...], preferred_element_type=F32)
        u = jnp.dot(x, wu_b[...], preferred_element_type=F32)
        hmid = (g * _sigmoid(g) * u).astype(BF16)
        y = jnp.dot(hmid, wd_b[...], preferred_element_type=F32)
        _store_row_tiles(o_ref, _pack_halves(y))

    @pl.when(pl.program_id(0) >= nv_ref[0])
    def _():
        o_ref[...] = jnp.zeros_like(o_ref)


def _moe(xs, blk_e, n_valid, w_gate, w_up, w_down):
    n_blocks = blk_e.shape[0]
    m = MOE_BLOCK

    def rows(k, be, nv):
        return (jnp.minimum(k, nv[0] - 1), 0)

    def wspec(shape):
        return pl.BlockSpec(shape, lambda k, be, nv: (be[k], 0, 0))

    grid_spec = pltpu.PrefetchScalarGridSpec(
        num_scalar_prefetch=2,
        grid=(n_blocks,),
        in_specs=[
            pl.BlockSpec((m * ROW_TILE, PACKED_LANES), rows),
            wspec((1, D_MODEL, EXPERT_DIM)), wspec((1, D_MODEL, EXPERT_DIM)), wspec((1, EXPERT_DIM, D_MODEL)),
        ],
        out_specs=pl.BlockSpec((m * ROW_TILE, PACKED_LANES), lambda k, be, nv: (k, 0)),
        scratch_shapes=[pltpu.VMEM((D_MODEL, EXPERT_DIM), BF16), pltpu.VMEM((D_MODEL, EXPERT_DIM), BF16),
                        pltpu.VMEM((EXPERT_DIM, D_MODEL), BF16)],
    )
    return pl.pallas_call(
        _moe_kernel,
        grid_spec=grid_spec,
        out_shape=jax.ShapeDtypeStruct(xs.shape, jnp.uint32),
        compiler_params=_cparams(("arbitrary",)),
        name="moe_experts",
    )(blk_e, n_valid, xs, w_gate, w_up, w_down)


def _final_kernel(dest_ref, dest_next_ref, x1_ref, w_ref, yb_hbm, sg_ref, su_ref, sd_ref, g_ref, b_ref, o_ref,
                  ybuf, sem):
    i = pl.program_id(0)
    tm = x1_ref.shape[0]

    def start(d_ref, slot, t):
        for k in range(TOP_K):
            _tile_copy(yb_hbm, d_ref[0, k, t], ybuf.at[slot, k], t, sem.at[slot]).start(priority=k % 2)

    def wait_all(slot):
        for k in range(TOP_K):
            pltpu.make_async_copy(yb_hbm.at[pl.ds(0, tm * ROW_TILE)], ybuf.at[slot, k], sem.at[slot]).wait()

    @pl.when(i == 0)
    def _():
        def issue(t, carry):
            start(dest_ref, 0, t)
            return carry

        lax.fori_loop(0, tm, issue, 0, unroll=4)

    cur = i % 2
    nxt = 1 - cur
    wait_all(cur)
    x1 = x1_ref[...]
    xb = x1.astype(BF16)
    g = jnp.dot(xb, sg_ref[...], preferred_element_type=F32)
    u = jnp.dot(xb, su_ref[...], preferred_element_type=F32)
    w = w_ref[...]
    per = tm // TOP_K
    r_lo = [None] * ROW_TILE
    r_hi = [None] * ROW_TILE
    for k in range(TOP_K):
        for t in range(k * per, (k + 1) * per):
            start(dest_next_ref, nxt, t)
        wk = jnp.broadcast_to(w[:, k:k + 1], (tm, PACKED_LANES))
        for s, piece in enumerate(_load_row_tiles(ybuf.at[cur, k], tm)):
            lo, hi = _unpack_halves(piece)
            r_lo[s] = wk * lo if r_lo[s] is None else r_lo[s] + wk * lo
            r_hi[s] = wk * hi if r_hi[s] is None else r_hi[s] + wk * hi
    y = jnp.dot((g * _sigmoid(g) * u).astype(BF16), sd_ref[...], preferred_element_type=F32)
    routed = jnp.concatenate(r_lo + r_hi, axis=1)
    o_ref[...] = _layer_norm(DN_ALPHA * x1 + (routed + y), g_ref[...], b_ref[...])

    @pl.when(i == pl.num_programs(0) - 1)
    def _():
        wait_all(nxt)


def _final(dest, x1, wts_t, yb, w_sg, w_su, w_sd, ln_g, ln_b, *, tm, row0, rows, name):
    blk0 = row0 // tm
    n_tiles = rows // tm

    def tile(i):
        return (blk0 + i, 0)

    def const(i):
        return (0, 0)

    return pl.pallas_call(
        _final_kernel,
        grid=(n_tiles,),
        in_specs=[
            pl.BlockSpec((1, TOP_K, tm), lambda i: (blk0 + i, 0, 0), memory_space=pltpu.SMEM),
            pl.BlockSpec((1, TOP_K, tm), lambda i: (blk0 + jnp.minimum(i + 1, n_tiles - 1), 0, 0),
                         memory_space=pltpu.SMEM),
            pl.BlockSpec((tm, D_MODEL), tile),
            pl.BlockSpec((tm, TOP_K), tile),
            pl.BlockSpec(memory_space=pl.ANY),
            _resident((D_MODEL, EXPERT_DIM), const),
            _resident((D_MODEL, EXPERT_DIM), const),
            _resident((EXPERT_DIM, D_MODEL), const),
            pl.BlockSpec((1, D_MODEL), const),
            pl.BlockSpec((1, D_MODEL), const),
        ],
        out_specs=pl.BlockSpec((tm, D_MODEL), lambda i: (i, 0)),
        out_shape=jax.ShapeDtypeStruct((rows, D_MODEL), F32),
        scratch_shapes=[pltpu.VMEM((2, TOP_K, tm * ROW_TILE, PACKED_LANES), jnp.uint32),
                        pltpu.SemaphoreType.DMA((2,))],
        compiler_params=_cparams(("arbitrary",)),
        name=name,
    )(dest, dest, x1, wts_t, yb, w_sg, w_su, w_sd, ln_g, ln_b)


def kernel(x_prompt, x_sample, cache_win_k, cache_win_v, state_hgrn, cache_mem_k, cache_mem_v, mem_prompt,
           w_in, w_mem_kv, a_sinks, b_lb_logits, b_norm_g, w_branch, w_out, ln1_g, ln1_b,
           w_router, router_bias, w_exp_gate, w_exp_up, w_exp_down, w_sh_gate, w_sh_up, w_sh_down, ln2_g, ln2_b):
    assert w_in.shape[0] == DEPTH == 1
    batch, seq, _ = x_prompt.shape
    dbatch, dseq, _ = x_sample.shape
    n_p = batch * seq
    n_s = dbatch * dseq
    n = n_p + n_s
    l = 0

    xp2 = x_prompt.reshape(n_p, D_MODEL)
    xs2 = x_sample.reshape(n_s, D_MODEL)
    win = w_in[l]
    o_k, o_v, o_b = A_WIDTH, A_WIDTH + A_KV_WIDTH, A_WIDTH + 2 * A_KV_WIDTH
    o_c = o_b + 4 * B_WIDTH
    o_g = o_c + C_WIDTH
    groups = [
        ("aq", 0, o_k, 1024, BF16, A_HEAD_DIM ** -0.5),
        ("ak", o_k, o_v, A_KV_WIDTH, F32, 1.0),
        ("av", o_v, o_b, A_KV_WIDTH, F32, 1.0),
        ("hgrn", o_b, o_c, 1024, F32, 1.0),
        ("cq", o_c, o_g, 1024, BF16, C_HEAD_DIM ** -0.5),
        ("gate", o_g, win.shape[1], 1024, F32, 1.0),
    ]
    zp, zs = {}, {}
    for gname, c0, c1, tn, dt, scale in groups:
        wslice = win[:, c0:c1].astype(BF16)
        zp[gname] = _matmul(xp2, wslice, tm=1024, tn=tn, out_dtype=dt, scale=scale, name=f"proj_{gname}_p")
        zs[gname] = _matmul(xs2, wslice, tm=n_s, tn=tn, out_dtype=dt, scale=scale, name=f"proj_{gname}_s")

    a_p = _swa_prompt(zp["aq"], zp["ak"], zp["av"], a_sinks[l], batch=batch, seq=seq)
    lc = cache_win_k.shape[2]
    kc = cache_win_k[l].reshape(dbatch, lc, A_KV_WIDTH)
    vc = cache_win_v[l].reshape(dbatch, lc, A_KV_WIDTH)
    a_s = _swa_sample(zs["aq"], zs["ak"], zs["av"], kc, vc, a_sinks[l], row0=0, batch=dbatch, seq=dseq)

    lower = jnp.cumsum(jax.nn.softmax(b_lb_logits.astype(F32), axis=0), axis=0)[l]
    lbp = jnp.stack([jnp.log(lower), jnp.log1p(-lower), 1.0 - lower])
    ng = jnp.tile(b_norm_g[l].astype(F32), B_HEADS).reshape(1, B_WIDTH)
    s_zero = jnp.zeros((batch, B_HEADS, B_KEY_DIM, B_VAL_DIM), F32)
    b_p, hs_p = _hgrn(zp["hgrn"], lbp, ng, s_zero, row0=0, batch=batch, seq=seq, blk=CHUNK)
    b_s, hs_s = _hgrn(zs["hgrn"], lbp, ng, state_hgrn[l].astype(F32), row0=0, batch=dbatch, seq=dseq, blk=dseq)

    mem = mem_prompt.reshape(batch * MEM_LEN, D_MODEL)
    wmem = w_mem_kv[l]
    mk = _matmul(mem, wmem[:, :C_WIDTH].astype(BF16), tm=batch * MEM_LEN, tn=512, out_dtype=F32, name="proj_mem_k")
    mv = _matmul(mem, wmem[:, C_WIDTH:].astype(BF16), tm=batch * MEM_LEN, tn=512, out_dtype=F32, name="proj_mem_v")
    mk = mk.reshape(batch, MEM_LEN, C_WIDTH)
    mv = mv.reshape(batch, MEM_LEN, C_WIDTH)
    c_p = _cattn(zp["cq"], mk, mv, row0=0, batch=batch, seq=seq, tq=512)
    c_s = _cattn(zs["cq"], cache_mem_k[l].reshape(dbatch, MEM_LEN, C_WIDTH),
                 cache_mem_v[l].reshape(dbatch, MEM_LEN, C_WIDTH), row0=0, batch=dbatch, seq=dseq, tq=dseq)

    tm_r = 256
    h = _merge((a_p, b_p, c_p), (a_s, b_s, c_s), zp["gate"], zs["gate"], w_branch[l].astype(BF16), tm=tm_r)
    x1, x1p, idx, wts, pos, cnt = _out_ln_route(
        xp2, xs2, h, w_out[l].astype(BF16), ln1_g[l].reshape(1, D_MODEL), ln1_b[l].reshape(1, D_MODEL),
        w_router[l].T.astype(F32), router_bias[l].reshape(N_EXPERTS, 1).astype(F32), tm=tm_r)

    m = MOE_BLOCK
    n_pairs = n * TOP_K
    n_blocks = (n_pairs + m - 1) // m + N_EXPERTS
    counts = cnt[:, 0]
    padded = (counts + m - 1) // m * m
    pad_end = jnp.cumsum(padded)
    pad_start = pad_end - padded
    n_valid = pad_end[-1:] // m
    wts_t = jnp.transpose(wts, (0, 2, 1)).reshape(n, TOP_K)
    e_ids = jnp.arange(N_EXPERTS, dtype=jnp.int32)
    first_row = jnp.sum(jnp.where(idx[..., None] == e_ids, pad_start.astype(jnp.int32), 0), axis=-1)
    dest = first_row + pos
    blk_first = jnp.arange(n_blocks, dtype=jnp.int32) * m
    blk_e = jnp.sum((blk_first[:, None] >= pad_end[None, :]).astype(jnp.int32), axis=1)
    blk_e = jnp.minimum(blk_e, N_EXPERTS - 1)
    z_pad = jnp.where(padded > counts, pad_end - m, -1)
    z_tail = jnp.where(blk_first >= pad_end[-1], blk_first, -1)
    zrows = jnp.concatenate([z_pad, z_tail]).astype(jnp.int32)

    xs = _dispatch(zrows, dest, x1p, n_rows=n_blocks * m, tm=tm_r, block_rows=m)
    yb = _moe(xs, blk_e, n_valid.astype(jnp.int32), w_exp_gate[l].astype(F32), w_exp_up[l].astype(F32),
              w_exp_down[l].astype(F32))
    tm_f = 128
    dest_blk = jnp.transpose(dest.reshape(n // tm_r, TOP_K, tm_r // tm_f, tm_f),
                             (0, 2, 1, 3)).reshape(n // tm_f, TOP_K, tm_f)
    fin = functools.partial(_final, dest_blk, x1, wts_t, yb, w_sh_gate[l].astype(BF16), w_sh_up[l].astype(BF16),
                            w_sh_down[l].astype(BF16), ln2_g[l].reshape(1, D_MODEL), ln2_b[l].reshape(1, D_MODEL),
                            tm=tm_f)
    y_p = fin(row0=0, rows=n_p, name="combine_shared_ln2_p").reshape(batch, seq, D_MODEL)
    y_s = fin(row0=n_p, rows=n_s, name="combine_shared_ln2_s").reshape(dbatch, dseq, D_MODEL)

    k_p = zp["ak"].reshape(batch, seq, A_KV_HEADS, A_HEAD_DIM)[:, -lc:]
    v_p = zp["av"].reshape(batch, seq, A_KV_HEADS, A_HEAD_DIM)[:, -lc:]
    k_s = zs["ak"].reshape(dbatch, dseq, A_KV_HEADS, A_HEAD_DIM)
    v_s = zs["av"].reshape(dbatch, dseq, A_KV_HEADS, A_HEAD_DIM)
    wk_s = jnp.concatenate([cache_win_k[l].astype(F32), k_s], axis=1)[:, -lc:]
    wv_s = jnp.concatenate([cache_win_v[l].astype(F32), v_s], axis=1)[:, -lc:]
    mk_o = mk.reshape(batch, MEM_LEN, C_HEADS, C_HEAD_DIM)
    mv_o = mv.reshape(batch, MEM_LEN, C_HEADS, C_HEAD_DIM)
    return (y_p, y_s, k_p[None], v_p[None], hs_p[None], mk_o[None], mv_o[None], wk_s[None], wv_s[None], hs_s[None])
```

```python
import functools

import jax
import jax.numpy as jnp
import numpy as np
from jax import lax
from jax.experimental import pallas as pl
from jax.experimental.pallas import tpu as pltpu

F32 = jnp.float32
BF16 = jnp.bfloat16

D_MODEL = 2048
DEPTH = 1
PAST_LEN = 2048
CHUNK = 64
A_HEADS = 16
A_KV_HEADS = 4
A_GROUP = A_HEADS // A_KV_HEADS
A_HEAD_DIM = 64
A_WIDTH = A_HEADS * A_HEAD_DIM
A_KV_WIDTH = A_KV_HEADS * A_HEAD_DIM
WINDOW = 128
WIN_CHUNKS = WINDOW // CHUNK
B_HEADS = 8
B_KEY_DIM = 128
B_VAL_DIM = 128
B_WIDTH = B_HEADS * B_VAL_DIM
SUB = 16
MEM_LEN = 256
C_HEADS = 4
C_HEAD_DIM = 256
C_WIDTH = C_HEADS * C_HEAD_DIM
N_EXPERTS = 64
N_GROUPS = 8
GROUP_SIZE = N_EXPERTS // N_GROUPS
TOPK_GROUPS = 4
TOP_K = 8
EXPERT_DIM = 512
ROUTED_SCALE = 2.5
DN_ALPHA = (2 * DEPTH) ** 0.25
LN_EPS = 1e-5
RMS_EPS = 1e-6

MOE_BLOCK = 512
VMEM_LIMIT = 56 * 1024 * 1024


def _cparams(sem):
    return pltpu.CompilerParams(dimension_semantics=sem, vmem_limit_bytes=VMEM_LIMIT)


def _resident(shape, index_map):
    return pl.BlockSpec(shape, index_map, pipeline_mode=pl.Buffered(1))


def _sigmoid(x):
    return 1.0 / (1.0 + jnp.exp(-x))


def _layer_norm(x, g, b):
    mu = jnp.mean(x, axis=-1, keepdims=True)
    xc = x - mu
    var = jnp.mean(xc * xc, axis=-1, keepdims=True)
    return xc * lax.rsqrt(var + LN_EPS) * g + b


def _mm_kernel(x_ref, w_ref, o_ref, *, scale):
    acc = jnp.dot(x_ref[...].astype(BF16), w_ref[...], preferred_element_type=F32)
    if scale != 1.0:
        acc = acc * scale
    o_ref[...] = acc.astype(o_ref.dtype)


def _matmul(x, w, *, tm, tn, out_dtype, scale=1.0, name):
    m, k = x.shape
    n = w.shape[1]
    return pl.pallas_call(
        functools.partial(_mm_kernel, scale=scale),
        grid=(m // tm, n // tn),
        in_specs=[pl.BlockSpec((tm, k), lambda i, j: (i, 0)), pl.BlockSpec((k, tn), lambda i, j: (0, j))],
        out_specs=pl.BlockSpec((tm, tn), lambda i, j: (i, j)),
        out_shape=jax.ShapeDtypeStruct((m, n), out_dtype),
        compiler_params=_cparams(("parallel", "parallel")),
        name=name,
    )(x, w)


SWA_KEYS = 256
HEAD_PAIR = 2 * A_HEAD_DIM


def _kv_head_planes(x):
    lane = lax.broadcasted_iota(jnp.int32, (1, HEAD_PAIR), 1)
    low = lane < A_HEAD_DIM
    planes = []
    for pair in range(A_KV_HEADS // 2):
        own = x[:, pair * HEAD_PAIR:(pair + 1) * HEAD_PAIR]
        swapped = pltpu.roll(own, A_HEAD_DIM, 1)
        planes.append((jnp.where(low, own, 0.0).astype(BF16), jnp.where(low, 0.0, swapped).astype(BF16)))
        planes.append((jnp.where(low, swapped, 0.0).astype(BF16), jnp.where(low, 0.0, own).astype(BF16)))
    return planes


SWA_LOOKAHEAD = 8


def _swa_chunks(q_ref, rows, chunk_rows, k_planes, v_planes, n_band, bias_ref, valid_of, o_ref):
    zeros_k = jnp.zeros((SWA_KEYS - n_band, HEAD_PAIR), BF16)
    ones_v = jnp.ones((SWA_KEYS, HEAD_PAIR), BF16)
    tiles = [(c, h, x) for c in range(len(chunk_rows)) for h in range(A_KV_HEADS) for x in range(2)]

    def band(plane, c):
        return jnp.concatenate([plane[chunk_rows[c]:chunk_rows[c] + n_band], zeros_k], axis=0)

    def scores(t):
        c, h, x = tiles[t]
        q2 = jnp.concatenate([q_ref[c * rows:(c + 1) * rows, (2 * h + j) * HEAD_PAIR:(2 * h + j + 1) * HEAD_PAIR]
                              for j in range(2)], axis=0).astype(BF16)
        s = lax.dot_general(q2, band(k_planes[h][x], c), (((1,), (1,)), ((), ())), preferred_element_type=F32)
        s = s - bias_ref[h, x]
        valid = valid_of(c)
        return s if valid is None else jnp.where(valid, s, -jnp.inf)

    pending = {t: scores(t) for t in range(min(SWA_LOOKAHEAD, len(tiles)))}
    even = None
    for t, (c, h, x) in enumerate(tiles):
        s = pending.pop(t)
        p = jnp.exp(s - jnp.max(s, axis=-1, keepdims=True)).astype(BF16)
        if t + SWA_LOOKAHEAD < len(tiles):
            pending[t + SWA_LOOKAHEAD] = scores(t + SWA_LOOKAHEAD)
        vw = jnp.concatenate([band(v_planes[h][x], c), ones_v], axis=1)
        o = jnp.dot(p, vw, preferred_element_type=F32)
        o = o[:, :HEAD_PAIR] / o[:, HEAD_PAIR:]
        if x == 0:
            even = o
        else:
            out = even + o
            for j in range(2):
                o_ref[c * rows:(c + 1) * rows, (2 * h + j) * HEAD_PAIR:(2 * h + j + 1) * HEAD_PAIR] = (
                    out[j * rows:(j + 1) * rows].astype(o_ref.dtype))


def _swa_prompt_kernel(q_ref, kc_ref, kp_ref, vc_ref, vp_ref, bias_ref, o_ref, *, n_chunks):
    i = pl.program_id(1)
    pad = WIN_CHUNKS * CHUNK
    n_band = pad + CHUNK
    k_planes = _kv_head_planes(jnp.concatenate([kp_ref[...], kc_ref[...]], axis=0))
    v_planes = _kv_head_planes(jnp.concatenate([vp_ref[...], vc_ref[...]], axis=0))
    s_idx = lax.broadcasted_iota(jnp.int32, (1, SWA_KEYS), 1)

    def valid_of(c):
        return jnp.logical_or(s_idx + (i * (n_chunks * CHUNK) + c * CHUNK - pad) >= 0, s_idx >= n_band)

    _swa_chunks(q_ref, CHUNK, [c * CHUNK for c in range(n_chunks)], k_planes, v_planes, n_band, bias_ref, valid_of,
                o_ref)


def _swa_sample_kernel(q_ref, kn_ref, kc_ref, vn_ref, vc_ref, bias_ref, o_ref):
    k_planes = _kv_head_planes(jnp.concatenate([kc_ref[0], kn_ref[...]], axis=0))
    v_planes = _kv_head_planes(jnp.concatenate([vc_ref[0], vn_ref[...]], axis=0))
    n_band = kc_ref.shape[1] + kn_ref.shape[0]
    _swa_chunks(q_ref, q_ref.shape[0], [0], k_planes, v_planes, n_band, bias_ref, lambda c: None, o_ref)


def _alibi_slopes():
    return (2.0 ** (-8.0 * np.arange(1, A_HEADS + 1) / A_HEADS)).astype(np.float32)


def _swa_bias(q_pos, k_pos, valid, sinks):
    n_q, n_k = len(q_pos), len(k_pos)
    assert n_k < SWA_KEYS
    dist = np.abs(q_pos[:, None] - k_pos[None, :]).astype(np.float32)
    band = _alibi_slopes()[:, None, None] * dist[None]
    if valid is not None:
        band = np.where(valid[None], band, np.inf)
    pad = np.full((A_HEADS, n_q, SWA_KEYS - n_k - 1), np.inf, np.float32)
    sink = jnp.broadcast_to(-sinks.astype(F32)[:, None, None], (A_HEADS, n_q, 1))
    bias = jnp.concatenate([jnp.asarray(band.astype(np.float32)), sink, jnp.asarray(pad)], axis=2)
    bias = bias.reshape(A_KV_HEADS, 2, 2, n_q, SWA_KEYS).transpose(0, 2, 1, 3, 4)
    return bias.reshape(A_KV_HEADS, 2, 2 * n_q, SWA_KEYS)


def _swa_prompt(q, k, v, sinks, *, batch, seq, n_chunks=4):
    tq = n_chunks * CHUNK
    pad = WIN_CHUNKS * CHUNK
    nb = seq // tq
    bias = _swa_bias(pad + np.arange(CHUNK), np.arange(pad + CHUNK), None, sinks)
    prev_per_blk = tq // pad

    def cur(b, i):
        return (b * nb + i, 0)

    def prev(b, i):
        return (jnp.maximum((b * nb + i) * prev_per_blk - 1, b * nb * prev_per_blk), 0)

    return pl.pallas_call(
        functools.partial(_swa_prompt_kernel, n_chunks=n_chunks),
        grid=(batch, nb),
        in_specs=[
            pl.BlockSpec((tq, A_WIDTH), cur),
            pl.BlockSpec((tq, A_KV_WIDTH), cur),
            pl.BlockSpec((pad, A_KV_WIDTH), prev),
            pl.BlockSpec((tq, A_KV_WIDTH), cur),
            pl.BlockSpec((pad, A_KV_WIDTH), prev),
            pl.BlockSpec(bias.shape, lambda b, i: (0, 0, 0, 0)),
        ],
        out_specs=pl.BlockSpec((tq, A_WIDTH), cur),
        out_shape=jax.ShapeDtypeStruct((batch * seq, A_WIDTH), BF16),
        compiler_params=_cparams(("parallel", "arbitrary")),
        name="swa_prompt",
    )(q, k, k, v, v, bias)


def _swa_sample(q, k, v, k_cache, v_cache, sinks, *, row0, batch, seq):
    lc = k_cache.shape[1]
    q_pos = PAST_LEN + np.arange(seq)
    k_pos = PAST_LEN - lc + np.arange(lc + seq)
    cdiff = q_pos[:, None] // CHUNK - k_pos[None, :] // CHUNK
    valid = (cdiff >= 0) & (cdiff <= WIN_CHUNKS)
    bias = _swa_bias(q_pos, k_pos, valid, sinks)
    blk0 = row0 // seq

    def rows(b):
        return (blk0 + b, 0)

    return pl.pallas_call(
        _swa_sample_kernel,
        grid=(batch,),
        in_specs=[
            pl.BlockSpec((seq, A_WIDTH), rows),
            pl.BlockSpec((seq, A_KV_WIDTH), rows),
            pl.BlockSpec((1, lc, A_KV_WIDTH), lambda b: (b, 0, 0)),
            pl.BlockSpec((seq, A_KV_WIDTH), rows),
            pl.BlockSpec((1, lc, A_KV_WIDTH), lambda b: (b, 0, 0)),
            pl.BlockSpec(bias.shape, lambda b: (0, 0, 0, 0)),
        ],
        out_specs=pl.BlockSpec((seq, A_WIDTH), lambda b: (b, 0)),
        out_shape=jax.ShapeDtypeStruct((batch * seq, A_WIDTH), BF16),
        compiler_params=_cparams(("parallel",)),
        name="swa_sample",
    )(q, k, k_cache, v, v_cache, bias)


def _split3(x):
    hi = x.astype(BF16)
    r1 = x - hi.astype(F32)
    mid = r1.astype(BF16)
    lo = (r1 - mid.astype(F32)).astype(BF16)
    return hi, mid, lo


def _dot3(mat, parts):
    acc = jnp.dot(mat, parts[0], preferred_element_type=F32)
    acc = acc + jnp.dot(mat, parts[1], preferred_element_type=F32)
    return acc + jnp.dot(mat, parts[2], preferred_element_type=F32)


def _hgrn_kernel(zq_ref, zf_ref, zi_ref, zg_ref, lb_ref, ng_ref, tri_ref, tsel_ref, s0_ref, o_ref, sfin_ref, s_scr,
                 *, blk):
    j = pl.program_id(1)

    @pl.when(j == 0)
    def _():
        s_scr[...] = s0_ref[0]

    bq = zq_ref[...]
    fl = zf_ref[...]
    v = zi_ref[...]
    bg = zg_ref[...]
    log_lb = lb_ref[0:1, :]
    log1m_lb = lb_ref[1:2, :]
    one_m_lb = lb_ref[2:3, :]

    q = bq * _sigmoid(bq) * (B_KEY_DIM ** -0.5)
    log_sig = jnp.minimum(fl, 0.0) - jnp.log1p(jnp.exp(-jnp.abs(fl)))
    c = log1m_lb + log_sig
    logf = jnp.maximum(log_lb, c) + jnp.log1p(jnp.exp(-jnp.abs(log_lb - c)))
    k = one_m_lb * _sigmoid(-fl)

    parts = _split3(logf)
    b = _dot3(tri_ref[...], parts)
    rq = _dot3(tsel_ref[...], parts)
    qt = q * jnp.exp(b - rq)
    qb = (q * jnp.exp(b)).astype(BF16)
    b_last = b[blk - 1:blk, :]
    khat = (k * jnp.exp(b_last - b)).astype(BF16)
    e_last = jnp.exp(b_last)
    vb = v.astype(BF16)
    row = lax.broadcasted_iota(jnp.int32, (blk, 1), 0)
    n_sub = blk // SUB
    kts = []
    for i in range(n_sub):
        r_i = rq[i * SUB:i * SUB + 1, :]
        kts.append(jnp.where(row < (i + 1) * SUB, k * jnp.exp(r_i - b), 0.0).astype(BF16))
    qt = qt.astype(BF16)
    tril = lax.broadcasted_iota(jnp.int32, (blk, blk), 0) >= lax.broadcasted_iota(jnp.int32, (blk, blk), 1)

    heads = [slice(h * B_KEY_DIM, (h + 1) * B_KEY_DIM) for h in range(B_HEADS)]
    s_old = [s_scr[h] for h in range(B_HEADS)]
    a_raw, o_state, ds = [], [], []
    for h, hs in enumerate(heads):
        a_raw.append(jnp.concatenate(
            [lax.dot_general(qt[i * SUB:(i + 1) * SUB, hs], kts[i][:, hs], (((1,), (1,)), ((), ())),
                             preferred_element_type=F32) for i in range(n_sub)], axis=0))
        o_state.append(jnp.dot(qb[:, hs], s_old[h].astype(BF16), preferred_element_type=F32))
        ds.append(lax.dot_general(khat[:, hs], vb[:, hs], (((0,), (0,)), ((), ())), preferred_element_type=F32))
    outs = []
    for h, hs in enumerate(heads):
        a = jnp.where(tril, a_raw[h], 0.0).astype(BF16)
        o = jnp.dot(a, vb[:, hs], preferred_element_type=F32) + o_state[h]
        decay = jnp.transpose(jnp.broadcast_to(e_last[:, hs], (B_KEY_DIM, B_KEY_DIM)))
        s_scr[h] = decay * s_old[h] + ds[h]
        o = o * lax.rsqrt(jnp.mean(o * o, axis=-1, keepdims=True) + RMS_EPS)
        outs.append(o)
    o_all = jnp.concatenate(outs, axis=1) * ng_ref[...] * (bg * _sigmoid(bg))
    o_ref[...] = o_all.astype(o_ref.dtype)

    @pl.when(j == pl.num_programs(1) - 1)
    def _():
        sfin_ref[0] = s_scr[...]


def _hgrn(zb, lbp, ng, s0, *, row0, batch, seq, blk):
    nb = seq // blk
    blk0 = row0 // blk
    t = np.arange(blk)
    tri = jnp.asarray((t[:, None] >= t[None, :]).astype(np.float32), dtype=BF16)
    tsel = jnp.asarray((t[None, :] < (t[:, None] // SUB) * SUB).astype(np.float32), dtype=BF16)

    def zspec(col):
        return pl.BlockSpec((blk, B_WIDTH), lambda b, j: (blk0 + b * nb + j, col))

    state_spec = pl.BlockSpec((1, B_HEADS, B_KEY_DIM, B_VAL_DIM), lambda b, j: (b, 0, 0, 0))
    return pl.pallas_call(
        functools.partial(_hgrn_kernel, blk=blk),
        grid=(batch, nb),
        in_specs=[
            zspec(0), zspec(1), zspec(2), zspec(3),
            pl.BlockSpec((3, B_WIDTH), lambda b, j: (0, 0)),
            pl.BlockSpec((1, B_WIDTH), lambda b, j: (0, 0)),
            pl.BlockSpec((blk, blk), lambda b, j: (0, 0)),
            pl.BlockSpec((blk, blk), lambda b, j: (0, 0)),
            state_spec,
        ],
        out_specs=[pl.BlockSpec((blk, B_WIDTH), lambda b, j: (b * nb + j, 0)), state_spec],
        out_shape=[jax.ShapeDtypeStruct((batch * seq, B_WIDTH), BF16),
                   jax.ShapeDtypeStruct((batch, B_HEADS, B_KEY_DIM, B_VAL_DIM), F32)],
        scratch_shapes=[pltpu.VMEM((B_HEADS, B_KEY_DIM, B_VAL_DIM), F32)],
        compiler_params=_cparams(("parallel", "arbitrary")),
        name=f"hgrn_blk{blk}",
    )(zb, zb, zb, zb, lbp, ng, tri, tsel, s0)


def _cattn_kernel(q_ref, mk_ref, mv_ref, o_ref):
    mk = mk_ref[0].astype(BF16)
    mv = mv_ref[0].astype(BF16)
    q = q_ref[...]
    for h in range(C_HEADS):
        hs = slice(h * C_HEAD_DIM, (h + 1) * C_HEAD_DIM)
        s = lax.dot_general(q[:, hs], mk[:, hs], (((1,), (1,)), ((), ())), preferred_element_type=F32)
        p = jnp.exp(s - jnp.max(s, axis=-1, keepdims=True))
        p = p / jnp.sum(p, axis=-1, keepdims=True)
        o_ref[:, hs] = jnp.dot(p.astype(BF16), mv[:, hs], preferred_element_type=F32).astype(o_ref.dtype)


def _cattn(q, mk, mv, *, row0, batch, seq, tq):
    nb = seq // tq
    blk0 = row0 // tq
    mem_spec = pl.BlockSpec((1, MEM_LEN, C_WIDTH), lambda b, i: (b, 0, 0))
    return pl.pallas_call(
        _cattn_kernel,
        grid=(batch, nb),
        in_specs=[pl.BlockSpec((tq, C_WIDTH), lambda b, i: (blk0 + b * nb + i, 0)), mem_spec, mem_spec],
        out_specs=pl.BlockSpec((tq, C_WIDTH), lambda b, i: (b * nb + i, 0)),
        out_shape=jax.ShapeDtypeStruct((batch * seq, C_WIDTH), BF16),
        compiler_params=_cparams(("parallel", "parallel")),
        name=f"cattn_tq{tq}",
    )(q, mk, mv)


def _merge_kernel(a_ref, b_ref, c_ref, g0_ref, g1_ref, g2_ref, pa_ref, pb_ref, pc_ref, o_ref):
    da = jnp.dot(a_ref[...], pa_ref[...], preferred_element_type=F32)
    db = jnp.dot(b_ref[...], pb_ref[...], preferred_element_type=F32)
    dc = jnp.dot(c_ref[...], pc_ref[...], preferred_element_type=F32)
    h = _sigmoid(g0_ref[...]) * da + _sigmoid(g1_ref[...]) * db + _sigmoid(g2_ref[...]) * dc
    o_ref[...] = h.astype(o_ref.dtype)


def _merge(a, b, c, gl, p, *, tm, name):
    n, width = a.shape

    def rows(i):
        return (i, 0)

    return pl.pallas_call(
        _merge_kernel,
        grid=(n // tm,),
        in_specs=[
            pl.BlockSpec((tm, width), rows), pl.BlockSpec((tm, width), rows), pl.BlockSpec((tm, width), rows),
            pl.BlockSpec((tm, D_MODEL), lambda i: (i, 0)),
            pl.BlockSpec((tm, D_MODEL), lambda i: (i, 1)),
            pl.BlockSpec((tm, D_MODEL), lambda i: (i, 2)),
            _resident((width, D_MODEL), lambda i: (0, 0)),
            _resident((width, D_MODEL), lambda i: (1, 0)),
            _resident((width, D_MODEL), lambda i: (2, 0)),
        ],
        out_specs=pl.BlockSpec((tm, D_MODEL), rows),
        out_shape=jax.ShapeDtypeStruct((n, D_MODEL), BF16),
        compiler_params=_cparams(("parallel",)),
        name=name,
    )(a, b, c, gl, gl, gl, p, p, p)


def _first_index(hit_src, m, iota, size, axis):
    return jnp.min(jnp.where(hit_src == m, iota, size), axis=axis, keepdims=True)


def _route(x1, wr_ref, rb_ref, upper_ref, run_ref, idx_ref, wts_ref, pos_ref):
    tm = x1.shape[0]
    logits = lax.dot_general(wr_ref[...], x1, (((1,), (1,)), ((), ())), precision=lax.Precision.HIGHEST,
                             preferred_element_type=F32)
    scores = _sigmoid(logits)
    choice = scores + rb_ref[...]
    g3 = choice.reshape(N_GROUPS, GROUP_SIZE, tm)
    mem_iota = lax.broadcasted_iota(jnp.int32, g3.shape, 1)
    m1 = jnp.max(g3, axis=1, keepdims=True)
    first = _first_index(g3, m1, mem_iota, GROUP_SIZE, 1)
    m2 = jnp.max(jnp.where(mem_iota == first, -jnp.inf, g3), axis=1, keepdims=True)
    gscore = (m1 + m2).reshape(N_GROUPS, tm)
    g_iota = lax.broadcasted_iota(jnp.int32, gscore.shape, 0)
    gsel = jnp.zeros(gscore.shape, F32)
    cur = gscore
    for _ in range(TOPK_GROUPS):
        m = jnp.max(cur, axis=0, keepdims=True)
        hit = g_iota == _first_index(cur, m, g_iota, N_GROUPS, 0)
        gsel = jnp.where(hit, 1.0, gsel)
        cur = jnp.where(hit, -jnp.inf, cur)
    masked = jnp.where(gsel.reshape(N_GROUPS, 1, tm) > 0.5, g3, -jnp.inf).reshape(N_EXPERTS, tm)
    e_iota = lax.broadcasted_iota(jnp.int32, masked.shape, 0)
    sel = jnp.zeros(masked.shape, F32)
    cur = masked
    idxs, ws = [], []
    for _ in range(TOP_K):
        m = jnp.max(cur, axis=0, keepdims=True)
        ei = _first_index(cur, m, e_iota, N_EXPERTS, 0)
        hit = e_iota == ei
        idxs.append(ei)
        ws.append(jnp.sum(jnp.where(hit, scores, 0.0), axis=0, keepdims=True))
        sel = jnp.where(hit, 1.0, sel)
        cur = jnp.where(hit, -jnp.inf, cur)
    w = jnp.concatenate(ws, axis=0)
    w = w / jnp.sum(w, axis=0, keepdims=True) * ROUTED_SCALE
    excl = jnp.dot(sel.astype(BF16), upper_ref[...], preferred_element_type=F32)
    posfull = excl + run_ref[:, 0:1]
    run_ref[...] = run_ref[...] + jnp.sum(sel, axis=1, keepdims=True)
    pos = [jnp.sum(jnp.where(e_iota == ei, posfull, 0.0), axis=0, keepdims=True) for ei in idxs]
    idx_ref[0] = jnp.concatenate(idxs, axis=0)
    wts_ref[0] = w
    pos_ref[0] = jnp.concatenate(pos, axis=0).astype(jnp.int32)


def _pack_halves(x):
    c = x.shape[1] // 2
    lo = lax.bitcast_convert_type(x[:, :c].astype(BF16).astype(F32), jnp.uint32)
    hi = lax.bitcast_convert_type(x[:, c:].astype(BF16).astype(F32), jnp.uint32)
    return (hi & jnp.uint32(0xFFFF0000)) | (lo >> 16)


def _unpack_halves(w):
    lo = lax.bitcast_convert_type(w << 16, F32)
    hi = lax.bitcast_convert_type(w & jnp.uint32(0xFFFF0000), F32)
    return lo, hi


ROW_TILE = 8
PACKED_LANES = D_MODEL // 2 // ROW_TILE


def _store_row_tiles(ref, val):
    rows = val.shape[0]
    for s in range(ROW_TILE):
        ref[pl.ds(s, rows, stride=ROW_TILE), :] = val[:, s * PACKED_LANES:(s + 1) * PACKED_LANES]


def _load_row_tiles(ref, rows):
    return [ref[pl.ds(s, rows, stride=ROW_TILE), :] for s in range(ROW_TILE)]


def _out_ln_route_kernel(x_ref, h_ref, wo_ref, g_ref, b_ref, wr_ref, rb_ref, upper_ref, run0_ref,
                         x1_ref, x1p_ref, idx_ref, wts_ref, pos_ref, cnt_ref, run_ref):
    @pl.when(pl.program_id(0) == 0)
    def _():
        run_ref[...] = run0_ref[...].astype(F32)

    y = DN_ALPHA * x_ref[...] + jnp.dot(h_ref[...], wo_ref[...], preferred_element_type=F32)
    x1 = _layer_norm(y, g_ref[...], b_ref[...])
    x1_ref[...] = x1
    _store_row_tiles(x1p_ref, _pack_halves(x1))
    _route(x1, wr_ref, rb_ref, upper_ref, run_ref, idx_ref, wts_ref, pos_ref)
    cnt_ref[...] = run_ref[...].astype(jnp.int32)


def _out_ln_route(x, h, w_out, ln_g, ln_b, w_router_t, router_bias, run0, *, tm, name):
    n = h.shape[0]
    nt = n // tm
    t = np.arange(tm)
    upper = jnp.asarray((t[:, None] < t[None, :]).astype(np.float32), dtype=BF16)

    def rows(i):
        return (i, 0)

    def const(i):
        return (0, 0)

    small = pl.BlockSpec((1, TOP_K, tm), lambda i: (i, 0, 0))
    return pl.pallas_call(
        _out_ln_route_kernel,
        grid=(nt,),
        in_specs=[
            pl.BlockSpec((tm, D_MODEL), rows),
            pl.BlockSpec((tm, D_MODEL), rows),
            _resident((D_MODEL, D_MODEL), const),
            pl.BlockSpec((1, D_MODEL), const),
            pl.BlockSpec((1, D_MODEL), const),
            pl.BlockSpec((N_EXPERTS, D_MODEL), const),
            pl.BlockSpec((N_EXPERTS, 1), const),
            pl.BlockSpec((tm, tm), const),
            pl.BlockSpec((N_EXPERTS, 128), const),
        ],
        out_specs=[pl.BlockSpec((tm, D_MODEL), rows), pl.BlockSpec((tm * ROW_TILE, PACKED_LANES), rows), small, small,
                   small, pl.BlockSpec((N_EXPERTS, 128), const)],
        out_shape=[
            jax.ShapeDtypeStruct((n, D_MODEL), F32),
            jax.ShapeDtypeStruct((n * ROW_TILE, PACKED_LANES), jnp.uint32),
            jax.ShapeDtypeStruct((nt, TOP_K, tm), jnp.int32),
            jax.ShapeDtypeStruct((nt, TOP_K, tm), F32),
            jax.ShapeDtypeStruct((nt, TOP_K, tm), jnp.int32),
            jax.ShapeDtypeStruct((N_EXPERTS, 128), jnp.int32),
        ],
        scratch_shapes=[pltpu.VMEM((N_EXPERTS, 128), F32)],
        compiler_params=_cparams(("arbitrary",)),
        name=name,
    )(x, h, w_out, ln_g, ln_b, w_router_t, router_bias, upper, run0)


def _row_copy(src_hbm, row, dst, dst_row, sem):
    return pltpu.make_async_copy(src_hbm.at[pl.ds(row, 1)], dst.at[pl.ds(dst_row, 1)], sem)


def _tile_copy(src, src_row, dst, dst_row, sem):
    def tile(row):
        first = row * ROW_TILE
        return pl.ds(first if isinstance(row, int) else pl.multiple_of(first, ROW_TILE), ROW_TILE)

    return pltpu.make_async_copy(src.at[tile(src_row)], dst.at[tile(dst_row)], sem)


def _dispatch_kernel(zrow_ref, dest_ref, xa_ref, xb_ref, o_hbm, zbuf, sem, zsem, *, block_rows, n_first):
    tm = xa_ref.shape[0] // ROW_TILE
    n_zero = zrow_ref.shape[0]

    def zero_copy(b):
        start = pl.multiple_of(jnp.maximum(zrow_ref[b], 0) * ROW_TILE, ROW_TILE)
        return pltpu.make_async_copy(zbuf, o_hbm.at[pl.ds(start, block_rows * ROW_TILE)], zsem)

    def zero_unowned_rows():
        zbuf[...] = jnp.zeros_like(zbuf)

        def start(b, carry):
            @pl.when(zrow_ref[b] >= 0)
            def _():
                zero_copy(b).start()
            return carry

        def wait(b, carry):
            @pl.when(zrow_ref[b] >= 0)
            def _():
                zero_copy(b).wait()
            return carry

        lax.fori_loop(0, n_zero, start, 0)
        lax.fori_loop(0, n_zero, wait, 0)

    pl.when(pl.program_id(0) == 0)(zero_unowned_rows)

    def scatter_tile(x_ref):
        for t in range(tm):
            for k in range(TOP_K):
                _tile_copy(x_ref, t, o_hbm, dest_ref[0, k, t], sem).start(priority=k % 2)
        for k in range(TOP_K):
            pltpu.make_async_copy(x_ref, o_hbm.at[pl.ds(0, tm * ROW_TILE)], sem).wait()

    pl.when(pl.program_id(0) < n_first)(functools.partial(scatter_tile, xa_ref))
    pl.when(pl.program_id(0) >= n_first)(functools.partial(scatter_tile, xb_ref))


def _dispatch(zrows, dest_blk, x1p_a, x1p_b, *, n_rows, tm, block_rows):
    nf = x1p_a.shape[0] // ROW_TILE // tm
    ns = x1p_b.shape[0] // ROW_TILE // tm
    grid_spec = pltpu.PrefetchScalarGridSpec(
        num_scalar_prefetch=1,
        grid=(nf + ns,),
        in_specs=[
            pl.BlockSpec((1, TOP_K, tm), lambda i, z: (i, 0, 0), memory_space=pltpu.SMEM),
            pl.BlockSpec((tm * ROW_TILE, PACKED_LANES), lambda i, z: (jnp.minimum(i, nf - 1), 0)),
            pl.BlockSpec((tm * ROW_TILE, PACKED_LANES), lambda i, z: (jnp.clip(i - nf, 0, ns - 1), 0)),
        ],
        out_specs=pl.BlockSpec(memory_space=pl.ANY),
        scratch_shapes=[pltpu.VMEM((block_rows * ROW_TILE, PACKED_LANES), jnp.uint32),
                        pltpu.SemaphoreType.DMA(()), pltpu.SemaphoreType.DMA(())],
    )
    return pl.pallas_call(
        functools.partial(_dispatch_kernel, block_rows=block_rows, n_first=nf),
        grid_spec=grid_spec,
        out_shape=jax.ShapeDtypeStruct((n_rows * ROW_TILE, PACKED_LANES), jnp.uint32),
        compiler_params=_cparams(("arbitrary",)),
        name="dispatch",
    )(zrows, dest_blk, x1p_a, x1p_b)


def _moe_kernel(be_ref, nv_ref, x_ref, wg_ref, wu_ref, wd_ref, o_ref, wg_b, wu_b, wd_b):
    k = pl.program_id(0)
    m = x_ref.shape[0] // ROW_TILE
    live = k < nv_ref[0]
    new_expert = jnp.logical_or(k == 0, be_ref[k] != be_ref[jnp.maximum(k - 1, 0)])

    @pl.when(jnp.logical_and(live, new_expert))
    def _():
        wg_b[...] = wg_ref[0].astype(BF16)
        wu_b[...] = wu_ref[0].astype(BF16)
        wd_b[...] = wd_ref[0].astype(BF16)

    @pl.when(live)
    def _():
        lo, hi = _unpack_halves(jnp.concatenate(_load_row_tiles(x_ref, m), axis=1))
        x = jnp.concatenate([lo.astype(BF16), hi.astype(BF16)], axis=1)
        g = jnp.dot(x, wg_b[...], preferred_element_type=F32)
        u = jnp.dot(x, wu_b[...], preferred_element_type=F32)
        hmid = (g * _sigmoid(g) * u).astype(BF16)
        y = jnp.dot(hmid, wd_b[...], preferred_element_type=F32)
        _store_row_tiles(o_ref, _pack_halves(y))

    @pl.when(pl.program_id(0) >= nv_ref[0])
    def _():
        o_ref[...] = jnp.zeros_like(o_ref)


def _moe(xs, blk_e, n_valid, w_gate, w_up, w_down):
    n_blocks = blk_e.shape[0]
    m = MOE_BLOCK

    def rows(k, be, nv):
        return (jnp.minimum(k, nv[0] - 1), 0)

    def wspec(shape):
        return pl.BlockSpec(shape, lambda k, be, nv: (be[k], 0, 0))

    grid_spec = pltpu.PrefetchScalarGridSpec(
        num_scalar_prefetch=2,
        grid=(n_blocks,),
        in_specs=[
            pl.BlockSpec((m * ROW_TILE, PACKED_LANES), rows),
            wspec((1, D_MODEL, EXPERT_DIM)), wspec((1, D_MODEL, EXPERT_DIM)), wspec((1, EXPERT_DIM, D_MODEL)),
        ],
        out_specs=pl.BlockSpec((m * ROW_TILE, PACKED_LANES), lambda k, be, nv: (k, 0)),
        scratch_shapes=[pltpu.VMEM((D_MODEL, EXPERT_DIM), BF16), pltpu.VMEM((D_MODEL, EXPERT_DIM), BF16),
                        pltpu.VMEM((EXPERT_DIM, D_MODEL), BF16)],
    )
    return pl.pallas_call(
        _moe_kernel,
        grid_spec=grid_spec,
        out_shape=jax.ShapeDtypeStruct(xs.shape, jnp.uint32),
        compiler_params=_cparams(("arbitrary",)),
        name="moe_experts",
    )(blk_e, n_valid, xs, w_gate, w_up, w_down)


def _final_kernel(dest_ref, dest_next_ref, x1_ref, w_ref, yb_hbm, sg_ref, su_ref, sd_ref, g_ref, b_ref, o_ref,
                  ybuf, sem):
    i = pl.program_id(0)
    tm = x1_ref.shape[0]

    def start(d_ref, slot, t):
        for k in range(TOP_K):
            _tile_copy(yb_hbm, d_ref[0, k, t], ybuf.at[slot, k], t, sem.at[slot]).start(priority=k % 2)

    def wait_all(slot):
        for k in range(TOP_K):
            pltpu.make_async_copy(yb_hbm.at[pl.ds(0, tm * ROW_TILE)], ybuf.at[slot, k], sem.at[slot]).wait()

    @pl.when(i == 0)
    def _():
        def issue(t, carry):
            start(dest_ref, 0, t)
            return carry

        lax.fori_loop(0, tm, issue, 0, unroll=4)

    cur = i % 2
    nxt = 1 - cur
    wait_all(cur)
    x1 = x1_ref[...]
    xb = x1.astype(BF16)
    g = jnp.dot(xb, sg_ref[...], preferred_element_type=F32)
    u = jnp.dot(xb, su_ref[...], preferred_element_type=F32)
    w = w_ref[...]
    per = tm // TOP_K
    r_lo = [None] * ROW_TILE
    r_hi = [None] * ROW_TILE
    for k in range(TOP_K):
        for t in range(k * per, (k + 1) * per):
            start(dest_next_ref, nxt, t)
        wk = jnp.broadcast_to(w[:, k:k + 1], (tm, PACKED_LANES))
        for s, piece in enumerate(_load_row_tiles(ybuf.at[cur, k], tm)):
            lo, hi = _unpack_halves(piece)
            r_lo[s] = wk * lo if r_lo[s] is None else r_lo[s] + wk * lo
            r_hi[s] = wk * hi if r_hi[s] is None else r_hi[s] + wk * hi
    y = jnp.dot((g * _sigmoid(g) * u).astype(BF16), sd_ref[...], preferred_element_type=F32)
    routed = jnp.concatenate(r_lo + r_hi, axis=1)
    o_ref[...] = _layer_norm(DN_ALPHA * x1 + (routed + y), g_ref[...], b_ref[...])

    @pl.when(i == pl.num_programs(0) - 1)
    def _():
        wait_all(nxt)


def _final(dest, x1, wts_t, yb, w_sg, w_su, w_sd, ln_g, ln_b, *, tm, row0, rows, name):
    blk0 = row0 // tm
    n_tiles = rows // tm

    def tile(i):
        return (blk0 + i, 0)

    def const(i):
        return (0, 0)

    return pl.pallas_call(
        _final_kernel,
        grid=(n_tiles,),
        in_specs=[
            pl.BlockSpec((1, TOP_K, tm), lambda i: (blk0 + i, 0, 0), memory_space=pltpu.SMEM),
            pl.BlockSpec((1, TOP_K, tm), lambda i: (blk0 + jnp.minimum(i + 1, n_tiles - 1), 0, 0),
                         memory_space=pltpu.SMEM),
            pl.BlockSpec((tm, D_MODEL), tile),
            pl.BlockSpec((tm, TOP_K), tile),
            pl.BlockSpec(memory_space=pl.ANY),
            _resident((D_MODEL, EXPERT_DIM), const),
            _resident((D_MODEL, EXPERT_DIM), const),
            _resident((EXPERT_DIM, D_MODEL), const),
            pl.BlockSpec((1, D_MODEL), const),
            pl.BlockSpec((1, D_MODEL), const),
        ],
        out_specs=pl.BlockSpec((tm, D_MODEL), lambda i: (i, 0)),
        out_shape=jax.ShapeDtypeStruct((rows, D_MODEL), F32),
        scratch_shapes=[pltpu.VMEM((2, TOP_K, tm * ROW_TILE, PACKED_LANES), jnp.uint32),
                        pltpu.SemaphoreType.DMA((2,))],
        compiler_params=_cparams(("arbitrary",)),
        name=name,
    )(dest, dest, x1, wts_t, yb, w_sg, w_su, w_sd, ln_g, ln_b)


def kernel(x_prompt, x_sample, cache_win_k, cache_win_v, state_hgrn, cache_mem_k, cache_mem_v, mem_prompt,
           w_in, w_mem_kv, a_sinks, b_lb_logits, b_norm_g, w_branch, w_out, ln1_g, ln1_b,
           w_router, router_bias, w_exp_gate, w_exp_up, w_exp_down, w_sh_gate, w_sh_up, w_sh_down, ln2_g, ln2_b):
    assert w_in.shape[0] == DEPTH == 1
    batch, seq, _ = x_prompt.shape
    dbatch, dseq, _ = x_sample.shape
    n_p = batch * seq
    n_s = dbatch * dseq
    n = n_p + n_s
    l = 0

    xp2 = x_prompt.reshape(n_p, D_MODEL)
    xs2 = x_sample.reshape(n_s, D_MODEL)
    win = w_in[l]
    o_k, o_v, o_b = A_WIDTH, A_WIDTH + A_KV_WIDTH, A_WIDTH + 2 * A_KV_WIDTH
    o_c = o_b + 4 * B_WIDTH
    o_g = o_c + C_WIDTH
    groups = [
        ("aq", 0, o_k, 1024, BF16, A_HEAD_DIM ** -0.5),
        ("ak", o_k, o_v, A_KV_WIDTH, F32, 1.0),
        ("av", o_v, o_b, A_KV_WIDTH, F32, 1.0),
        ("hgrn", o_b, o_c, 1024, F32, 1.0),
        ("cq", o_c, o_g, 1024, BF16, C_HEAD_DIM ** -0.5),
        ("gate", o_g, win.shape[1], 1024, F32, 1.0),
    ]
    zp, zs = {}, {}
    for gname, c0, c1, tn, dt, scale in groups:
        wslice = win[:, c0:c1].astype(BF16)
        zp[gname] = _matmul(xp2, wslice, tm=1024, tn=tn, out_dtype=dt, scale=scale, name=f"proj_{gname}_p")
        zs[gname] = _matmul(xs2, wslice, tm=n_s, tn=tn, out_dtype=dt, scale=scale, name=f"proj_{gname}_s")

    a_p = _swa_prompt(zp["aq"], zp["ak"], zp["av"], a_sinks[l], batch=batch, seq=seq)
    lc = cache_win_k.shape[2]
    kc = cache_win_k[l].reshape(dbatch, lc, A_KV_WIDTH)
    vc = cache_win_v[l].reshape(dbatch, lc, A_KV_WIDTH)
    a_s = _swa_sample(zs["aq"], zs["ak"], zs["av"], kc, vc, a_sinks[l], row0=0, batch=dbatch, seq=dseq)

    lower = jnp.cumsum(jax.nn.softmax(b_lb_logits.astype(F32), axis=0), axis=0)[l]
    lbp = jnp.stack([jnp.log(lower), jnp.log1p(-lower), 1.0 - lower])
    ng = jnp.tile(b_norm_g[l].astype(F32), B_HEADS).reshape(1, B_WIDTH)
    s_zero = jnp.zeros((batch, B_HEADS, B_KEY_DIM, B_VAL_DIM), F32)
    b_p, hs_p = _hgrn(zp["hgrn"], lbp, ng, s_zero, row0=0, batch=batch, seq=seq, blk=CHUNK)
    b_s, hs_s = _hgrn(zs["hgrn"], lbp, ng, state_hgrn[l].astype(F32), row0=0, batch=dbatch, seq=dseq, blk=dseq)

    mem = mem_prompt.reshape(batch * MEM_LEN, D_MODEL)
    wmem = w_mem_kv[l]
    mk = _matmul(mem, wmem[:, :C_WIDTH].astype(BF16), tm=batch * MEM_LEN, tn=512, out_dtype=F32, name="proj_mem_k")
    mv = _matmul(mem, wmem[:, C_WIDTH:].astype(BF16), tm=batch * MEM_LEN, tn=512, out_dtype=F32, name="proj_mem_v")
    mk = mk.reshape(batch, MEM_LEN, C_WIDTH)
    mv = mv.reshape(batch, MEM_LEN, C_WIDTH)
    c_p = _cattn(zp["cq"], mk, mv, row0=0, batch=batch, seq=seq, tq=512)
    c_s = _cattn(zs["cq"], cache_mem_k[l].reshape(dbatch, MEM_LEN, C_WIDTH),
                 cache_mem_v[l].reshape(dbatch, MEM_LEN, C_WIDTH), row0=0, batch=dbatch, seq=dseq, tq=dseq)

    w_br = w_branch[l].astype(BF16)
    route_w = (w_out[l].astype(BF16), ln1_g[l].reshape(1, D_MODEL), ln1_b[l].reshape(1, D_MODEL),
               w_router[l].T.astype(F32), router_bias[l].reshape(N_EXPERTS, 1).astype(F32))
    segs = []
    cnt = jnp.zeros((N_EXPERTS, 128), jnp.int32)
    for tag, x2, abc, gate, tm_r in (("p", xp2, (a_p, b_p, c_p), zp["gate"], 512), ("s", xs2, (a_s, b_s, c_s), zs["gate"], n_s)):
        h = _merge(*abc, gate, w_br, tm=tm_r, name=f"merge_{tag}")
        x1, x1p, idx, wts, pos, cnt = _out_ln_route(x2, h, *route_w, cnt, tm=tm_r, name=f"out_ln_route_{tag}")
        segs.append(dict(tag=tag, x1=x1, x1p=x1p, idx=idx, wts=wts, pos=pos, tm=tm_r, rows=x2.shape[0]))

    m = MOE_BLOCK
    n_pairs = n * TOP_K
    n_blocks = (n_pairs + m - 1) // m + N_EXPERTS
    counts = cnt[:, 0]
    padded = (counts + m - 1) // m * m
    pad_end = jnp.cumsum(padded)
    pad_start = (pad_end - padded).astype(jnp.int32)
    n_valid = pad_end[-1:] // m
    e_ids = jnp.arange(N_EXPERTS, dtype=jnp.int32)
    blk_first = jnp.arange(n_blocks, dtype=jnp.int32) * m
    blk_e = jnp.sum((blk_first[:, None] >= pad_end[None, :]).astype(jnp.int32), axis=1)
    blk_e = jnp.minimum(blk_e, N_EXPERTS - 1)
    z_pad = jnp.where(padded > counts, pad_end - m, -1)
    z_tail = jnp.where(blk_first >= pad_end[-1], blk_first, -1)
    zrows = jnp.concatenate([z_pad, z_tail]).astype(jnp.int32)

    def retile(dest, rows, tm_from, tm_to):
        d = dest.reshape(rows // tm_from, TOP_K, tm_from // tm_to, tm_to)
        return jnp.transpose(d, (0, 2, 1, 3)).reshape(rows // tm_to, TOP_K, tm_to)

    tm_d = 256
    for seg in segs:
        first_row = jnp.sum(jnp.where(seg["idx"][..., None] == e_ids, pad_start, 0), axis=-1)
        seg["dest"] = first_row + seg["pos"]
    dest_d = jnp.concatenate([retile(seg["dest"], seg["rows"], seg["tm"], tm_d) for seg in segs], axis=0)
    xs = _dispatch(zrows, dest_d, segs[0]["x1p"], segs[1]["x1p"], n_rows=n_blocks * m, tm=tm_d, block_rows=m)
    yb = _moe(xs, blk_e, n_valid.astype(jnp.int32), w_exp_gate[l].astype(F32), w_exp_up[l].astype(F32),
              w_exp_down[l].astype(F32))
    tm_f = 128
    shared_w = (w_sh_gate[l].astype(BF16), w_sh_up[l].astype(BF16), w_sh_down[l].astype(BF16),
                ln2_g[l].reshape(1, D_MODEL), ln2_b[l].reshape(1, D_MODEL))
    ys = []
    for seg in segs:
        rows = seg["rows"]
        dest_blk = retile(seg["dest"], rows, seg["tm"], tm_f)
        wts_t = jnp.transpose(seg["wts"], (0, 2, 1)).reshape(rows, TOP_K)
        ys.append(_final(dest_blk, seg["x1"], wts_t, yb, *shared_w, tm=tm_f, row0=0, rows=rows,
                         name=f"combine_shared_ln2_{seg['tag']}"))
    y_p = ys[0].reshape(batch, seq, D_MODEL)
    y_s = ys[1].reshape(dbatch, dseq, D_MODEL)

    k_p = zp["ak"].reshape(batch, seq, A_KV_HEADS, A_HEAD_DIM)[:, -lc:]
    v_p = zp["av"].reshape(batch, seq, A_KV_HEADS, A_HEAD_DIM)[:, -lc:]
    k_s = zs["ak"].reshape(dbatch, dseq, A_KV_HEADS, A_HEAD_DIM)
    v_s = zs["av"].reshape(dbatch, dseq, A_KV_HEADS, A_HEAD_DIM)
    wk_s = jnp.concatenate([cache_win_k[l].astype(F32), k_s], axis=1)[:, -lc:]
    wv_s = jnp.concatenate([cache_win_v[l].astype(F32), v_s], axis=1)[:, -lc:]
    mk_o = mk.reshape(batch, MEM_LEN, C_HEADS, C_HEAD_DIM)
    mv_o = mv.reshape(batch, MEM_LEN, C_HEADS, C_HEAD_DIM)
    return (y_p, y_s, k_p[None], v_p[None], hs_p[None], mk_o[None], mv_o[None], wk_s[None], wv_s[None], hs_s[None])
```

```python
import functools

import jax
import jax.numpy as jnp
import numpy as np
from jax import lax
from jax.experimental import pallas as pl
from jax.experimental.pallas import tpu as pltpu

F32 = jnp.float32
BF16 = jnp.bfloat16

D_MODEL = 2048
DEPTH = 1
PAST_LEN = 2048
CHUNK = 64
A_HEADS = 16
A_KV_HEADS = 4
A_GROUP = A_HEADS // A_KV_HEADS
A_HEAD_DIM = 64
A_WIDTH = A_HEADS * A_HEAD_DIM
A_KV_WIDTH = A_KV_HEADS * A_HEAD_DIM
WINDOW = 128
WIN_CHUNKS = WINDOW // CHUNK
B_HEADS = 8
B_KEY_DIM = 128
B_VAL_DIM = 128
B_WIDTH = B_HEADS * B_VAL_DIM
SUB = 16
MEM_LEN = 256
C_HEADS = 4
C_HEAD_DIM = 256
C_WIDTH = C_HEADS * C_HEAD_DIM
N_EXPERTS = 64
N_GROUPS = 8
GROUP_SIZE = N_EXPERTS // N_GROUPS
TOPK_GROUPS = 4
TOP_K = 8
EXPERT_DIM = 512
ROUTED_SCALE = 2.5
DN_ALPHA = (2 * DEPTH) ** 0.25
LN_EPS = 1e-5
RMS_EPS = 1e-6

MOE_BLOCK = 512
VMEM_LIMIT = 56 * 1024 * 1024


def _cparams(sem):
    return pltpu.CompilerParams(dimension_semantics=sem, vmem_limit_bytes=VMEM_LIMIT)


def _resident(shape, index_map):
    return pl.BlockSpec(shape, index_map, pipeline_mode=pl.Buffered(1))


def _sigmoid(x):
    return 1.0 / (1.0 + jnp.exp(-x))


def _layer_norm(x, g, b):
    mu = jnp.mean(x, axis=-1, keepdims=True)
    xc = x - mu
    var = jnp.mean(xc * xc, axis=-1, keepdims=True)
    return xc * lax.rsqrt(var + LN_EPS) * g + b


def _mm_kernel(x_ref, w_ref, o_ref, *, scale):
    acc = jnp.dot(x_ref[...].astype(BF16), w_ref[...], preferred_element_type=F32)
    if scale != 1.0:
        acc = acc * scale
    o_ref[...] = acc.astype(o_ref.dtype)


def _matmul(x, w, *, tm, tn, out_dtype, scale=1.0, name):
    m, k = x.shape
    n = w.shape[1]
    return pl.pallas_call(
        functools.partial(_mm_kernel, scale=scale),
        grid=(m // tm, n // tn),
        in_specs=[pl.BlockSpec((tm, k), lambda i, j: (i, 0)), pl.BlockSpec((k, tn), lambda i, j: (0, j))],
        out_specs=pl.BlockSpec((tm, tn), lambda i, j: (i, j)),
        out_shape=jax.ShapeDtypeStruct((m, n), out_dtype),
        compiler_params=_cparams(("parallel", "parallel")),
        name=name,
    )(x, w)


SWA_KEYS = 256
HEAD_PAIR = 2 * A_HEAD_DIM


def _kv_head_planes(x):
    lane = lax.broadcasted_iota(jnp.int32, (1, HEAD_PAIR), 1)
    low = lane < A_HEAD_DIM
    planes = []
    for pair in range(A_KV_HEADS // 2):
        own = x[:, pair * HEAD_PAIR:(pair + 1) * HEAD_PAIR]
        swapped = pltpu.roll(own, A_HEAD_DIM, 1)
        planes.append((jnp.where(low, own, 0.0).astype(BF16), jnp.where(low, 0.0, swapped).astype(BF16)))
        planes.append((jnp.where(low, swapped, 0.0).astype(BF16), jnp.where(low, 0.0, own).astype(BF16)))
    return planes


SWA_LOOKAHEAD = 8


def _swa_chunks(q_ref, rows, chunk_rows, k_planes, v_planes, n_band, bias_ref, valid_of, o_ref):
    zeros_k = jnp.zeros((SWA_KEYS - n_band, HEAD_PAIR), BF16)
    ones_v = jnp.ones((SWA_KEYS, HEAD_PAIR), BF16)
    tiles = [(c, h, x) for c in range(len(chunk_rows)) for h in range(A_KV_HEADS) for x in range(2)]

    def band(plane, c):
        return jnp.concatenate([plane[chunk_rows[c]:chunk_rows[c] + n_band], zeros_k], axis=0)

    def scores(t):
        c, h, x = tiles[t]
        q2 = jnp.concatenate([q_ref[c * rows:(c + 1) * rows, (2 * h + j) * HEAD_PAIR:(2 * h + j + 1) * HEAD_PAIR]
                              for j in range(2)], axis=0).astype(BF16)
        s = lax.dot_general(q2, band(k_planes[h][x], c), (((1,), (1,)), ((), ())), preferred_element_type=F32)
        s = s - bias_ref[h, x]
        valid = valid_of(c)
        return s if valid is None else jnp.where(valid, s, -jnp.inf)

    pending = {t: scores(t) for t in range(min(SWA_LOOKAHEAD, len(tiles)))}
    even = None
    for t, (c, h, x) in enumerate(tiles):
        s = pending.pop(t)
        p = jnp.exp(s - jnp.max(s, axis=-1, keepdims=True)).astype(BF16)
        if t + SWA_LOOKAHEAD < len(tiles):
            pending[t + SWA_LOOKAHEAD] = scores(t + SWA_LOOKAHEAD)
        vw = jnp.concatenate([band(v_planes[h][x], c), ones_v], axis=1)
        o = jnp.dot(p, vw, preferred_element_type=F32)
        o = o[:, :HEAD_PAIR] / o[:, HEAD_PAIR:]
        if x == 0:
            even = o
        else:
            out = even + o
            for j in range(2):
                o_ref[c * rows:(c + 1) * rows, (2 * h + j) * HEAD_PAIR:(2 * h + j + 1) * HEAD_PAIR] = (
                    out[j * rows:(j + 1) * rows].astype(o_ref.dtype))


def _swa_prompt_kernel(q_ref, kc_ref, kp_ref, vc_ref, vp_ref, bias_ref, o_ref, *, n_chunks):
    i = pl.program_id(1)
    pad = WIN_CHUNKS * CHUNK
    n_band = pad + CHUNK
    k_planes = _kv_head_planes(jnp.concatenate([kp_ref[...], kc_ref[...]], axis=0))
    v_planes = _kv_head_planes(jnp.concatenate([vp_ref[...], vc_ref[...]], axis=0))
    s_idx = lax.broadcasted_iota(jnp.int32, (1, SWA_KEYS), 1)

    def valid_of(c):
        return jnp.logical_or(s_idx + (i * (n_chunks * CHUNK) + c * CHUNK - pad) >= 0, s_idx >= n_band)

    _swa_chunks(q_ref, CHUNK, [c * CHUNK for c in range(n_chunks)], k_planes, v_planes, n_band, bias_ref, valid_of,
                o_ref)


def _swa_sample_kernel(q_ref, kn_ref, kc_ref, vn_ref, vc_ref, bias_ref, o_ref):
    k_planes = _kv_head_planes(jnp.concatenate([kc_ref[0], kn_ref[...]], axis=0))
    v_planes = _kv_head_planes(jnp.concatenate([vc_ref[0], vn_ref[...]], axis=0))
    n_band = kc_ref.shape[1] + kn_ref.shape[0]
    _swa_chunks(q_ref, q_ref.shape[0], [0], k_planes, v_planes, n_band, bias_ref, lambda c: None, o_ref)


def _alibi_slopes():
    return (2.0 ** (-8.0 * np.arange(1, A_HEADS + 1) / A_HEADS)).astype(np.float32)


def _swa_bias(q_pos, k_pos, valid, sinks):
    n_q, n_k = len(q_pos), len(k_pos)
    assert n_k < SWA_KEYS
    dist = np.abs(q_pos[:, None] - k_pos[None, :]).astype(np.float32)
    band = _alibi_slopes()[:, None, None] * dist[None]
    if valid is not None:
        band = np.where(valid[None], band, np.inf)
    pad = np.full((A_HEADS, n_q, SWA_KEYS - n_k - 1), np.inf, np.float32)
    sink = jnp.broadcast_to(-sinks.astype(F32)[:, None, None], (A_HEADS, n_q, 1))
    bias = jnp.concatenate([jnp.asarray(band.astype(np.float32)), sink, jnp.asarray(pad)], axis=2)
    bias = bias.reshape(A_KV_HEADS, 2, 2, n_q, SWA_KEYS).transpose(0, 2, 1, 3, 4)
    return bias.reshape(A_KV_HEADS, 2, 2 * n_q, SWA_KEYS)


def _swa_prompt(q, kv, sinks, *, batch, seq, n_chunks=4):
    tq = n_chunks * CHUNK
    pad = WIN_CHUNKS * CHUNK
    nb = seq // tq
    bias = _swa_bias(pad + np.arange(CHUNK), np.arange(pad + CHUNK), None, sinks)
    prev_per_blk = tq // pad

    def cur(b, i, col=0):
        return (b * nb + i, col)

    def prev(b, i, col=0):
        return (jnp.maximum((b * nb + i) * prev_per_blk - 1, b * nb * prev_per_blk), col)

    return pl.pallas_call(
        functools.partial(_swa_prompt_kernel, n_chunks=n_chunks),
        grid=(batch, nb),
        in_specs=[
            pl.BlockSpec((tq, A_WIDTH), cur),
            pl.BlockSpec((tq, A_KV_WIDTH), cur),
            pl.BlockSpec((pad, A_KV_WIDTH), prev),
            pl.BlockSpec((tq, A_KV_WIDTH), functools.partial(cur, col=1)),
            pl.BlockSpec((pad, A_KV_WIDTH), functools.partial(prev, col=1)),
            pl.BlockSpec(bias.shape, lambda b, i: (0, 0, 0, 0)),
        ],
        out_specs=pl.BlockSpec((tq, A_WIDTH), cur),
        out_shape=jax.ShapeDtypeStruct((batch * seq, A_WIDTH), BF16),
        compiler_params=_cparams(("parallel", "arbitrary")),
        name="swa_prompt",
    )(q, kv, kv, kv, kv, bias)


def _swa_sample(q, kv, k_cache, v_cache, sinks, *, row0, batch, seq):
    lc = k_cache.shape[1]
    q_pos = PAST_LEN + np.arange(seq)
    k_pos = PAST_LEN - lc + np.arange(lc + seq)
    cdiff = q_pos[:, None] // CHUNK - k_pos[None, :] // CHUNK
    valid = (cdiff >= 0) & (cdiff <= WIN_CHUNKS)
    bias = _swa_bias(q_pos, k_pos, valid, sinks)
    blk0 = row0 // seq

    def rows(b):
        return (blk0 + b, 0)

    return pl.pallas_call(
        _swa_sample_kernel,
        grid=(batch,),
        in_specs=[
            pl.BlockSpec((seq, A_WIDTH), rows),
            pl.BlockSpec((seq, A_KV_WIDTH), rows),
            pl.BlockSpec((1, lc, A_KV_WIDTH), lambda b: (b, 0, 0)),
            pl.BlockSpec((seq, A_KV_WIDTH), lambda b: (blk0 + b, 1)),
            pl.BlockSpec((1, lc, A_KV_WIDTH), lambda b: (b, 0, 0)),
            pl.BlockSpec(bias.shape, lambda b: (0, 0, 0, 0)),
        ],
        out_specs=pl.BlockSpec((seq, A_WIDTH), lambda b: (b, 0)),
        out_shape=jax.ShapeDtypeStruct((batch * seq, A_WIDTH), BF16),
        compiler_params=_cparams(("parallel",)),
        name="swa_sample",
    )(q, kv, k_cache, kv, v_cache, bias)


def _split3(x):
    hi = x.astype(BF16)
    r1 = x - hi.astype(F32)
    mid = r1.astype(BF16)
    lo = (r1 - mid.astype(F32)).astype(BF16)
    return hi, mid, lo


def _dot3(mat, parts):
    acc = jnp.dot(mat, parts[0], preferred_element_type=F32)
    acc = acc + jnp.dot(mat, parts[1], preferred_element_type=F32)
    return acc + jnp.dot(mat, parts[2], preferred_element_type=F32)


def _hgrn_kernel(zq_ref, zf_ref, zi_ref, zg_ref, lb_ref, ng_ref, tri_ref, tsel_ref, s0_ref, o_ref, sfin_ref, s_scr,
                 *, blk):
    j = pl.program_id(1)

    @pl.when(j == 0)
    def _():
        s_scr[...] = s0_ref[0]

    bq = zq_ref[...]
    fl = zf_ref[...]
    v = zi_ref[...]
    bg = zg_ref[...]
    log_lb = lb_ref[0:1, :]
    log1m_lb = lb_ref[1:2, :]
    one_m_lb = lb_ref[2:3, :]

    q = bq * _sigmoid(bq) * (B_KEY_DIM ** -0.5)
    log_sig = jnp.minimum(fl, 0.0) - jnp.log1p(jnp.exp(-jnp.abs(fl)))
    c = log1m_lb + log_sig
    logf = jnp.maximum(log_lb, c) + jnp.log1p(jnp.exp(-jnp.abs(log_lb - c)))
    k = one_m_lb * _sigmoid(-fl)

    parts = _split3(logf)
    b = _dot3(tri_ref[...], parts)
    rq = _dot3(tsel_ref[...], parts)
    qt = q * jnp.exp(b - rq)
    qb = (q * jnp.exp(b)).astype(BF16)
    b_last = b[blk - 1:blk, :]
    khat = (k * jnp.exp(b_last - b)).astype(BF16)
    e_last = jnp.exp(b_last)
    vb = v.astype(BF16)
    row = lax.broadcasted_iota(jnp.int32, (blk, 1), 0)
    n_sub = blk // SUB
    kts = []
    for i in range(n_sub):
        r_i = rq[i * SUB:i * SUB + 1, :]
        kts.append(jnp.where(row < (i + 1) * SUB, k * jnp.exp(r_i - b), 0.0).astype(BF16))
    qt = qt.astype(BF16)
    tril = lax.broadcasted_iota(jnp.int32, (blk, blk), 0) >= lax.broadcasted_iota(jnp.int32, (blk, blk), 1)

    heads = [slice(h * B_KEY_DIM, (h + 1) * B_KEY_DIM) for h in range(B_HEADS)]
    s_old = [s_scr[h] for h in range(B_HEADS)]
    a_raw, o_state, ds = [], [], []
    for h, hs in enumerate(heads):
        a_raw.append(jnp.concatenate(
            [lax.dot_general(qt[i * SUB:(i + 1) * SUB, hs], kts[i][:, hs], (((1,), (1,)), ((), ())),
                             preferred_element_type=F32) for i in range(n_sub)], axis=0))
        o_state.append(jnp.dot(qb[:, hs], s_old[h].astype(BF16), preferred_element_type=F32))
        ds.append(lax.dot_general(khat[:, hs], vb[:, hs], (((0,), (0,)), ((), ())), preferred_element_type=F32))
    outs = []
    for h, hs in enumerate(heads):
        a = jnp.where(tril, a_raw[h], 0.0).astype(BF16)
        o = jnp.dot(a, vb[:, hs], preferred_element_type=F32) + o_state[h]
        decay = jnp.transpose(jnp.broadcast_to(e_last[:, hs], (B_KEY_DIM, B_KEY_DIM)))
        s_scr[h] = decay * s_old[h] + ds[h]
        o = o * lax.rsqrt(jnp.mean(o * o, axis=-1, keepdims=True) + RMS_EPS)
        outs.append(o)
    o_all = jnp.concatenate(outs, axis=1) * ng_ref[...] * (bg * _sigmoid(bg))
    o_ref[...] = o_all.astype(o_ref.dtype)

    @pl.when(j == pl.num_programs(1) - 1)
    def _():
        sfin_ref[0] = s_scr[...]


def _hgrn(zb, lbp, ng, s0, *, row0, batch, seq, blk):
    nb = seq // blk
    blk0 = row0 // blk
    t = np.arange(blk)
    tri = jnp.asarray((t[:, None] >= t[None, :]).astype(np.float32), dtype=BF16)
    tsel = jnp.asarray((t[None, :] < (t[:, None] // SUB) * SUB).astype(np.float32), dtype=BF16)

    def zspec(col):
        return pl.BlockSpec((blk, B_WIDTH), lambda b, j: (blk0 + b * nb + j, col))

    state_spec = pl.BlockSpec((1, B_HEADS, B_KEY_DIM, B_VAL_DIM), lambda b, j: (b, 0, 0, 0))
    return pl.pallas_call(
        functools.partial(_hgrn_kernel, blk=blk),
        grid=(batch, nb),
        in_specs=[
            zspec(0), zspec(1), zspec(2), zspec(3),
            pl.BlockSpec((3, B_WIDTH), lambda b, j: (0, 0)),
            pl.BlockSpec((1, B_WIDTH), lambda b, j: (0, 0)),
            pl.BlockSpec((blk, blk), lambda b, j: (0, 0)),
            pl.BlockSpec((blk, blk), lambda b, j: (0, 0)),
            state_spec,
        ],
        out_specs=[pl.BlockSpec((blk, B_WIDTH), lambda b, j: (b * nb + j, 0)), state_spec],
        out_shape=[jax.ShapeDtypeStruct((batch * seq, B_WIDTH), BF16),
                   jax.ShapeDtypeStruct((batch, B_HEADS, B_KEY_DIM, B_VAL_DIM), F32)],
        scratch_shapes=[pltpu.VMEM((B_HEADS, B_KEY_DIM, B_VAL_DIM), F32)],
        compiler_params=_cparams(("parallel", "arbitrary")),
        name=f"hgrn_blk{blk}",
    )(zb, zb, zb, zb, lbp, ng, tri, tsel, s0)


def _cattn_kernel(q_ref, mk_ref, mv_ref, o_ref):
    mk = mk_ref[0].astype(BF16)
    mv = mv_ref[0].astype(BF16)
    q = q_ref[...]
    for h in range(C_HEADS):
        hs = slice(h * C_HEAD_DIM, (h + 1) * C_HEAD_DIM)
        s = lax.dot_general(q[:, hs], mk[:, hs], (((1,), (1,)), ((), ())), preferred_element_type=F32)
        p = jnp.exp(s - jnp.max(s, axis=-1, keepdims=True))
        p = p / jnp.sum(p, axis=-1, keepdims=True)
        o_ref[:, hs] = jnp.dot(p.astype(BF16), mv[:, hs], preferred_element_type=F32).astype(o_ref.dtype)


def _cattn(q, mk, mv, *, row0, batch, seq, tq):
    nb = seq // tq
    blk0 = row0 // tq
    mem_spec = pl.BlockSpec((1, MEM_LEN, C_WIDTH), lambda b, i: (b, 0, 0))
    return pl.pallas_call(
        _cattn_kernel,
        grid=(batch, nb),
        in_specs=[pl.BlockSpec((tq, C_WIDTH), lambda b, i: (blk0 + b * nb + i, 0)), mem_spec, mem_spec],
        out_specs=pl.BlockSpec((tq, C_WIDTH), lambda b, i: (b * nb + i, 0)),
        out_shape=jax.ShapeDtypeStruct((batch * seq, C_WIDTH), BF16),
        compiler_params=_cparams(("parallel", "parallel")),
        name=f"cattn_tq{tq}",
    )(q, mk, mv)


def _merge_kernel(a_ref, b_ref, c_ref, g0_ref, g1_ref, g2_ref, pa_ref, pb_ref, pc_ref, o_ref):
    da = jnp.dot(a_ref[...], pa_ref[...], preferred_element_type=F32)
    db = jnp.dot(b_ref[...], pb_ref[...], preferred_element_type=F32)
    dc = jnp.dot(c_ref[...], pc_ref[...], preferred_element_type=F32)
    h = _sigmoid(g0_ref[...]) * da + _sigmoid(g1_ref[...]) * db + _sigmoid(g2_ref[...]) * dc
    o_ref[...] = h.astype(o_ref.dtype)


def _merge(a, b, c, gl, p, *, tm, name):
    n, width = a.shape

    def rows(i):
        return (i, 0)

    return pl.pallas_call(
        _merge_kernel,
        grid=(n // tm,),
        in_specs=[
            pl.BlockSpec((tm, width), rows), pl.BlockSpec((tm, width), rows), pl.BlockSpec((tm, width), rows),
            pl.BlockSpec((tm, D_MODEL), lambda i: (i, 0)),
            pl.BlockSpec((tm, D_MODEL), lambda i: (i, 1)),
            pl.BlockSpec((tm, D_MODEL), lambda i: (i, 2)),
            _resident((width, D_MODEL), lambda i: (0, 0)),
            _resident((width, D_MODEL), lambda i: (1, 0)),
            _resident((width, D_MODEL), lambda i: (2, 0)),
        ],
        out_specs=pl.BlockSpec((tm, D_MODEL), rows),
        out_shape=jax.ShapeDtypeStruct((n, D_MODEL), BF16),
        compiler_params=_cparams(("parallel",)),
        name=name,
    )(a, b, c, gl, gl, gl, p, p, p)


def _first_index(hit_src, m, iota, size, axis):
    return jnp.min(jnp.where(hit_src == m, iota, size), axis=axis, keepdims=True)


def _route(x1, wr_ref, rb_ref, upper_ref, run_ref, idx_ref, wts_ref, pos_ref):
    tm = x1.shape[0]
    logits = lax.dot_general(wr_ref[...], x1.astype(BF16), (((1,), (1,)), ((), ())),
                             preferred_element_type=F32)
    scores = _sigmoid(logits)
    choice = scores + rb_ref[...]
    g3 = choice.reshape(N_GROUPS, GROUP_SIZE, tm)
    mem_iota = lax.broadcasted_iota(jnp.int32, g3.shape, 1)
    m1 = jnp.max(g3, axis=1, keepdims=True)
    first = _first_index(g3, m1, mem_iota, GROUP_SIZE, 1)
    m2 = jnp.max(jnp.where(mem_iota == first, -jnp.inf, g3), axis=1, keepdims=True)
    gscore = (m1 + m2).reshape(N_GROUPS, tm)
    g_iota = lax.broadcasted_iota(jnp.int32, gscore.shape, 0)
    gsel = jnp.zeros(gscore.shape, F32)
    cur = gscore
    for _ in range(TOPK_GROUPS):
        m = jnp.max(cur, axis=0, keepdims=True)
        hit = g_iota == _first_index(cur, m, g_iota, N_GROUPS, 0)
        gsel = jnp.where(hit, 1.0, gsel)
        cur = jnp.where(hit, -jnp.inf, cur)
    masked = jnp.where(gsel.reshape(N_GROUPS, 1, tm) > 0.5, g3, -jnp.inf).reshape(N_EXPERTS, tm)
    e_iota = lax.broadcasted_iota(jnp.int32, masked.shape, 0)
    sel = jnp.zeros(masked.shape, F32)
    cur = masked
    idxs, ws = [], []
    for _ in range(TOP_K):
        m = jnp.max(cur, axis=0, keepdims=True)
        ei = _first_index(cur, m, e_iota, N_EXPERTS, 0)
        hit = e_iota == ei
        idxs.append(ei)
        ws.append(jnp.sum(jnp.where(hit, scores, 0.0), axis=0, keepdims=True))
        sel = jnp.where(hit, 1.0, sel)
        cur = jnp.where(hit, -jnp.inf, cur)
    w = jnp.concatenate(ws, axis=0)
    w = w / jnp.sum(w, axis=0, keepdims=True) * ROUTED_SCALE
    excl = jnp.dot(sel.astype(BF16), upper_ref[...], preferred_element_type=F32)
    posfull = excl + run_ref[:, 0:1]
    run_ref[...] = run_ref[...] + jnp.sum(sel, axis=1, keepdims=True)
    pos = [jnp.sum(jnp.where(e_iota == ei, posfull, 0.0), axis=0, keepdims=True) for ei in idxs]
    idx_ref[0] = jnp.concatenate(idxs, axis=0)
    wts_ref[0] = w
    pos_ref[0] = jnp.concatenate(pos, axis=0).astype(jnp.int32)


def _pack_halves(x):
    c = x.shape[1] // 2
    lo = lax.bitcast_convert_type(x[:, :c].astype(BF16).astype(F32), jnp.uint32)
    hi = lax.bitcast_convert_type(x[:, c:].astype(BF16).astype(F32), jnp.uint32)
    return (hi & jnp.uint32(0xFFFF0000)) | (lo >> 16)


def _unpack_halves(w):
    lo = lax.bitcast_convert_type(w << 16, F32)
    hi = lax.bitcast_convert_type(w & jnp.uint32(0xFFFF0000), F32)
    return lo, hi


ROW_TILE = 8
PACKED_LANES = D_MODEL // 2 // ROW_TILE


def _store_row_tiles(ref, val):
    rows = val.shape[0]
    for s in range(ROW_TILE):
        ref[pl.ds(s, rows, stride=ROW_TILE), :] = val[:, s * PACKED_LANES:(s + 1) * PACKED_LANES]


def _load_row_tiles(ref, rows):
    return [ref[pl.ds(s, rows, stride=ROW_TILE), :] for s in range(ROW_TILE)]


def _out_ln_route_kernel(x_ref, h_ref, wo_ref, g_ref, b_ref, wr_ref, rb_ref, upper_ref, run0_ref,
                         x1_ref, x1p_ref, idx_ref, wts_ref, pos_ref, cnt_ref, run_ref):
    @pl.when(pl.program_id(0) == 0)
    def _():
        run_ref[...] = run0_ref[...].astype(F32)

    y = DN_ALPHA * x_ref[...] + jnp.dot(h_ref[...], wo_ref[...], preferred_element_type=F32)
    x1 = _layer_norm(y, g_ref[...], b_ref[...])
    x1_ref[...] = x1
    _store_row_tiles(x1p_ref, _pack_halves(x1))
    _route(x1, wr_ref, rb_ref, upper_ref, run_ref, idx_ref, wts_ref, pos_ref)
    cnt_ref[...] = run_ref[...].astype(jnp.int32)


def _out_ln_route(x, h, w_out, ln_g, ln_b, w_router_t, router_bias, run0, *, tm, name):
    n = h.shape[0]
    nt = n // tm
    t = np.arange(tm)
    upper = jnp.asarray((t[:, None] < t[None, :]).astype(np.float32), dtype=BF16)

    def rows(i):
        return (i, 0)

    def const(i):
        return (0, 0)

    small = pl.BlockSpec((1, TOP_K, tm), lambda i: (i, 0, 0))
    return pl.pallas_call(
        _out_ln_route_kernel,
        grid=(nt,),
        in_specs=[
            pl.BlockSpec((tm, D_MODEL), rows),
            pl.BlockSpec((tm, D_MODEL), rows),
            _resident((D_MODEL, D_MODEL), const),
            pl.BlockSpec((1, D_MODEL), const),
            pl.BlockSpec((1, D_MODEL), const),
            pl.BlockSpec((N_EXPERTS, D_MODEL), const),
            pl.BlockSpec((N_EXPERTS, 1), const),
            pl.BlockSpec((tm, tm), const),
            pl.BlockSpec((N_EXPERTS, 128), const),
        ],
        out_specs=[pl.BlockSpec((tm, D_MODEL), rows), pl.BlockSpec((tm * ROW_TILE, PACKED_LANES), rows), small, small,
                   small, pl.BlockSpec((N_EXPERTS, 128), const)],
        out_shape=[
            jax.ShapeDtypeStruct((n, D_MODEL), F32),
            jax.ShapeDtypeStruct((n * ROW_TILE, PACKED_LANES), jnp.uint32),
            jax.ShapeDtypeStruct((nt, TOP_K, tm), jnp.int32),
            jax.ShapeDtypeStruct((nt, TOP_K, tm), F32),
            jax.ShapeDtypeStruct((nt, TOP_K, tm), jnp.int32),
            jax.ShapeDtypeStruct((N_EXPERTS, 128), jnp.int32),
        ],
        scratch_shapes=[pltpu.VMEM((N_EXPERTS, 128), F32)],
        compiler_params=_cparams(("arbitrary",)),
        name=name,
    )(x, h, w_out, ln_g, ln_b, w_router_t, router_bias, upper, run0)


def _row_copy(src_hbm, row, dst, dst_row, sem):
    return pltpu.make_async_copy(src_hbm.at[pl.ds(row, 1)], dst.at[pl.ds(dst_row, 1)], sem)


def _tile_copy(src, src_row, dst, dst_row, sem):
    def tile(row):
        first = row * ROW_TILE
        return pl.ds(first if isinstance(row, int) else pl.multiple_of(first, ROW_TILE), ROW_TILE)

    return pltpu.make_async_copy(src.at[tile(src_row)], dst.at[tile(dst_row)], sem)


def _dispatch_kernel(zrow_ref, dest_ref, xa_ref, xb_ref, o_hbm, zbuf, sem, zsem, *, block_rows, n_first):
    tm = xa_ref.shape[0] // ROW_TILE
    n_zero = zrow_ref.shape[0]

    def zero_copy(b):
        start = pl.multiple_of(jnp.maximum(zrow_ref[b], 0) * ROW_TILE, ROW_TILE)
        return pltpu.make_async_copy(zbuf, o_hbm.at[pl.ds(start, block_rows * ROW_TILE)], zsem)

    def zero_unowned_rows():
        zbuf[...] = jnp.zeros_like(zbuf)

        def start(b, carry):
            @pl.when(zrow_ref[b] >= 0)
            def _():
                zero_copy(b).start()
            return carry

        def wait(b, carry):
            @pl.when(zrow_ref[b] >= 0)
            def _():
                zero_copy(b).wait()
            return carry

        lax.fori_loop(0, n_zero, start, 0)
        lax.fori_loop(0, n_zero, wait, 0)

    pl.when(pl.program_id(0) == 0)(zero_unowned_rows)

    def scatter_tile(x_ref):
        for t in range(tm):
            for k in range(TOP_K):
                _tile_copy(x_ref, t, o_hbm, dest_ref[0, k, t], sem).start(priority=k % 2)
        for k in range(TOP_K):
            pltpu.make_async_copy(x_ref, o_hbm.at[pl.ds(0, tm * ROW_TILE)], sem).wait()

    pl.when(pl.program_id(0) < n_first)(functools.partial(scatter_tile, xa_ref))
    pl.when(pl.program_id(0) >= n_first)(functools.partial(scatter_tile, xb_ref))


def _dispatch(zrows, dest_blk, x1p_a, x1p_b, *, n_rows, tm, block_rows):
    nf = x1p_a.shape[0] // ROW_TILE // tm
    ns = x1p_b.shape[0] // ROW_TILE // tm
    grid_spec = pltpu.PrefetchScalarGridSpec(
        num_scalar_prefetch=1,
        grid=(nf + ns,),
        in_specs=[
            pl.BlockSpec((1, TOP_K, tm), lambda i, z: (i, 0, 0), memory_space=pltpu.SMEM),
            pl.BlockSpec((tm * ROW_TILE, PACKED_LANES), lambda i, z: (jnp.minimum(i, nf - 1), 0)),
            pl.BlockSpec((tm * ROW_TILE, PACKED_LANES), lambda i, z: (jnp.clip(i - nf, 0, ns - 1), 0)),
        ],
        out_specs=pl.BlockSpec(memory_space=pl.ANY),
        scratch_shapes=[pltpu.VMEM((block_rows * ROW_TILE, PACKED_LANES), jnp.uint32),
                        pltpu.SemaphoreType.DMA(()), pltpu.SemaphoreType.DMA(())],
    )
    return pl.pallas_call(
        functools.partial(_dispatch_kernel, block_rows=block_rows, n_first=nf),
        grid_spec=grid_spec,
        out_shape=jax.ShapeDtypeStruct((n_rows * ROW_TILE, PACKED_LANES), jnp.uint32),
        compiler_params=_cparams(("arbitrary",)),
        name="dispatch",
    )(zrows, dest_blk, x1p_a, x1p_b)


def _moe_kernel(be_ref, nv_ref, x_ref, wg_ref, wu_ref, wd_ref, o_ref, wg_b, wu_b, wd_b):
    k = pl.program_id(0)
    m = x_ref.shape[0] // ROW_TILE
    live = k < nv_ref[0]
    new_expert = jnp.logical_or(k == 0, be_ref[k] != be_ref[jnp.maximum(k - 1, 0)])

    @pl.when(jnp.logical_and(live, new_expert))
    def _():
        wg_b[...] = wg_ref[0].astype(BF16)
        wu_b[...] = wu_ref[0].astype(BF16)
        wd_b[...] = wd_ref[0].astype(BF16)

    @pl.when(live)
    def _():
        lo, hi = _unpack_halves(jnp.concatenate(_load_row_tiles(x_ref, m), axis=1))
        x = jnp.concatenate([lo.astype(BF16), hi.astype(BF16)], axis=1)
        g = jnp.dot(x, wg_b[...], preferred_element_type=F32)
        u = jnp.dot(x, wu_b[...], preferred_element_type=F32)
        hmid = (g * _sigmoid(g) * u).astype(BF16)
        y = jnp.dot(hmid, wd_b[...], preferred_element_type=F32)
        _store_row_tiles(o_ref, _pack_halves(y))

    @pl.when(pl.program_id(0) >= nv_ref[0])
    def _():
        o_ref[...] = jnp.zeros_like(o_ref)


def _moe(xs, blk_e, n_valid, w_gate, w_up, w_down):
    n_blocks = blk_e.shape[0]
    m = MOE_BLOCK

    def rows(k, be, nv):
        return (jnp.minimum(k, nv[0] - 1), 0)

    def wspec(shape):
        return pl.BlockSpec(shape, lambda k, be, nv: (be[k], 0, 0))

    grid_spec = pltpu.PrefetchScalarGridSpec(
        num_scalar_prefetch=2,
        grid=(n_blocks,),
        in_specs=[
            pl.BlockSpec((m * ROW_TILE, PACKED_LANES), rows),
            wspec((1, D_MODEL, EXPERT_DIM)), wspec((1, D_MODEL, EXPERT_DIM)), wspec((1, EXPERT_DIM, D_MODEL)),
        ],
        out_specs=pl.BlockSpec((m * ROW_TILE, PACKED_LANES), lambda k, be, nv: (k, 0)),
        scratch_shapes=[pltpu.VMEM((D_MODEL, EXPERT_DIM), BF16), pltpu.VMEM((D_MODEL, EXPERT_DIM), BF16),
                        pltpu.VMEM((EXPERT_DIM, D_MODEL), BF16)],
    )
    return pl.pallas_call(
        _moe_kernel,
        grid_spec=grid_spec,
        out_shape=jax.ShapeDtypeStruct(xs.shape, jnp.uint32),
        compiler_params=_cparams(("arbitrary",)),
        name="moe_experts",
    )(blk_e, n_valid, xs, w_gate, w_up, w_down)


def _final_kernel(dest_ref, dest_next_ref, x1_ref, w_ref, yb_hbm, sg_ref, su_ref, sd_ref, g_ref, b_ref, o_ref,
                  ybuf, sem):
    i = pl.program_id(0)
    tm = x1_ref.shape[0]

    def start(d_ref, slot, t):
        for k in range(TOP_K):
            _tile_copy(yb_hbm, d_ref[0, k, t], ybuf.at[slot, k], t, sem.at[slot]).start(priority=k % 2)

    def wait_all(slot):
        for k in range(TOP_K):
            pltpu.make_async_copy(yb_hbm.at[pl.ds(0, tm * ROW_TILE)], ybuf.at[slot, k], sem.at[slot]).wait()

    @pl.when(i == 0)
    def _():
        def issue(t, carry):
            start(dest_ref, 0, t)
            return carry

        lax.fori_loop(0, tm, issue, 0, unroll=4)

    cur = i % 2
    nxt = 1 - cur
    wait_all(cur)
    x1 = x1_ref[...]
    xb = x1.astype(BF16)
    g = jnp.dot(xb, sg_ref[...], preferred_element_type=F32)
    u = jnp.dot(xb, su_ref[...], preferred_element_type=F32)
    w = w_ref[...]
    per = tm // TOP_K
    r_lo = [None] * ROW_TILE
    r_hi = [None] * ROW_TILE
    for k in range(TOP_K):
        for t in range(k * per, (k + 1) * per):
            start(dest_next_ref, nxt, t)
        wk = jnp.broadcast_to(w[:, k:k + 1], (tm, PACKED_LANES))
        for s, piece in enumerate(_load_row_tiles(ybuf.at[cur, k], tm)):
            lo, hi = _unpack_halves(piece)
            r_lo[s] = wk * lo if r_lo[s] is None else r_lo[s] + wk * lo
            r_hi[s] = wk * hi if r_hi[s] is None else r_hi[s] + wk * hi
    y = jnp.dot((g * _sigmoid(g) * u).astype(BF16), sd_ref[...], preferred_element_type=F32)
    routed = jnp.concatenate(r_lo + r_hi, axis=1)
    o_ref[...] = _layer_norm(DN_ALPHA * x1 + (routed + y), g_ref[...], b_ref[...])

    @pl.when(i == pl.num_programs(0) - 1)
    def _():
        wait_all(nxt)


def _final(dest, x1, wts_t, yb, w_sg, w_su, w_sd, ln_g, ln_b, *, tm, name):
    n_tiles = x1.shape[0] // tm

    def tile(i):
        return (i, 0)

    def const(i):
        return (0, 0)

    smem = functools.partial(pl.BlockSpec, memory_space=pltpu.SMEM)
    return pl.pallas_call(
        _final_kernel,
        grid=(n_tiles,),
        in_specs=[
            smem((1, TOP_K, tm), lambda i: (i, 0, 0)),
            smem((1, TOP_K, tm), lambda i: (jnp.minimum(i + 1, n_tiles - 1), 0, 0)),
            pl.BlockSpec((tm, D_MODEL), tile),
            pl.BlockSpec((tm, TOP_K), tile),
            pl.BlockSpec(memory_space=pl.ANY),
            _resident((D_MODEL, EXPERT_DIM), const),
            _resident((D_MODEL, EXPERT_DIM), const),
            _resident((EXPERT_DIM, D_MODEL), const),
            pl.BlockSpec((1, D_MODEL), const),
            pl.BlockSpec((1, D_MODEL), const),
        ],
        out_specs=pl.BlockSpec((tm, D_MODEL), tile),
        out_shape=jax.ShapeDtypeStruct((n_tiles * tm, D_MODEL), F32),
        scratch_shapes=[pltpu.VMEM((2, TOP_K, tm * ROW_TILE, PACKED_LANES), jnp.uint32),
                        pltpu.SemaphoreType.DMA((2,))],
        compiler_params=_cparams(("arbitrary",)),
        name=name,
    )(dest, dest, x1, wts_t, yb, w_sg, w_su, w_sd, ln_g, ln_b)


def kernel(x_prompt, x_sample, cache_win_k, cache_win_v, state_hgrn, cache_mem_k, cache_mem_v, mem_prompt,
           w_in, w_mem_kv, a_sinks, b_lb_logits, b_norm_g, w_branch, w_out, ln1_g, ln1_b,
           w_router, router_bias, w_exp_gate, w_exp_up, w_exp_down, w_sh_gate, w_sh_up, w_sh_down, ln2_g, ln2_b):
    assert w_in.shape[0] == DEPTH == 1
    batch, seq, _ = x_prompt.shape
    dbatch, dseq, _ = x_sample.shape
    n_p = batch * seq
    n_s = dbatch * dseq
    n = n_p + n_s
    l = 0

    xp2 = x_prompt.reshape(n_p, D_MODEL)
    xs2 = x_sample.reshape(n_s, D_MODEL)
    win = w_in[l]
    o_k, o_v, o_b = A_WIDTH, A_WIDTH + A_KV_WIDTH, A_WIDTH + 2 * A_KV_WIDTH
    o_c = o_b + 4 * B_WIDTH
    o_g = o_c + C_WIDTH
    groups = [
        ("aq", 0, o_k, 1024, BF16, A_HEAD_DIM ** -0.5),
        ("akv", o_k, o_b, 2 * A_KV_WIDTH, F32, 1.0),
        ("hgrn", o_b, o_c, 1024, F32, 1.0),
        ("cq", o_c, o_g, 1024, BF16, C_HEAD_DIM ** -0.5),
        ("gate", o_g, win.shape[1], 1024, F32, 1.0),
    ]
    zp, zs = {}, {}
    for gname, c0, c1, tn, dt, scale in groups:
        wslice = win[:, c0:c1].astype(BF16)
        zp[gname] = _matmul(xp2, wslice, tm=1024, tn=tn, out_dtype=dt, scale=scale, name=f"proj_{gname}_p")
        zs[gname] = _matmul(xs2, wslice, tm=n_s, tn=tn, out_dtype=dt, scale=scale, name=f"proj_{gname}_s")

    a_p = _swa_prompt(zp["aq"], zp["akv"], a_sinks[l], batch=batch, seq=seq)
    lc = cache_win_k.shape[2]
    kc = cache_win_k[l].reshape(dbatch, lc, A_KV_WIDTH)
    vc = cache_win_v[l].reshape(dbatch, lc, A_KV_WIDTH)
    a_s = _swa_sample(zs["aq"], zs["akv"], kc, vc, a_sinks[l], row0=0, batch=dbatch, seq=dseq)

    lower = jnp.cumsum(jax.nn.softmax(b_lb_logits.astype(F32), axis=0), axis=0)[l]
    lbp = jnp.stack([jnp.log(lower), jnp.log1p(-lower), 1.0 - lower])
    ng = jnp.tile(b_norm_g[l].astype(F32), B_HEADS).reshape(1, B_WIDTH)
    s_zero = jnp.zeros((batch, B_HEADS, B_KEY_DIM, B_VAL_DIM), F32)
    b_p, hs_p = _hgrn(zp["hgrn"], lbp, ng, s_zero, row0=0, batch=batch, seq=seq, blk=CHUNK)
    b_s, hs_s = _hgrn(zs["hgrn"], lbp, ng, state_hgrn[l].astype(F32), row0=0, batch=dbatch, seq=dseq, blk=dseq)

    mem = mem_prompt.reshape(batch * MEM_LEN, D_MODEL)
    wmem = w_mem_kv[l]
    mk = _matmul(mem, wmem[:, :C_WIDTH].astype(BF16), tm=batch * MEM_LEN, tn=512, out_dtype=F32, name="proj_mem_k")
    mv = _matmul(mem, wmem[:, C_WIDTH:].astype(BF16), tm=batch * MEM_LEN, tn=512, out_dtype=F32, name="proj_mem_v")
    mk = mk.reshape(batch, MEM_LEN, C_WIDTH)
    mv = mv.reshape(batch, MEM_LEN, C_WIDTH)
    c_p = _cattn(zp["cq"], mk, mv, row0=0, batch=batch, seq=seq, tq=512)
    c_s = _cattn(zs["cq"], cache_mem_k[l].reshape(dbatch, MEM_LEN, C_WIDTH),
                 cache_mem_v[l].reshape(dbatch, MEM_LEN, C_WIDTH), row0=0, batch=dbatch, seq=dseq, tq=dseq)

    w_br = w_branch[l].astype(BF16)
    route_w = (w_out[l].astype(BF16), ln1_g[l].reshape(1, D_MODEL), ln1_b[l].reshape(1, D_MODEL),
               w_router[l].T.astype(BF16), router_bias[l].reshape(N_EXPERTS, 1).astype(F32))
    segs = []
    cnt = jnp.zeros((N_EXPERTS, 128), jnp.int32)
    for tag, x2, abc, gate, tm_r in (("p", xp2, (a_p, b_p, c_p), zp["gate"], 512), ("s", xs2, (a_s, b_s, c_s), zs["gate"], n_s)):
        h = _merge(*abc, gate, w_br, tm=tm_r, name=f"merge_{tag}")
        x1, x1p, idx, wts, pos, cnt = _out_ln_route(x2, h, *route_w, cnt, tm=tm_r, name=f"out_ln_route_{tag}")
        segs.append(dict(tag=tag, x1=x1, x1p=x1p, idx=idx, wts=wts, pos=pos, tm=tm_r, rows=x2.shape[0]))

    m = MOE_BLOCK
    n_pairs = n * TOP_K
    n_blocks = (n_pairs + m - 1) // m + N_EXPERTS
    counts = cnt[:, 0]
    padded = (counts + m - 1) // m * m
    pad_end = jnp.cumsum(padded)
    pad_start = (pad_end - padded).astype(jnp.int32)
    n_valid = pad_end[-1:] // m
    e_ids = jnp.arange(N_EXPERTS, dtype=jnp.int32)
    blk_first = jnp.arange(n_blocks, dtype=jnp.int32) * m
    blk_e = jnp.sum((blk_first[:, None] >= pad_end[None, :]).astype(jnp.int32), axis=1)
    blk_e = jnp.minimum(blk_e, N_EXPERTS - 1)
    z_pad = jnp.where(padded > counts, pad_end - m, -1)
    z_tail = jnp.where(blk_first >= pad_end[-1], blk_first, -1)
    zrows = jnp.concatenate([z_pad, z_tail]).astype(jnp.int32)

    def retile(dest, rows, tm_from, tm_to):
        d = dest.reshape(rows // tm_from, TOP_K, tm_from // tm_to, tm_to)
        return jnp.transpose(d, (0, 2, 1, 3)).reshape(rows // tm_to, TOP_K, tm_to)

    tm_d = 256
    for seg in segs:
        first_row = jnp.sum(jnp.where(seg["idx"][..., None] == e_ids, pad_start, 0), axis=-1)
        seg["dest"] = first_row + seg["pos"]
    dest_d = jnp.concatenate([retile(seg["dest"], seg["rows"], seg["tm"], tm_d) for seg in segs], axis=0)
    xs = _dispatch(zrows, dest_d, segs[0]["x1p"], segs[1]["x1p"], n_rows=n_blocks * m, tm=tm_d, block_rows=m)
    yb = _moe(xs, blk_e, n_valid.astype(jnp.int32), w_exp_gate[l].astype(F32), w_exp_up[l].astype(F32),
              w_exp_down[l].astype(F32))
    tm_f = 128
    shared_w = (w_sh_gate[l].astype(BF16), w_sh_up[l].astype(BF16), w_sh_down[l].astype(BF16),
                ln2_g[l].reshape(1, D_MODEL), ln2_b[l].reshape(1, D_MODEL))
    ys = []
    for seg in segs:
        rows = seg["rows"]
        wts_t = jnp.transpose(seg["wts"], (0, 2, 1)).reshape(rows, TOP_K)
        ys.append(_final(retile(seg["dest"], rows, seg["tm"], tm_f), seg["x1"], wts_t, yb, *shared_w, tm=tm_f,
                         name=f"combine_shared_ln2_{seg['tag']}"))
    y_p = ys[0].reshape(batch, seq, D_MODEL)
    y_s = ys[1].reshape(dbatch, dseq, D_MODEL)

    kv_p = zp["akv"].reshape(batch, seq, 2, A_KV_HEADS, A_HEAD_DIM)[:, -lc:]
    k_p, v_p = kv_p[:, :, 0], kv_p[:, :, 1]
    kv_s = zs["akv"].reshape(dbatch, dseq, 2, A_KV_HEADS, A_HEAD_DIM)
    k_s, v_s = kv_s[:, :, 0], kv_s[:, :, 1]
    wk_s = jnp.concatenate([cache_win_k[l].astype(F32), k_s], axis=1)[:, -lc:]
    wv_s = jnp.concatenate([cache_win_v[l].astype(F32), v_s], axis=1)[:, -lc:]
    mk_o = mk.reshape(batch, MEM_LEN, C_HEADS, C_HEAD_DIM)
    mv_o = mv.reshape(batch, MEM_LEN, C_HEADS, C_HEAD_DIM)
    return (y_p, y_s, k_p[None], v_p[None], hs_p[None], mk_o[None], mv_o[None], wk_s[None], wv_s[None], hs_s[None])
```

```python
import functools

import jax
import jax.numpy as jnp
import numpy as np
from jax import lax
from jax.experimental import pallas as pl
from jax.experimental.pallas import tpu as pltpu

F32 = jnp.float32
BF16 = jnp.bfloat16

D_MODEL = 2048
DEPTH = 1
PAST_LEN = 2048
CHUNK = 64
A_HEADS = 16
A_KV_HEADS = 4
A_GROUP = A_HEADS // A_KV_HEADS
A_HEAD_DIM = 64
A_WIDTH = A_HEADS * A_HEAD_DIM
A_KV_WIDTH = A_KV_HEADS * A_HEAD_DIM
WINDOW = 128
WIN_CHUNKS = WINDOW // CHUNK
B_HEADS = 8
B_KEY_DIM = 128
B_VAL_DIM = 128
B_WIDTH = B_HEADS * B_VAL_DIM
SUB = 16
MEM_LEN = 256
C_HEADS = 4
C_HEAD_DIM = 256
C_WIDTH = C_HEADS * C_HEAD_DIM
N_EXPERTS = 64
N_GROUPS = 8
GROUP_SIZE = N_EXPERTS // N_GROUPS
TOPK_GROUPS = 4
TOP_K = 8
EXPERT_DIM = 512
ROUTED_SCALE = 2.5
DN_ALPHA = (2 * DEPTH) ** 0.25
LN_EPS = 1e-5
RMS_EPS = 1e-6

MOE_BLOCK = 512
VMEM_LIMIT = 56 * 1024 * 1024


def _cparams(sem):
    return pltpu.CompilerParams(dimension_semantics=sem, vmem_limit_bytes=VMEM_LIMIT)


def _resident(shape, index_map):
    return pl.BlockSpec(shape, index_map, pipeline_mode=pl.Buffered(1))


def _sigmoid(x):
    return 1.0 / (1.0 + jnp.exp(-x))


def _layer_norm(x, g, b):
    mu = jnp.mean(x, axis=-1, keepdims=True)
    xc = x - mu
    var = jnp.mean(xc * xc, axis=-1, keepdims=True)
    return xc * lax.rsqrt(var + LN_EPS) * g + b


def _mm_kernel(x_ref, w_ref, o_ref, *, scale):
    acc = jnp.dot(x_ref[...].astype(BF16), w_ref[...], preferred_element_type=F32)
    if scale != 1.0:
        acc = acc * scale
    o_ref[...] = acc.astype(o_ref.dtype)


def _matmul(x, w, *, tm, tn, out_dtype, scale=1.0, name):
    m, k = x.shape
    n = w.shape[1]
    return pl.pallas_call(
        functools.partial(_mm_kernel, scale=scale),
        grid=(m // tm, n // tn),
        in_specs=[pl.BlockSpec((tm, k), lambda i, j: (i, 0)), pl.BlockSpec((k, tn), lambda i, j: (0, j))],
        out_specs=pl.BlockSpec((tm, tn), lambda i, j: (i, j)),
        out_shape=jax.ShapeDtypeStruct((m, n), out_dtype),
        compiler_params=_cparams(("parallel", "parallel")),
        name=name,
    )(x, w)


SWA_KEYS = 256
HEAD_PAIR = 2 * A_HEAD_DIM


def _kv_head_planes(x):
    lane = lax.broadcasted_iota(jnp.int32, (1, HEAD_PAIR), 1)
    low = lane < A_HEAD_DIM
    planes = []
    for pair in range(A_KV_HEADS // 2):
        own = x[:, pair * HEAD_PAIR:(pair + 1) * HEAD_PAIR]
        swapped = pltpu.roll(own, A_HEAD_DIM, 1)
        planes.append((jnp.where(low, own, 0.0).astype(BF16), jnp.where(low, 0.0, swapped).astype(BF16)))
        planes.append((jnp.where(low, swapped, 0.0).astype(BF16), jnp.where(low, 0.0, own).astype(BF16)))
    return planes


SWA_LOOKAHEAD = 8


def _swa_chunks(q_ref, rows, chunk_rows, k_planes, v_planes, n_band, bias_ref, valid_of, o_ref):
    zeros_k = jnp.zeros((SWA_KEYS - n_band, HEAD_PAIR), BF16)
    ones_v = jnp.ones((SWA_KEYS, HEAD_PAIR), BF16)
    tiles = [(c, h, x) for c in range(len(chunk_rows)) for h in range(A_KV_HEADS) for x in range(2)]

    def band(plane, c):
        return jnp.concatenate([plane[chunk_rows[c]:chunk_rows[c] + n_band], zeros_k], axis=0)

    def scores(t):
        c, h, x = tiles[t]
        q2 = jnp.concatenate([q_ref[c * rows:(c + 1) * rows, (2 * h + j) * HEAD_PAIR:(2 * h + j + 1) * HEAD_PAIR]
                              for j in range(2)], axis=0).astype(BF16)
        s = lax.dot_general(q2, band(k_planes[h][x], c), (((1,), (1,)), ((), ())), preferred_element_type=F32)
        s = s - bias_ref[h, x]
        valid = valid_of(c)
        return s if valid is None else jnp.where(valid, s, -jnp.inf)

    pending = {t: scores(t) for t in range(min(SWA_LOOKAHEAD, len(tiles)))}
    even = None
    for t, (c, h, x) in enumerate(tiles):
        s = pending.pop(t)
        p = jnp.exp(s - jnp.max(s, axis=-1, keepdims=True)).astype(BF16)
        if t + SWA_LOOKAHEAD < len(tiles):
            pending[t + SWA_LOOKAHEAD] = scores(t + SWA_LOOKAHEAD)
        vw = jnp.concatenate([band(v_planes[h][x], c), ones_v], axis=1)
        o = jnp.dot(p, vw, preferred_element_type=F32)
        o = o[:, :HEAD_PAIR] / o[:, HEAD_PAIR:]
        if x == 0:
            even = o
        else:
            out = even + o
            for j in range(2):
                o_ref[c * rows:(c + 1) * rows, (2 * h + j) * HEAD_PAIR:(2 * h + j + 1) * HEAD_PAIR] = (
                    out[j * rows:(j + 1) * rows].astype(o_ref.dtype))


def _swa_prompt_kernel(q_ref, kc_ref, kp_ref, vc_ref, vp_ref, bias_ref, o_ref, *, n_chunks):
    i = pl.program_id(1)
    pad = WIN_CHUNKS * CHUNK
    n_band = pad + CHUNK
    k_planes = _kv_head_planes(jnp.concatenate([kp_ref[...], kc_ref[...]], axis=0))
    v_planes = _kv_head_planes(jnp.concatenate([vp_ref[...], vc_ref[...]], axis=0))
    s_idx = lax.broadcasted_iota(jnp.int32, (1, SWA_KEYS), 1)

    def valid_of(c):
        return jnp.logical_or(s_idx + (i * (n_chunks * CHUNK) + c * CHUNK - pad) >= 0, s_idx >= n_band)

    _swa_chunks(q_ref, CHUNK, [c * CHUNK for c in range(n_chunks)], k_planes, v_planes, n_band, bias_ref, valid_of,
                o_ref)


def _swa_sample_kernel(q_ref, kn_ref, kc_ref, vn_ref, vc_ref, bias_ref, o_ref):
    k_planes = _kv_head_planes(jnp.concatenate([kc_ref[0], kn_ref[...]], axis=0))
    v_planes = _kv_head_planes(jnp.concatenate([vc_ref[0], vn_ref[...]], axis=0))
    n_band = kc_ref.shape[1] + kn_ref.shape[0]
    _swa_chunks(q_ref, q_ref.shape[0], [0], k_planes, v_planes, n_band, bias_ref, lambda c: None, o_ref)


def _alibi_slopes():
    return (2.0 ** (-8.0 * np.arange(1, A_HEADS + 1) / A_HEADS)).astype(np.float32)


def _swa_bias(q_pos, k_pos, valid, sinks):
    n_q, n_k = len(q_pos), len(k_pos)
    assert n_k < SWA_KEYS
    dist = np.abs(q_pos[:, None] - k_pos[None, :]).astype(np.float32)
    band = _alibi_slopes()[:, None, None] * dist[None]
    if valid is not None:
        band = np.where(valid[None], band, np.inf)
    pad = np.full((A_HEADS, n_q, SWA_KEYS - n_k - 1), np.inf, np.float32)
    sink = jnp.broadcast_to(-sinks.astype(F32)[:, None, None], (A_HEADS, n_q, 1))
    bias = jnp.concatenate([jnp.asarray(band.astype(np.float32)), sink, jnp.asarray(pad)], axis=2)
    bias = bias.reshape(A_KV_HEADS, 2, 2, n_q, SWA_KEYS).transpose(0, 2, 1, 3, 4)
    return bias.reshape(A_KV_HEADS, 2, 2 * n_q, SWA_KEYS)


def _swa_prompt(q, kv, sinks, *, batch, seq, n_chunks=4):
    tq = n_chunks * CHUNK
    pad = WIN_CHUNKS * CHUNK
    nb = seq // tq
    bias = _swa_bias(pad + np.arange(CHUNK), np.arange(pad + CHUNK), None, sinks)
    prev_per_blk = tq // pad

    def cur(b, i, col=0):
        return (b * nb + i, col)

    def prev(b, i, col=0):
        return (jnp.maximum((b * nb + i) * prev_per_blk - 1, b * nb * prev_per_blk), col)

    return pl.pallas_call(
        functools.partial(_swa_prompt_kernel, n_chunks=n_chunks),
        grid=(batch, nb),
        in_specs=[
            pl.BlockSpec((tq, A_WIDTH), cur),
            pl.BlockSpec((tq, A_KV_WIDTH), cur),
            pl.BlockSpec((pad, A_KV_WIDTH), prev),
            pl.BlockSpec((tq, A_KV_WIDTH), functools.partial(cur, col=1)),
            pl.BlockSpec((pad, A_KV_WIDTH), functools.partial(prev, col=1)),
            pl.BlockSpec(bias.shape, lambda b, i: (0, 0, 0, 0)),
        ],
        out_specs=pl.BlockSpec((tq, A_WIDTH), cur),
        out_shape=jax.ShapeDtypeStruct((batch * seq, A_WIDTH), BF16),
        compiler_params=_cparams(("parallel", "arbitrary")),
        name="swa_prompt",
    )(q, kv, kv, kv, kv, bias)


def _swa_sample(q, kv, k_cache, v_cache, sinks, *, row0, batch, seq):
    lc = k_cache.shape[1]
    q_pos = PAST_LEN + np.arange(seq)
    k_pos = PAST_LEN - lc + np.arange(lc + seq)
    cdiff = q_pos[:, None] // CHUNK - k_pos[None, :] // CHUNK
    valid = (cdiff >= 0) & (cdiff <= WIN_CHUNKS)
    bias = _swa_bias(q_pos, k_pos, valid, sinks)
    blk0 = row0 // seq

    def rows(b):
        return (blk0 + b, 0)

    return pl.pallas_call(
        _swa_sample_kernel,
        grid=(batch,),
        in_specs=[
            pl.BlockSpec((seq, A_WIDTH), rows),
            pl.BlockSpec((seq, A_KV_WIDTH), rows),
            pl.BlockSpec((1, lc, A_KV_WIDTH), lambda b: (b, 0, 0)),
            pl.BlockSpec((seq, A_KV_WIDTH), lambda b: (blk0 + b, 1)),
            pl.BlockSpec((1, lc, A_KV_WIDTH), lambda b: (b, 0, 0)),
            pl.BlockSpec(bias.shape, lambda b: (0, 0, 0, 0)),
        ],
        out_specs=pl.BlockSpec((seq, A_WIDTH), lambda b: (b, 0)),
        out_shape=jax.ShapeDtypeStruct((batch * seq, A_WIDTH), BF16),
        compiler_params=_cparams(("parallel",)),
        name="swa_sample",
    )(q, kv, k_cache, kv, v_cache, bias)


def _split3(x):
    hi = x.astype(BF16)
    r1 = x - hi.astype(F32)
    mid = r1.astype(BF16)
    lo = (r1 - mid.astype(F32)).astype(BF16)
    return hi, mid, lo


def _dot3(mat, parts):
    acc = jnp.dot(mat, parts[0], preferred_element_type=F32)
    acc = acc + jnp.dot(mat, parts[1], preferred_element_type=F32)
    return acc + jnp.dot(mat, parts[2], preferred_element_type=F32)


def _hgrn_kernel(zq_ref, zf_ref, zi_ref, zg_ref, lb_ref, ng_ref, tri_ref, tsel_ref, s0_ref, o_ref, sfin_ref, s_scr,
                 *, blk, nbat):
    j = pl.program_id(1)

    @pl.when(j == 0)
    def _():
        s_scr[...] = s0_ref[...]

    log_lb = lb_ref[0:1, :]
    log1m_lb = lb_ref[1:2, :]
    one_m_lb = lb_ref[2:3, :]
    row = lax.broadcasted_iota(jnp.int32, (blk, 1), 0)
    n_sub = blk // SUB
    tril = lax.broadcasted_iota(jnp.int32, (blk, blk), 0) >= lax.broadcasted_iota(jnp.int32, (blk, blk), 1)
    heads = [slice(h * B_KEY_DIM, (h + 1) * B_KEY_DIM) for h in range(B_HEADS)]

    def prepare(bi):
        bq = zq_ref[bi]
        fl = zf_ref[bi]
        q = bq * _sigmoid(bq) * (B_KEY_DIM ** -0.5)
        log_sig = jnp.minimum(fl, 0.0) - jnp.log1p(jnp.exp(-jnp.abs(fl)))
        c = log1m_lb + log_sig
        logf = jnp.maximum(log_lb, c) + jnp.log1p(jnp.exp(-jnp.abs(log_lb - c)))
        k = one_m_lb * _sigmoid(-fl)
        parts = _split3(logf)
        b = _dot3(tri_ref[...], parts)
        rq = _dot3(tsel_ref[...], parts)
        qt = (q * jnp.exp(b - rq)).astype(BF16)
        qb = (q * jnp.exp(b)).astype(BF16)
        b_last = b[blk - 1:blk, :]
        khat = (k * jnp.exp(b_last - b)).astype(BF16)
        vb = zi_ref[bi].astype(BF16)
        kts = []
        for i in range(n_sub):
            r_i = rq[i * SUB:i * SUB + 1, :]
            kts.append(jnp.where(row < (i + 1) * SUB, k * jnp.exp(r_i - b), 0.0).astype(BF16))
        s_old = [s_scr[bi, h] for h in range(B_HEADS)]
        a_raw, o_state, ds = [], [], []
        for h, hs in enumerate(heads):
            a_raw.append(jnp.concatenate(
                [lax.dot_general(qt[i * SUB:(i + 1) * SUB, hs], kts[i][:, hs], (((1,), (1,)), ((), ())),
                                 preferred_element_type=F32) for i in range(n_sub)], axis=0))
            o_state.append(jnp.dot(qb[:, hs], s_old[h].astype(BF16), preferred_element_type=F32))
            ds.append(lax.dot_general(khat[:, hs], vb[:, hs], (((0,), (0,)), ((), ())), preferred_element_type=F32))
        return dict(a_raw=a_raw, o_state=o_state, ds=ds, s_old=s_old, vb=vb, e_last=jnp.exp(b_last))

    def finish(bi, p):
        outs = []
        for h, hs in enumerate(heads):
            a = jnp.where(tril, p["a_raw"][h], 0.0).astype(BF16)
            o = jnp.dot(a, p["vb"][:, hs], preferred_element_type=F32) + p["o_state"][h]
            decay = jnp.transpose(jnp.broadcast_to(p["e_last"][:, hs], (B_KEY_DIM, B_KEY_DIM)))
            s_scr[bi, h] = decay * p["s_old"][h] + p["ds"][h]
            outs.append(o * lax.rsqrt(jnp.mean(o * o, axis=-1, keepdims=True) + RMS_EPS))
        bg = zg_ref[bi]
        o_all = jnp.concatenate(outs, axis=1) * ng_ref[...] * (bg * _sigmoid(bg))
        o_ref[bi] = o_all.astype(o_ref.dtype)

    prepared = [prepare(bi) for bi in range(nbat)]
    for bi in range(nbat):
        finish(bi, prepared[bi])

    @pl.when(j == pl.num_programs(1) - 1)
    def _():
        sfin_ref[...] = s_scr[...]


def _hgrn(zb, lbp, ng, s0, *, batch, seq, blk, nbat):
    nb = seq // blk
    t = np.arange(blk)
    tri = jnp.asarray((t[:, None] >= t[None, :]).astype(np.float32), dtype=BF16)
    tsel = jnp.asarray((t[None, :] < (t[:, None] // SUB) * SUB).astype(np.float32), dtype=BF16)
    z3 = zb.reshape(batch, seq, 4 * B_WIDTH)

    def zspec(col):
        return pl.BlockSpec((nbat, blk, B_WIDTH), lambda b, j: (b, j, col))

    state_spec = pl.BlockSpec((nbat, B_HEADS, B_KEY_DIM, B_VAL_DIM), lambda b, j: (b, 0, 0, 0))
    out, s_fin = pl.pallas_call(
        functools.partial(_hgrn_kernel, blk=blk, nbat=nbat),
        grid=(batch // nbat, nb),
        in_specs=[
            zspec(0), zspec(1), zspec(2), zspec(3),
            pl.BlockSpec((3, B_WIDTH), lambda b, j: (0, 0)),
            pl.BlockSpec((1, B_WIDTH), lambda b, j: (0, 0)),
            pl.BlockSpec((blk, blk), lambda b, j: (0, 0)),
            pl.BlockSpec((blk, blk), lambda b, j: (0, 0)),
            state_spec,
        ],
        out_specs=[pl.BlockSpec((nbat, blk, B_WIDTH), lambda b, j: (b, j, 0)), state_spec],
        out_shape=[jax.ShapeDtypeStruct((batch, seq, B_WIDTH), BF16),
                   jax.ShapeDtypeStruct((batch, B_HEADS, B_KEY_DIM, B_VAL_DIM), F32)],
        scratch_shapes=[pltpu.VMEM((nbat, B_HEADS, B_KEY_DIM, B_VAL_DIM), F32)],
        compiler_params=_cparams(("parallel", "arbitrary")),
        name=f"hgrn_blk{blk}",
    )(z3, z3, z3, z3, lbp, ng, tri, tsel, s0)
    return out.reshape(batch * seq, B_WIDTH), s_fin


def _cattn_kernel(q_ref, mk_ref, mv_ref, o_ref):
    mk = mk_ref[0].astype(BF16)
    mv = mv_ref[0].astype(BF16)
    q = q_ref[...]
    for h in range(C_HEADS):
        hs = slice(h * C_HEAD_DIM, (h + 1) * C_HEAD_DIM)
        s = lax.dot_general(q[:, hs], mk[:, hs], (((1,), (1,)), ((), ())), preferred_element_type=F32)
        p = jnp.exp(s - jnp.max(s, axis=-1, keepdims=True))
        p = p / jnp.sum(p, axis=-1, keepdims=True)
        o_ref[:, hs] = jnp.dot(p.astype(BF16), mv[:, hs], preferred_element_type=F32).astype(o_ref.dtype)


def _cattn(q, mk, mv, *, row0, batch, seq, tq):
    nb = seq // tq
    blk0 = row0 // tq
    mem_spec = pl.BlockSpec((1, MEM_LEN, C_WIDTH), lambda b, i: (b, 0, 0))
    return pl.pallas_call(
        _cattn_kernel,
        grid=(batch, nb),
        in_specs=[pl.BlockSpec((tq, C_WIDTH), lambda b, i: (blk0 + b * nb + i, 0)), mem_spec, mem_spec],
        out_specs=pl.BlockSpec((tq, C_WIDTH), lambda b, i: (b * nb + i, 0)),
        out_shape=jax.ShapeDtypeStruct((batch * seq, C_WIDTH), BF16),
        compiler_params=_cparams(("parallel", "parallel")),
        name=f"cattn_tq{tq}",
    )(q, mk, mv)


def _merge_kernel(a_ref, b_ref, c_ref, g0_ref, g1_ref, g2_ref, pa_ref, pb_ref, pc_ref, o_ref):
    da = jnp.dot(a_ref[...], pa_ref[...], preferred_element_type=F32)
    db = jnp.dot(b_ref[...], pb_ref[...], preferred_element_type=F32)
    dc = jnp.dot(c_ref[...], pc_ref[...], preferred_element_type=F32)
    h = _sigmoid(g0_ref[...]) * da + _sigmoid(g1_ref[...]) * db + _sigmoid(g2_ref[...]) * dc
    o_ref[...] = h.astype(o_ref.dtype)


def _merge(a, b, c, gl, p, *, tm, name):
    n, width = a.shape

    def rows(i):
        return (i, 0)

    return pl.pallas_call(
        _merge_kernel,
        grid=(n // tm,),
        in_specs=[
            pl.BlockSpec((tm, width), rows), pl.BlockSpec((tm, width), rows), pl.BlockSpec((tm, width), rows),
            pl.BlockSpec((tm, D_MODEL), lambda i: (i, 0)),
            pl.BlockSpec((tm, D_MODEL), lambda i: (i, 1)),
            pl.BlockSpec((tm, D_MODEL), lambda i: (i, 2)),
            _resident((width, D_MODEL), lambda i: (0, 0)),
            _resident((width, D_MODEL), lambda i: (1, 0)),
            _resident((width, D_MODEL), lambda i: (2, 0)),
        ],
        out_specs=pl.BlockSpec((tm, D_MODEL), rows),
        out_shape=jax.ShapeDtypeStruct((n, D_MODEL), BF16),
        compiler_params=_cparams(("parallel",)),
        name=name,
    )(a, b, c, gl, gl, gl, p, p, p)


def _first_index(hit_src, m, iota, size, axis):
    return jnp.min(jnp.where(hit_src == m, iota, size), axis=axis, keepdims=True)


def _route(x1, wr_ref, rb_ref, upper_ref, run_ref, idx_ref, wts_ref, pos_ref):
    tm = x1.shape[0]
    logits = lax.dot_general(wr_ref[...], x1.astype(BF16), (((1,), (1,)), ((), ())),
                             preferred_element_type=F32)
    scores = _sigmoid(logits)
    choice = scores + rb_ref[...]
    g3 = choice.reshape(N_GROUPS, GROUP_SIZE, tm)
    mem_iota = lax.broadcasted_iota(jnp.int32, g3.shape, 1)
    m1 = jnp.max(g3, axis=1, keepdims=True)
    first = _first_index(g3, m1, mem_iota, GROUP_SIZE, 1)
    m2 = jnp.max(jnp.where(mem_iota == first, -jnp.inf, g3), axis=1, keepdims=True)
    gscore = (m1 + m2).reshape(N_GROUPS, tm)
    g_iota = lax.broadcasted_iota(jnp.int32, gscore.shape, 0)
    gsel = jnp.zeros(gscore.shape, F32)
    cur = gscore
    for _ in range(TOPK_GROUPS):
        m = jnp.max(cur, axis=0, keepdims=True)
        hit = g_iota == _first_index(cur, m, g_iota, N_GROUPS, 0)
        gsel = jnp.where(hit, 1.0, gsel)
        cur = jnp.where(hit, -jnp.inf, cur)
    masked = jnp.where(gsel.reshape(N_GROUPS, 1, tm) > 0.5, g3, -jnp.inf).reshape(N_EXPERTS, tm)
    e_iota = lax.broadcasted_iota(jnp.int32, masked.shape, 0)
    sel = jnp.zeros(masked.shape, F32)
    cur = masked
    idxs, ws = [], []
    for _ in range(TOP_K):
        m = jnp.max(cur, axis=0, keepdims=True)
        ei = _first_index(cur, m, e_iota, N_EXPERTS, 0)
        hit = e_iota == ei
        idxs.append(ei)
        ws.append(jnp.sum(jnp.where(hit, scores, 0.0), axis=0, keepdims=True))
        sel = jnp.where(hit, 1.0, sel)
        cur = jnp.where(hit, -jnp.inf, cur)
    w = jnp.concatenate(ws, axis=0)
    w = w / jnp.sum(w, axis=0, keepdims=True) * ROUTED_SCALE
    excl = jnp.dot(sel.astype(BF16), upper_ref[...], preferred_element_type=F32)
    posfull = excl + run_ref[:, 0:1]
    run_ref[...] = run_ref[...] + jnp.sum(sel, axis=1, keepdims=True)
    pos = [jnp.sum(jnp.where(e_iota == ei, posfull, 0.0), axis=0, keepdims=True) for ei in idxs]
    idx_ref[0] = jnp.concatenate(idxs, axis=0)
    wts_ref[0] = w
    pos_ref[0] = jnp.concatenate(pos, axis=0).astype(jnp.int32)


def _pack_halves(x):
    c = x.shape[1] // 2
    lo = lax.bitcast_convert_type(x[:, :c].astype(BF16).astype(F32), jnp.uint32)
    hi = lax.bitcast_convert_type(x[:, c:].astype(BF16).astype(F32), jnp.uint32)
    return (hi & jnp.uint32(0xFFFF0000)) | (lo >> 16)


def _unpack_halves(w):
    lo = lax.bitcast_convert_type(w << 16, F32)
    hi = lax.bitcast_convert_type(w & jnp.uint32(0xFFFF0000), F32)
    return lo, hi


ROW_TILE = 8
PACKED_LANES = D_MODEL // 2 // ROW_TILE


def _store_row_tiles(ref, val):
    rows = val.shape[0]
    for s in range(ROW_TILE):
        ref[pl.ds(s, rows, stride=ROW_TILE), :] = val[:, s * PACKED_LANES:(s + 1) * PACKED_LANES]


def _load_row_tiles(ref, rows):
    return [ref[pl.ds(s, rows, stride=ROW_TILE), :] for s in range(ROW_TILE)]


def _out_ln_route_kernel(x_ref, h_ref, wo_ref, g_ref, b_ref, wr_ref, rb_ref, upper_ref, run0_ref,
                         x1_ref, x1p_ref, idx_ref, wts_ref, pos_ref, cnt_ref, run_ref):
    @pl.when(pl.program_id(0) == 0)
    def _():
        run_ref[...] = run0_ref[...].astype(F32)

    y = DN_ALPHA * x_ref[...] + jnp.dot(h_ref[...], wo_ref[...], preferred_element_type=F32)
    x1 = _layer_norm(y, g_ref[...], b_ref[...])
    x1_ref[...] = x1
    _store_row_tiles(x1p_ref, _pack_halves(x1))
    _route(x1, wr_ref, rb_ref, upper_ref, run_ref, idx_ref, wts_ref, pos_ref)
    cnt_ref[...] = run_ref[...].astype(jnp.int32)


def _out_ln_route(x, h, w_out, ln_g, ln_b, w_router_t, router_bias, run0, *, tm, name):
    n = h.shape[0]
    nt = n // tm
    t = np.arange(tm)
    upper = jnp.asarray((t[:, None] < t[None, :]).astype(np.float32), dtype=BF16)

    def rows(i):
        return (i, 0)

    def const(i):
        return (0, 0)

    small = pl.BlockSpec((1, TOP_K, tm), lambda i: (i, 0, 0))
    return pl.pallas_call(
        _out_ln_route_kernel,
        grid=(nt,),
        in_specs=[
            pl.BlockSpec((tm, D_MODEL), rows),
            pl.BlockSpec((tm, D_MODEL), rows),
            _resident((D_MODEL, D_MODEL), const),
            pl.BlockSpec((1, D_MODEL), const),
            pl.BlockSpec((1, D_MODEL), const),
            pl.BlockSpec((N_EXPERTS, D_MODEL), const),
            pl.BlockSpec((N_EXPERTS, 1), const),
            pl.BlockSpec((tm, tm), const),
            pl.BlockSpec((N_EXPERTS, 128), const),
        ],
        out_specs=[pl.BlockSpec((tm, D_MODEL), rows), pl.BlockSpec((tm * ROW_TILE, PACKED_LANES), rows), small, small,
                   small, pl.BlockSpec((N_EXPERTS, 128), const)],
        out_shape=[
            jax.ShapeDtypeStruct((n, D_MODEL), F32),
            jax.ShapeDtypeStruct((n * ROW_TILE, PACKED_LANES), jnp.uint32),
            jax.ShapeDtypeStruct((nt, TOP_K, tm), jnp.int32),
            jax.ShapeDtypeStruct((nt, TOP_K, tm), F32),
            jax.ShapeDtypeStruct((nt, TOP_K, tm), jnp.int32),
            jax.ShapeDtypeStruct((N_EXPERTS, 128), jnp.int32),
        ],
        scratch_shapes=[pltpu.VMEM((N_EXPERTS, 128), F32)],
        compiler_params=_cparams(("arbitrary",)),
        name=name,
    )(x, h, w_out, ln_g, ln_b, w_router_t, router_bias, upper, run0)


def _row_copy(src_hbm, row, dst, dst_row, sem):
    return pltpu.make_async_copy(src_hbm.at[pl.ds(row, 1)], dst.at[pl.ds(dst_row, 1)], sem)


def _tile_copy(src, src_row, dst, dst_row, sem):
    def tile(row):
        first = row * ROW_TILE
        return pl.ds(first if isinstance(row, int) else pl.multiple_of(first, ROW_TILE), ROW_TILE)

    return pltpu.make_async_copy(src.at[tile(src_row)], dst.at[tile(dst_row)], sem)


def _dispatch_kernel(zrow_ref, dest_ref, xa_ref, xb_ref, o_hbm, zbuf, sem, zsem, *, block_rows, n_first):
    tm = xa_ref.shape[0] // ROW_TILE
    n_zero = zrow_ref.shape[0]

    def zero_copy(b):
        start = pl.multiple_of(jnp.maximum(zrow_ref[b], 0) * ROW_TILE, ROW_TILE)
        return pltpu.make_async_copy(zbuf, o_hbm.at[pl.ds(start, block_rows * ROW_TILE)], zsem)

    def zero_unowned_rows():
        zbuf[...] = jnp.zeros_like(zbuf)

        def start(b, carry):
            @pl.when(zrow_ref[b] >= 0)
            def _():
                zero_copy(b).start()
            return carry

        def wait(b, carry):
            @pl.when(zrow_ref[b] >= 0)
            def _():
                zero_copy(b).wait()
            return carry

        lax.fori_loop(0, n_zero, start, 0)
        lax.fori_loop(0, n_zero, wait, 0)

    pl.when(pl.program_id(0) == 0)(zero_unowned_rows)

    def scatter_tile(x_ref):
        for t in range(tm):
            for k in range(TOP_K):
                _tile_copy(x_ref, t, o_hbm, dest_ref[0, k, t], sem).start(priority=k % 2)
        for k in range(TOP_K):
            pltpu.make_async_copy(x_ref, o_hbm.at[pl.ds(0, tm * ROW_TILE)], sem).wait()

    pl.when(pl.program_id(0) < n_first)(functools.partial(scatter_tile, xa_ref))
    pl.when(pl.program_id(0) >= n_first)(functools.partial(scatter_tile, xb_ref))


def _dispatch(zrows, dest_blk, x1p_a, x1p_b, *, n_rows, tm, block_rows):
    nf = x1p_a.shape[0] // ROW_TILE // tm
    ns = x1p_b.shape[0] // ROW_TILE // tm
    grid_spec = pltpu.PrefetchScalarGridSpec(
        num_scalar_prefetch=1,
        grid=(nf + ns,),
        in_specs=[
            pl.BlockSpec((1, TOP_K, tm), lambda i, z: (i, 0, 0), memory_space=pltpu.SMEM),
            pl.BlockSpec((tm * ROW_TILE, PACKED_LANES), lambda i, z: (jnp.minimum(i, nf - 1), 0)),
            pl.BlockSpec((tm * ROW_TILE, PACKED_LANES), lambda i, z: (jnp.clip(i - nf, 0, ns - 1), 0)),
        ],
        out_specs=pl.BlockSpec(memory_space=pl.ANY),
        scratch_shapes=[pltpu.VMEM((block_rows * ROW_TILE, PACKED_LANES), jnp.uint32),
                        pltpu.SemaphoreType.DMA(()), pltpu.SemaphoreType.DMA(())],
    )
    return pl.pallas_call(
        functools.partial(_dispatch_kernel, block_rows=block_rows, n_first=nf),
        grid_spec=grid_spec,
        out_shape=jax.ShapeDtypeStruct((n_rows * ROW_TILE, PACKED_LANES), jnp.uint32),
        compiler_params=_cparams(("arbitrary",)),
        name="dispatch",
    )(zrows, dest_blk, x1p_a, x1p_b)


def _moe_kernel(be_ref, nv_ref, x_ref, wg_ref, wu_ref, wd_ref, o_ref, wg_b, wu_b, wd_b):
    k = pl.program_id(0)
    m = x_ref.shape[0] // ROW_TILE
    live = k < nv_ref[0]
    new_expert = jnp.logical_or(k == 0, be_ref[k] != be_ref[jnp.maximum(k - 1, 0)])

    @pl.when(jnp.logical_and(live, new_expert))
    def _():
        wg_b[...] = wg_ref[0].astype(BF16)
        wu_b[...] = wu_ref[0].astype(BF16)
        wd_b[...] = wd_ref[0].astype(BF16)

    @pl.when(live)
    def _():
        lo, hi = _unpack_halves(jnp.concatenate(_load_row_tiles(x_ref, m), axis=1))
        x = jnp.concatenate([lo.astype(BF16), hi.astype(BF16)], axis=1)
        g = jnp.dot(x, wg_b[...], preferred_element_type=F32)
        u = jnp.dot(x, wu_b[...], preferred_element_type=F32)
        hmid = (g * _sigmoid(g) * u).astype(BF16)
        y = jnp.dot(hmid, wd_b[...], preferred_element_type=F32)
        _store_row_tiles(o_ref, _pack_halves(y))

    @pl.when(pl.program_id(0) >= nv_ref[0])
    def _():
        o_ref[...] = jnp.zeros_like(o_ref)


def _moe(xs, blk_e, n_valid, w_gate, w_up, w_down):
    n_blocks = blk_e.shape[0]
    m = MOE_BLOCK

    def rows(k, be, nv):
        return (jnp.minimum(k, nv[0] - 1), 0)

    def wspec(shape):
        return pl.BlockSpec(shape, lambda k, be, nv: (be[k], 0, 0))

    grid_spec = pltpu.PrefetchScalarGridSpec(
        num_scalar_prefetch=2,
        grid=(n_blocks,),
        in_specs=[
            pl.BlockSpec((m * ROW_TILE, PACKED_LANES), rows),
            wspec((1, D_MODEL, EXPERT_DIM)), wspec((1, D_MODEL, EXPERT_DIM)), wspec((1, EXPERT_DIM, D_MODEL)),
        ],
        out_specs=pl.BlockSpec((m * ROW_TILE, PACKED_LANES), lambda k, be, nv: (k, 0)),
        scratch_shapes=[pltpu.VMEM((D_MODEL, EXPERT_DIM), BF16), pltpu.VMEM((D_MODEL, EXPERT_DIM), BF16),
                        pltpu.VMEM((EXPERT_DIM, D_MODEL), BF16)],
    )
    return pl.pallas_call(
        _moe_kernel,
        grid_spec=grid_spec,
        out_shape=jax.ShapeDtypeStruct(xs.shape, jnp.uint32),
        compiler_params=_cparams(("arbitrary",)),
        name="moe_experts",
    )(blk_e, n_valid, xs, w_gate, w_up, w_down)


def _final_kernel(dest_ref, dest_next_ref, x1_ref, w_ref, yb_hbm, sg_ref, su_ref, sd_ref, g_ref, b_ref, o_ref,
                  ybuf, sem):
    i = pl.program_id(0)
    tm = x1_ref.shape[0]

    def start(d_ref, slot, t):
        for k in range(TOP_K):
            _tile_copy(yb_hbm, d_ref[0, k, t], ybuf.at[slot, k], t, sem.at[slot]).start(priority=k % 2)

    def wait_all(slot):
        for k in range(TOP_K):
            pltpu.make_async_copy(yb_hbm.at[pl.ds(0, tm * ROW_TILE)], ybuf.at[slot, k], sem.at[slot]).wait()

    @pl.when(i == 0)
    def _():
        def issue(t, carry):
            start(dest_ref, 0, t)
            return carry

        lax.fori_loop(0, tm, issue, 0, unroll=4)

    cur = i % 2
    nxt = 1 - cur
    wait_all(cur)
    x1 = x1_ref[...]
    xb = x1.astype(BF16)
    g = jnp.dot(xb, sg_ref[...], preferred_element_type=F32)
    u = jnp.dot(xb, su_ref[...], preferred_element_type=F32)
    w = w_ref[...]
    per = tm // TOP_K
    r_lo = [None] * ROW_TILE
    r_hi = [None] * ROW_TILE
    for k in range(TOP_K):
        for t in range(k * per, (k + 1) * per):
            start(dest_next_ref, nxt, t)
        wk = jnp.broadcast_to(w[:, k:k + 1], (tm, PACKED_LANES))
        for s, piece in enumerate(_load_row_tiles(ybuf.at[cur, k], tm)):
            lo, hi = _unpack_halves(piece)
            r_lo[s] = wk * lo if r_lo[s] is None else r_lo[s] + wk * lo
            r_hi[s] = wk * hi if r_hi[s] is None else r_hi[s] + wk * hi
    y = jnp.dot((g * _sigmoid(g) * u).astype(BF16), sd_ref[...], preferred_element_type=F32)
    routed = jnp.concatenate(r_lo + r_hi, axis=1)
    o_ref[...] = _layer_norm(DN_ALPHA * x1 + (routed + y), g_ref[...], b_ref[...])

    @pl.when(i == pl.num_programs(0) - 1)
    def _():
        wait_all(nxt)


def _final(dest, x1, wts_t, yb, w_sg, w_su, w_sd, ln_g, ln_b, *, tm, name):
    n_tiles = x1.shape[0] // tm

    def tile(i):
        return (i, 0)

    def const(i):
        return (0, 0)

    smem = functools.partial(pl.BlockSpec, memory_space=pltpu.SMEM)
    return pl.pallas_call(
        _final_kernel,
        grid=(n_tiles,),
        in_specs=[
            smem((1, TOP_K, tm), lambda i: (i, 0, 0)),
            smem((1, TOP_K, tm), lambda i: (jnp.minimum(i + 1, n_tiles - 1), 0, 0)),
            pl.BlockSpec((tm, D_MODEL), tile),
            pl.BlockSpec((tm, TOP_K), tile),
            pl.BlockSpec(memory_space=pl.ANY),
            _resident((D_MODEL, EXPERT_DIM), const),
            _resident((D_MODEL, EXPERT_DIM), const),
            _resident((EXPERT_DIM, D_MODEL), const),
            pl.BlockSpec((1, D_MODEL), const),
            pl.BlockSpec((1, D_MODEL), const),
        ],
        out_specs=pl.BlockSpec((tm, D_MODEL), tile),
        out_shape=jax.ShapeDtypeStruct((n_tiles * tm, D_MODEL), F32),
        scratch_shapes=[pltpu.VMEM((2, TOP_K, tm * ROW_TILE, PACKED_LANES), jnp.uint32),
                        pltpu.SemaphoreType.DMA((2,))],
        compiler_params=_cparams(("arbitrary",)),
        name=name,
    )(dest, dest, x1, wts_t, yb, w_sg, w_su, w_sd, ln_g, ln_b)


def kernel(x_prompt, x_sample, cache_win_k, cache_win_v, state_hgrn, cache_mem_k, cache_mem_v, mem_prompt,
           w_in, w_mem_kv, a_sinks, b_lb_logits, b_norm_g, w_branch, w_out, ln1_g, ln1_b,
           w_router, router_bias, w_exp_gate, w_exp_up, w_exp_down, w_sh_gate, w_sh_up, w_sh_down, ln2_g, ln2_b):
    assert w_in.shape[0] == DEPTH == 1
    batch, seq, _ = x_prompt.shape
    dbatch, dseq, _ = x_sample.shape
    n_p = batch * seq
    n_s = dbatch * dseq
    n = n_p + n_s
    l = 0

    xp2 = x_prompt.reshape(n_p, D_MODEL)
    xs2 = x_sample.reshape(n_s, D_MODEL)
    win = w_in[l]
    o_k, o_v, o_b = A_WIDTH, A_WIDTH + A_KV_WIDTH, A_WIDTH + 2 * A_KV_WIDTH
    o_c = o_b + 4 * B_WIDTH
    o_g = o_c + C_WIDTH
    groups = [
        ("aq", 0, o_k, 1024, BF16, A_HEAD_DIM ** -0.5),
        ("akv", o_k, o_b, 2 * A_KV_WIDTH, F32, 1.0),
        ("hgrn", o_b, o_c, 1024, F32, 1.0),
        ("cq", o_c, o_g, 1024, BF16, C_HEAD_DIM ** -0.5),
        ("gate", o_g, win.shape[1], 1024, F32, 1.0),
    ]
    zp, zs = {}, {}
    for gname, c0, c1, tn, dt, scale in groups:
        wslice = win[:, c0:c1].astype(BF16)
        zp[gname] = _matmul(xp2, wslice, tm=1024, tn=tn, out_dtype=dt, scale=scale, name=f"proj_{gname}_p")
        zs[gname] = _matmul(xs2, wslice, tm=n_s, tn=tn, out_dtype=dt, scale=scale, name=f"proj_{gname}_s")

    a_p = _swa_prompt(zp["aq"], zp["akv"], a_sinks[l], batch=batch, seq=seq)
    lc = cache_win_k.shape[2]
    kc = cache_win_k[l].reshape(dbatch, lc, A_KV_WIDTH)
    vc = cache_win_v[l].reshape(dbatch, lc, A_KV_WIDTH)
    a_s = _swa_sample(zs["aq"], zs["akv"], kc, vc, a_sinks[l], row0=0, batch=dbatch, seq=dseq)

    lower = jnp.cumsum(jax.nn.softmax(b_lb_logits.astype(F32), axis=0), axis=0)[l]
    lbp = jnp.stack([jnp.log(lower), jnp.log1p(-lower), 1.0 - lower])
    ng = jnp.tile(b_norm_g[l].astype(F32), B_HEADS).reshape(1, B_WIDTH)
    s_zero = jnp.zeros((batch, B_HEADS, B_KEY_DIM, B_VAL_DIM), F32)
    b_p, hs_p = _hgrn(zp["hgrn"], lbp, ng, s_zero, batch=batch, seq=seq, blk=CHUNK, nbat=2 if batch % 2 == 0 else 1)
    b_s, hs_s = _hgrn(zs["hgrn"], lbp, ng, state_hgrn[l].astype(F32), batch=dbatch, seq=dseq, blk=dseq,
                      nbat=2 if dbatch % 2 == 0 else 1)

    mem = mem_prompt.reshape(batch * MEM_LEN, D_MODEL)
    wmem = w_mem_kv[l]
    mk = _matmul(mem, wmem[:, :C_WIDTH].astype(BF16), tm=batch * MEM_LEN, tn=512, out_dtype=F32, name="proj_mem_k")
    mv = _matmul(mem, wmem[:, C_WIDTH:].astype(BF16), tm=batch * MEM_LEN, tn=512, out_dtype=F32, name="proj_mem_v")
    mk = mk.reshape(batch, MEM_LEN, C_WIDTH)
    mv = mv.reshape(batch, MEM_LEN, C_WIDTH)
    c_p = _cattn(zp["cq"], mk, mv, row0=0, batch=batch, seq=seq, tq=512)
    c_s = _cattn(zs["cq"], cache_mem_k[l].reshape(dbatch, MEM_LEN, C_WIDTH),
                 cache_mem_v[l].reshape(dbatch, MEM_LEN, C_WIDTH), row0=0, batch=dbatch, seq=dseq, tq=dseq)

    w_br = w_branch[l].astype(BF16)
    route_w = (w_out[l].astype(BF16), ln1_g[l].reshape(1, D_MODEL), ln1_b[l].reshape(1, D_MODEL),
               w_router[l].T.astype(BF16), router_bias[l].reshape(N_EXPERTS, 1).astype(F32))
    segs = []
    cnt = jnp.zeros((N_EXPERTS, 128), jnp.int32)
    for tag, x2, abc, gate, tm_r in (("p", xp2, (a_p, b_p, c_p), zp["gate"], 512), ("s", xs2, (a_s, b_s, c_s), zs["gate"], n_s)):
        h = _merge(*abc, gate, w_br, tm=tm_r, name=f"merge_{tag}")
        x1, x1p, idx, wts, pos, cnt = _out_ln_route(x2, h, *route_w, cnt, tm=tm_r, name=f"out_ln_route_{tag}")
        segs.append(dict(tag=tag, x1=x1, x1p=x1p, idx=idx, wts=wts, pos=pos, tm=tm_r, rows=x2.shape[0]))

    m = MOE_BLOCK
    n_pairs = n * TOP_K
    n_blocks = (n_pairs + m - 1) // m + N_EXPERTS
    counts = cnt[:, 0]
    padded = (counts + m - 1) // m * m
    pad_end = jnp.cumsum(padded)
    pad_start = (pad_end - padded).astype(jnp.int32)
    n_valid = pad_end[-1:] // m
    e_ids = jnp.arange(N_EXPERTS, dtype=jnp.int32)
    blk_first = jnp.arange(n_blocks, dtype=jnp.int32) * m
    blk_e = jnp.sum((blk_first[:, None] >= pad_end[None, :]).astype(jnp.int32), axis=1)
    blk_e = jnp.minimum(blk_e, N_EXPERTS - 1)
    z_pad = jnp.where(padded > counts, pad_end - m, -1)
    z_tail = jnp.where(blk_first >= pad_end[-1], blk_first, -1)
    zrows = jnp.concatenate([z_pad, z_tail]).astype(jnp.int32)

    def retile(dest, rows, tm_from, tm_to):
        d = dest.reshape(rows // tm_from, TOP_K, tm_from // tm_to, tm_to)
        return jnp.transpose(d, (0, 2, 1, 3)).reshape(rows // tm_to, TOP_K, tm_to)

    tm_d = 256
    for seg in segs:
        first_row = jnp.sum(jnp.where(seg["idx"][..., None] == e_ids, pad_start, 0), axis=-1)
        seg["dest"] = first_row + seg["pos"]
    dest_d = jnp.concatenate([retile(seg["dest"], seg["rows"], seg["tm"], tm_d) for seg in segs], axis=0)
    xs = _dispatch(zrows, dest_d, segs[0]["x1p"], segs[1]["x1p"], n_rows=n_blocks * m, tm=tm_d, block_rows=m)
    yb = _moe(xs, blk_e, n_valid.astype(jnp.int32), w_exp_gate[l].astype(F32), w_exp_up[l].astype(F32),
              w_exp_down[l].astype(F32))
    tm_f = 256
    shared_w = (w_sh_gate[l].astype(BF16), w_sh_up[l].astype(BF16), w_sh_down[l].astype(BF16),
                ln2_g[l].reshape(1, D_MODEL), ln2_b[l].reshape(1, D_MODEL))
    ys = []
    for seg in segs:
        rows = seg["rows"]
        wts_t = jnp.transpose(seg["wts"], (0, 2, 1)).reshape(rows, TOP_K)
        ys.append(_final(retile(seg["dest"], rows, seg["tm"], tm_f), seg["x1"], wts_t, yb, *shared_w, tm=tm_f,
                         name=f"combine_shared_ln2_{seg['tag']}"))
    y_p = ys[0].reshape(batch, seq, D_MODEL)
    y_s = ys[1].reshape(dbatch, dseq, D_MODEL)

    kv_p = zp["akv"].reshape(batch, seq, 2, A_KV_HEADS, A_HEAD_DIM)[:, -lc:]
    k_p, v_p = kv_p[:, :, 0], kv_p[:, :, 1]
    kv_s = zs["akv"].reshape(dbatch, dseq, 2, A_KV_HEADS, A_HEAD_DIM)
    k_s, v_s = kv_s[:, :, 0], kv_s[:, :, 1]
    wk_s = jnp.concatenate([cache_win_k[l].astype(F32), k_s], axis=1)[:, -lc:]
    wv_s = jnp.concatenate([cache_win_v[l].astype(F32), v_s], axis=1)[:, -lc:]
    mk_o = mk.reshape(batch, MEM_LEN, C_HEADS, C_HEAD_DIM)
    mv_o = mv.reshape(batch, MEM_LEN, C_HEADS, C_HEAD_DIM)
    return (y_p, y_s, k_p[None], v_p[None], hs_p[None], mk_o[None], mv_o[None], wk_s[None], wv_s[None], hs_s[None])
```

```python
import functools

import jax
import jax.numpy as jnp
import numpy as np
from jax import lax
from jax.experimental import pallas as pl
from jax.experimental.pallas import tpu as pltpu

F32 = jnp.float32
BF16 = jnp.bfloat16

D_MODEL = 2048
DEPTH = 1
PAST_LEN = 2048
CHUNK = 64
A_HEADS = 16
A_KV_HEADS = 4
A_GROUP = A_HEADS // A_KV_HEADS
A_HEAD_DIM = 64
A_WIDTH = A_HEADS * A_HEAD_DIM
A_KV_WIDTH = A_KV_HEADS * A_HEAD_DIM
WINDOW = 128
WIN_CHUNKS = WINDOW // CHUNK
B_HEADS = 8
B_KEY_DIM = 128
B_VAL_DIM = 128
B_WIDTH = B_HEADS * B_VAL_DIM
SUB = 16
MEM_LEN = 256
C_HEADS = 4
C_HEAD_DIM = 256
C_WIDTH = C_HEADS * C_HEAD_DIM
N_EXPERTS = 64
N_GROUPS = 8
GROUP_SIZE = N_EXPERTS // N_GROUPS
TOPK_GROUPS = 4
TOP_K = 8
EXPERT_DIM = 512
ROUTED_SCALE = 2.5
DN_ALPHA = (2 * DEPTH) ** 0.25
LN_EPS = 1e-5
RMS_EPS = 1e-6

MOE_BLOCK = 512
VMEM_LIMIT = 56 * 1024 * 1024


def _cparams(sem):
    return pltpu.CompilerParams(dimension_semantics=sem, vmem_limit_bytes=VMEM_LIMIT)


def _resident(shape, index_map):
    return pl.BlockSpec(shape, index_map, pipeline_mode=pl.Buffered(1))


def _sigmoid(x):
    return 1.0 / (1.0 + jnp.exp(-x))


def _layer_norm(x, g, b):
    mu = jnp.mean(x, axis=-1, keepdims=True)
    xc = x - mu
    var = jnp.mean(xc * xc, axis=-1, keepdims=True)
    return xc * lax.rsqrt(var + LN_EPS) * g + b


def _mm_kernel(x_ref, w_ref, o_ref, *, scale):
    acc = jnp.dot(x_ref[...].astype(BF16), w_ref[...], preferred_element_type=F32)
    if scale != 1.0:
        acc = acc * scale
    o_ref[...] = acc.astype(o_ref.dtype)


def _matmul(x, w, *, tm, tn, out_dtype, scale=1.0, name):
    m, k = x.shape
    n = w.shape[1]
    return pl.pallas_call(
        functools.partial(_mm_kernel, scale=scale),
        grid=(m // tm, n // tn),
        in_specs=[pl.BlockSpec((tm, k), lambda i, j: (i, 0)), pl.BlockSpec((k, tn), lambda i, j: (0, j))],
        out_specs=pl.BlockSpec((tm, tn), lambda i, j: (i, j)),
        out_shape=jax.ShapeDtypeStruct((m, n), out_dtype),
        compiler_params=_cparams(("parallel", "parallel")),
        name=name,
    )(x, w)


SWA_KEYS = 256
HEAD_PAIR = 2 * A_HEAD_DIM


def _kv_head_planes(x):
    lane = lax.broadcasted_iota(jnp.int32, (1, HEAD_PAIR), 1)
    low = lane < A_HEAD_DIM
    planes = []
    for pair in range(A_KV_HEADS // 2):
        own = x[:, pair * HEAD_PAIR:(pair + 1) * HEAD_PAIR]
        swapped = pltpu.roll(own, A_HEAD_DIM, 1)
        planes.append((jnp.where(low, own, 0.0).astype(BF16), jnp.where(low, 0.0, swapped).astype(BF16)))
        planes.append((jnp.where(low, swapped, 0.0).astype(BF16), jnp.where(low, 0.0, own).astype(BF16)))
    return planes


SWA_LOOKAHEAD = 8


def _swa_chunks(q_ref, rows, chunk_rows, k_planes, v_planes, n_band, bias_ref, valid_of, o_ref):
    zeros_k = jnp.zeros((SWA_KEYS - n_band, HEAD_PAIR), BF16)
    ones_v = jnp.ones((SWA_KEYS, HEAD_PAIR), BF16)
    tiles = [(c, h, x) for c in range(len(chunk_rows)) for h in range(A_KV_HEADS) for x in range(2)]

    def band(plane, c):
        return jnp.concatenate([plane[chunk_rows[c]:chunk_rows[c] + n_band], zeros_k], axis=0)

    def scores(t):
        c, h, x = tiles[t]
        q2 = jnp.concatenate([q_ref[c * rows:(c + 1) * rows, (2 * h + j) * HEAD_PAIR:(2 * h + j + 1) * HEAD_PAIR]
                              for j in range(2)], axis=0).astype(BF16)
        s = lax.dot_general(q2, band(k_planes[h][x], c), (((1,), (1,)), ((), ())), preferred_element_type=F32)
        s = s - bias_ref[h, x]
        valid = valid_of(c)
        return s if valid is None else jnp.where(valid, s, -jnp.inf)

    pending = {t: scores(t) for t in range(min(SWA_LOOKAHEAD, len(tiles)))}
    even = None
    for t, (c, h, x) in enumerate(tiles):
        s = pending.pop(t)
        p = jnp.exp(s - jnp.max(s, axis=-1, keepdims=True)).astype(BF16)
        if t + SWA_LOOKAHEAD < len(tiles):
            pending[t + SWA_LOOKAHEAD] = scores(t + SWA_LOOKAHEAD)
        vw = jnp.concatenate([band(v_planes[h][x], c), ones_v], axis=1)
        o = jnp.dot(p, vw, preferred_element_type=F32)
        o = o[:, :HEAD_PAIR] / o[:, HEAD_PAIR:]
        if x == 0:
            even = o
        else:
            out = even + o
            for j in range(2):
                o_ref[c * rows:(c + 1) * rows, (2 * h + j) * HEAD_PAIR:(2 * h + j + 1) * HEAD_PAIR] = (
                    out[j * rows:(j + 1) * rows].astype(o_ref.dtype))


def _swa_prompt_kernel(q_ref, kc_ref, kp_ref, vc_ref, vp_ref, bias_ref, o_ref, *, n_chunks):
    i = pl.program_id(1)
    pad = WIN_CHUNKS * CHUNK
    n_band = pad + CHUNK
    k_planes = _kv_head_planes(jnp.concatenate([kp_ref[...], kc_ref[...]], axis=0))
    v_planes = _kv_head_planes(jnp.concatenate([vp_ref[...], vc_ref[...]], axis=0))
    s_idx = lax.broadcasted_iota(jnp.int32, (1, SWA_KEYS), 1)

    def valid_of(c):
        return jnp.logical_or(s_idx + (i * (n_chunks * CHUNK) + c * CHUNK - pad) >= 0, s_idx >= n_band)

    _swa_chunks(q_ref, CHUNK, [c * CHUNK for c in range(n_chunks)], k_planes, v_planes, n_band, bias_ref, valid_of,
                o_ref)


def _swa_sample_kernel(q_ref, kn_ref, kc_ref, vn_ref, vc_ref, bias_ref, o_ref):
    k_planes = _kv_head_planes(jnp.concatenate([kc_ref[0], kn_ref[...]], axis=0))
    v_planes = _kv_head_planes(jnp.concatenate([vc_ref[0], vn_ref[...]], axis=0))
    n_band = kc_ref.shape[1] + kn_ref.shape[0]
    _swa_chunks(q_ref, q_ref.shape[0], [0], k_planes, v_planes, n_band, bias_ref, lambda c: None, o_ref)


def _alibi_slopes():
    return (2.0 ** (-8.0 * np.arange(1, A_HEADS + 1) / A_HEADS)).astype(np.float32)


def _swa_bias(q_pos, k_pos, valid, sinks):
    n_q, n_k = len(q_pos), len(k_pos)
    assert n_k < SWA_KEYS
    dist = np.abs(q_pos[:, None] - k_pos[None, :]).astype(np.float32)
    band = _alibi_slopes()[:, None, None] * dist[None]
    if valid is not None:
        band = np.where(valid[None], band, np.inf)
    pad = np.full((A_HEADS, n_q, SWA_KEYS - n_k - 1), np.inf, np.float32)
    sink = jnp.broadcast_to(-sinks.astype(F32)[:, None, None], (A_HEADS, n_q, 1))
    bias = jnp.concatenate([jnp.asarray(band.astype(np.float32)), sink, jnp.asarray(pad)], axis=2)
    bias = bias.reshape(A_KV_HEADS, 2, 2, n_q, SWA_KEYS).transpose(0, 2, 1, 3, 4)
    return bias.reshape(A_KV_HEADS, 2, 2 * n_q, SWA_KEYS)


def _swa_prompt(q, kv, sinks, *, batch, seq, n_chunks=4):
    tq = n_chunks * CHUNK
    pad = WIN_CHUNKS * CHUNK
    nb = seq // tq
    bias = _swa_bias(pad + np.arange(CHUNK), np.arange(pad + CHUNK), None, sinks)
    prev_per_blk = tq // pad

    def cur(b, i, col=0):
        return (b * nb + i, col)

    def prev(b, i, col=0):
        return (jnp.maximum((b * nb + i) * prev_per_blk - 1, b * nb * prev_per_blk), col)

    return pl.pallas_call(
        functools.partial(_swa_prompt_kernel, n_chunks=n_chunks),
        grid=(batch, nb),
        in_specs=[
            pl.BlockSpec((tq, A_WIDTH), cur),
            pl.BlockSpec((tq, A_KV_WIDTH), cur),
            pl.BlockSpec((pad, A_KV_WIDTH), prev),
            pl.BlockSpec((tq, A_KV_WIDTH), functools.partial(cur, col=1)),
            pl.BlockSpec((pad, A_KV_WIDTH), functools.partial(prev, col=1)),
            pl.BlockSpec(bias.shape, lambda b, i: (0, 0, 0, 0)),
        ],
        out_specs=pl.BlockSpec((tq, A_WIDTH), cur),
        out_shape=jax.ShapeDtypeStruct((batch * seq, A_WIDTH), BF16),
        compiler_params=_cparams(("parallel", "arbitrary")),
        name="swa_prompt",
    )(q, kv, kv, kv, kv, bias)


def _swa_sample(q, kv, k_cache, v_cache, sinks, *, row0, batch, seq):
    lc = k_cache.shape[1]
    q_pos = PAST_LEN + np.arange(seq)
    k_pos = PAST_LEN - lc + np.arange(lc + seq)
    cdiff = q_pos[:, None] // CHUNK - k_pos[None, :] // CHUNK
    valid = (cdiff >= 0) & (cdiff <= WIN_CHUNKS)
    bias = _swa_bias(q_pos, k_pos, valid, sinks)
    blk0 = row0 // seq

    def rows(b):
        return (blk0 + b, 0)

    return pl.pallas_call(
        _swa_sample_kernel,
        grid=(batch,),
        in_specs=[
            pl.BlockSpec((seq, A_WIDTH), rows),
            pl.BlockSpec((seq, A_KV_WIDTH), rows),
            pl.BlockSpec((1, lc, A_KV_WIDTH), lambda b: (b, 0, 0)),
            pl.BlockSpec((seq, A_KV_WIDTH), lambda b: (blk0 + b, 1)),
            pl.BlockSpec((1, lc, A_KV_WIDTH), lambda b: (b, 0, 0)),
            pl.BlockSpec(bias.shape, lambda b: (0, 0, 0, 0)),
        ],
        out_specs=pl.BlockSpec((seq, A_WIDTH), lambda b: (b, 0)),
        out_shape=jax.ShapeDtypeStruct((batch * seq, A_WIDTH), BF16),
        compiler_params=_cparams(("parallel",)),
        name="swa_sample",
    )(q, kv, k_cache, kv, v_cache, bias)


def _split3(x):
    hi = x.astype(BF16)
    r1 = x - hi.astype(F32)
    mid = r1.astype(BF16)
    lo = (r1 - mid.astype(F32)).astype(BF16)
    return hi, mid, lo


def _dot3(mat, parts):
    acc = jnp.dot(mat, parts[0], preferred_element_type=F32)
    acc = acc + jnp.dot(mat, parts[1], preferred_element_type=F32)
    return acc + jnp.dot(mat, parts[2], preferred_element_type=F32)


def _hgrn_kernel(zq_ref, zf_ref, zi_ref, zg_ref, lb_ref, ng_ref, tri_ref, tsel_ref, s0_ref, o_ref, sfin_ref, s_scr,
                 *, blk, nbat):
    j = pl.program_id(1)

    @pl.when(j == 0)
    def _():
        s_scr[...] = s0_ref[...]

    log_lb = lb_ref[0:1, :]
    log1m_lb = lb_ref[1:2, :]
    one_m_lb = lb_ref[2:3, :]
    row = lax.broadcasted_iota(jnp.int32, (blk, 1), 0)
    n_sub = blk // SUB
    tril = lax.broadcasted_iota(jnp.int32, (blk, blk), 0) >= lax.broadcasted_iota(jnp.int32, (blk, blk), 1)
    heads = [slice(h * B_KEY_DIM, (h + 1) * B_KEY_DIM) for h in range(B_HEADS)]

    def prepare(bi):
        bq = zq_ref[bi]
        fl = zf_ref[bi]
        q = bq * _sigmoid(bq) * (B_KEY_DIM ** -0.5)
        log_sig = jnp.minimum(fl, 0.0) - jnp.log1p(jnp.exp(-jnp.abs(fl)))
        c = log1m_lb + log_sig
        logf = jnp.maximum(log_lb, c) + jnp.log1p(jnp.exp(-jnp.abs(log_lb - c)))
        k = one_m_lb * _sigmoid(-fl)
        parts = _split3(logf)
        b = _dot3(tri_ref[...], parts)
        rq = _dot3(tsel_ref[...], parts)
        qt = (q * jnp.exp(b - rq)).astype(BF16)
        qb = (q * jnp.exp(b)).astype(BF16)
        b_last = b[blk - 1:blk, :]
        khat = (k * jnp.exp(b_last - b)).astype(BF16)
        vb = zi_ref[bi].astype(BF16)
        kts = []
        for i in range(n_sub):
            r_i = rq[i * SUB:i * SUB + 1, :]
            kts.append(jnp.where(row < (i + 1) * SUB, k * jnp.exp(r_i - b), 0.0).astype(BF16))
        s_old = [s_scr[bi, h] for h in range(B_HEADS)]
        a_raw, o_state, ds = [], [], []
        for h, hs in enumerate(heads):
            a_raw.append(jnp.concatenate(
                [lax.dot_general(qt[i * SUB:(i + 1) * SUB, hs], kts[i][:, hs], (((1,), (1,)), ((), ())),
                                 preferred_element_type=F32) for i in range(n_sub)], axis=0))
            o_state.append(jnp.dot(qb[:, hs], s_old[h].astype(BF16), preferred_element_type=F32))
            ds.append(lax.dot_general(khat[:, hs], vb[:, hs], (((0,), (0,)), ((), ())), preferred_element_type=F32))
        return dict(a_raw=a_raw, o_state=o_state, ds=ds, s_old=s_old, vb=vb, e_last=jnp.exp(b_last))

    def finish(bi, p):
        outs = []
        for h, hs in enumerate(heads):
            a = jnp.where(tril, p["a_raw"][h], 0.0).astype(BF16)
            o = jnp.dot(a, p["vb"][:, hs], preferred_element_type=F32) + p["o_state"][h]
            decay = jnp.transpose(jnp.broadcast_to(p["e_last"][:, hs], (B_KEY_DIM, B_KEY_DIM)))
            s_scr[bi, h] = decay * p["s_old"][h] + p["ds"][h]
            outs.append(o * lax.rsqrt(jnp.mean(o * o, axis=-1, keepdims=True) + RMS_EPS))
        bg = zg_ref[bi]
        o_all = jnp.concatenate(outs, axis=1) * ng_ref[...] * (bg * _sigmoid(bg))
        o_ref[bi] = o_all.astype(o_ref.dtype)

    prepared = [prepare(bi) for bi in range(nbat)]
    for bi in range(nbat):
        finish(bi, prepared[bi])

    @pl.when(j == pl.num_programs(1) - 1)
    def _():
        sfin_ref[...] = s_scr[...]


def _hgrn(zb, lbp, ng, s0, *, batch, seq, blk, nbat):
    nb = seq // blk
    t = np.arange(blk)
    tri = jnp.asarray((t[:, None] >= t[None, :]).astype(np.float32), dtype=BF16)
    tsel = jnp.asarray((t[None, :] < (t[:, None] // SUB) * SUB).astype(np.float32), dtype=BF16)
    z3 = zb.reshape(batch, seq, 4 * B_WIDTH)

    def zspec(col):
        return pl.BlockSpec((nbat, blk, B_WIDTH), lambda b, j: (b, j, col))

    state_spec = pl.BlockSpec((nbat, B_HEADS, B_KEY_DIM, B_VAL_DIM), lambda b, j: (b, 0, 0, 0))
    out, s_fin = pl.pallas_call(
        functools.partial(_hgrn_kernel, blk=blk, nbat=nbat),
        grid=(batch // nbat, nb),
        in_specs=[
            zspec(0), zspec(1), zspec(2), zspec(3),
            pl.BlockSpec((3, B_WIDTH), lambda b, j: (0, 0)),
            pl.BlockSpec((1, B_WIDTH), lambda b, j: (0, 0)),
            pl.BlockSpec((blk, blk), lambda b, j: (0, 0)),
            pl.BlockSpec((blk, blk), lambda b, j: (0, 0)),
            state_spec,
        ],
        out_specs=[pl.BlockSpec((nbat, blk, B_WIDTH), lambda b, j: (b, j, 0)), state_spec],
        out_shape=[jax.ShapeDtypeStruct((batch, seq, B_WIDTH), BF16),
                   jax.ShapeDtypeStruct((batch, B_HEADS, B_KEY_DIM, B_VAL_DIM), F32)],
        scratch_shapes=[pltpu.VMEM((nbat, B_HEADS, B_KEY_DIM, B_VAL_DIM), F32)],
        compiler_params=_cparams(("parallel", "arbitrary")),
        name=f"hgrn_blk{blk}",
    )(z3, z3, z3, z3, lbp, ng, tri, tsel, s0)
    return out.reshape(batch * seq, B_WIDTH), s_fin


def _cattn_kernel(q_ref, mk_ref, mv_ref, o_ref):
    mk = mk_ref[0].astype(BF16)
    mv = mv_ref[0].astype(BF16)
    q = q_ref[...]
    heads = [slice(h * C_HEAD_DIM, (h + 1) * C_HEAD_DIM) for h in range(C_HEADS)]
    scores = [lax.dot_general(q[:, hs], mk[:, hs], (((1,), (1,)), ((), ())), preferred_element_type=F32)
              for hs in heads]
    for hs, s in zip(heads, scores):
        p = jnp.exp(s - jnp.max(s, axis=-1, keepdims=True))
        p = p / jnp.sum(p, axis=-1, keepdims=True)
        o_ref[:, hs] = jnp.dot(p.astype(BF16), mv[:, hs], preferred_element_type=F32).astype(o_ref.dtype)


def _cattn(q, mk, mv, *, row0, batch, seq, tq):
    nb = seq // tq
    blk0 = row0 // tq
    mem_spec = pl.BlockSpec((1, MEM_LEN, C_WIDTH), lambda b, i: (b, 0, 0))
    return pl.pallas_call(
        _cattn_kernel,
        grid=(batch, nb),
        in_specs=[pl.BlockSpec((tq, C_WIDTH), lambda b, i: (blk0 + b * nb + i, 0)), mem_spec, mem_spec],
        out_specs=pl.BlockSpec((tq, C_WIDTH), lambda b, i: (b * nb + i, 0)),
        out_shape=jax.ShapeDtypeStruct((batch * seq, C_WIDTH), BF16),
        compiler_params=_cparams(("parallel", "parallel")),
        name=f"cattn_tq{tq}",
    )(q, mk, mv)


def _merge_kernel(a_ref, b_ref, c_ref, g0_ref, g1_ref, g2_ref, pa_ref, pb_ref, pc_ref, o_ref):
    da = jnp.dot(a_ref[...], pa_ref[...], preferred_element_type=F32)
    db = jnp.dot(b_ref[...], pb_ref[...], preferred_element_type=F32)
    dc = jnp.dot(c_ref[...], pc_ref[...], preferred_element_type=F32)
    h = _sigmoid(g0_ref[...]) * da + _sigmoid(g1_ref[...]) * db + _sigmoid(g2_ref[...]) * dc
    o_ref[...] = h.astype(o_ref.dtype)


def _merge(a, b, c, gl, p, *, tm, name):
    n, width = a.shape

    def rows(i):
        return (i, 0)

    return pl.pallas_call(
        _merge_kernel,
        grid=(n // tm,),
        in_specs=[
            pl.BlockSpec((tm, width), rows), pl.BlockSpec((tm, width), rows), pl.BlockSpec((tm, width), rows),
            pl.BlockSpec((tm, D_MODEL), lambda i: (i, 0)),
            pl.BlockSpec((tm, D_MODEL), lambda i: (i, 1)),
            pl.BlockSpec((tm, D_MODEL), lambda i: (i, 2)),
            _resident((width, D_MODEL), lambda i: (0, 0)),
            _resident((width, D_MODEL), lambda i: (1, 0)),
            _resident((width, D_MODEL), lambda i: (2, 0)),
        ],
        out_specs=pl.BlockSpec((tm, D_MODEL), rows),
        out_shape=jax.ShapeDtypeStruct((n, D_MODEL), BF16),
        compiler_params=_cparams(("parallel",)),
        name=name,
    )(a, b, c, gl, gl, gl, p, p, p)


def _first_index(hit_src, m, iota, size, axis):
    return jnp.min(jnp.where(hit_src == m, iota, size), axis=axis, keepdims=True)


def _route(x1, wr_ref, rb_ref, upper_ref, run_ref, idx_ref, wts_ref, pos_ref):
    tm = x1.shape[0]
    logits = lax.dot_general(wr_ref[...], x1.astype(BF16), (((1,), (1,)), ((), ())),
                             preferred_element_type=F32)
    scores = _sigmoid(logits)
    choice = scores + rb_ref[...]
    g3 = choice.reshape(N_GROUPS, GROUP_SIZE, tm)
    mem_iota = lax.broadcasted_iota(jnp.int32, g3.shape, 1)
    m1 = jnp.max(g3, axis=1, keepdims=True)
    first = _first_index(g3, m1, mem_iota, GROUP_SIZE, 1)
    m2 = jnp.max(jnp.where(mem_iota == first, -jnp.inf, g3), axis=1, keepdims=True)
    gscore = (m1 + m2).reshape(N_GROUPS, tm)
    g_iota = lax.broadcasted_iota(jnp.int32, gscore.shape, 0)
    gsel = jnp.zeros(gscore.shape, F32)
    cur = gscore
    for _ in range(TOPK_GROUPS):
        m = jnp.max(cur, axis=0, keepdims=True)
        hit = g_iota == _first_index(cur, m, g_iota, N_GROUPS, 0)
        gsel = jnp.where(hit, 1.0, gsel)
        cur = jnp.where(hit, -jnp.inf, cur)
    masked = jnp.where(gsel.reshape(N_GROUPS, 1, tm) > 0.5, g3, -jnp.inf).reshape(N_EXPERTS, tm)
    e_iota = lax.broadcasted_iota(jnp.int32, masked.shape, 0)
    sel = jnp.zeros(masked.shape, F32)
    cur = masked
    idxs, ws = [], []
    for _ in range(TOP_K):
        m = jnp.max(cur, axis=0, keepdims=True)
        ei = _first_index(cur, m, e_iota, N_EXPERTS, 0)
        hit = e_iota == ei
        idxs.append(ei)
        ws.append(jnp.sum(jnp.where(hit, scores, 0.0), axis=0, keepdims=True))
        sel = jnp.where(hit, 1.0, sel)
        cur = jnp.where(hit, -jnp.inf, cur)
    w = jnp.concatenate(ws, axis=0)
    w = w / jnp.sum(w, axis=0, keepdims=True) * ROUTED_SCALE
    excl = jnp.dot(sel.astype(BF16), upper_ref[...], preferred_element_type=F32)
    posfull = excl + run_ref[:, 0:1]
    run_ref[...] = run_ref[...] + jnp.sum(sel, axis=1, keepdims=True)
    pos = [jnp.sum(jnp.where(e_iota == ei, posfull, 0.0), axis=0, keepdims=True) for ei in idxs]
    idx_ref[0] = jnp.concatenate(idxs, axis=0)
    wts_ref[0] = w
    pos_ref[0] = jnp.concatenate(pos, axis=0).astype(jnp.int32)


def _pack_halves(x):
    c = x.shape[1] // 2
    lo = lax.bitcast_convert_type(x[:, :c].astype(BF16).astype(F32), jnp.uint32)
    hi = lax.bitcast_convert_type(x[:, c:].astype(BF16).astype(F32), jnp.uint32)
    return (hi & jnp.uint32(0xFFFF0000)) | (lo >> 16)


def _unpack_halves(w):
    lo = lax.bitcast_convert_type(w << 16, F32)
    hi = lax.bitcast_convert_type(w & jnp.uint32(0xFFFF0000), F32)
    return lo, hi


ROW_TILE = 8
PACKED_LANES = D_MODEL // 2 // ROW_TILE


def _store_row_tiles(ref, val):
    rows = val.shape[0]
    for s in range(ROW_TILE):
        ref[pl.ds(s, rows, stride=ROW_TILE), :] = val[:, s * PACKED_LANES:(s + 1) * PACKED_LANES]


def _load_row_tiles(ref, rows):
    return [ref[pl.ds(s, rows, stride=ROW_TILE), :] for s in range(ROW_TILE)]


def _out_ln_route_kernel(x_ref, h_ref, wo_ref, g_ref, b_ref, wr_ref, rb_ref, upper_ref, run0_ref,
                         x1_ref, x1p_ref, idx_ref, wts_ref, pos_ref, cnt_ref, run_ref):
    @pl.when(pl.program_id(0) == 0)
    def _():
        run_ref[...] = run0_ref[...].astype(F32)

    y = DN_ALPHA * x_ref[...] + jnp.dot(h_ref[...], wo_ref[...], preferred_element_type=F32)
    x1 = _layer_norm(y, g_ref[...], b_ref[...])
    x1_ref[...] = x1
    _store_row_tiles(x1p_ref, _pack_halves(x1))
    _route(x1, wr_ref, rb_ref, upper_ref, run_ref, idx_ref, wts_ref, pos_ref)
    cnt_ref[...] = run_ref[...].astype(jnp.int32)


def _out_ln_route(x, h, w_out, ln_g, ln_b, w_router_t, router_bias, run0, *, tm, name):
    n = h.shape[0]
    nt = n // tm
    t = np.arange(tm)
    upper = jnp.asarray((t[:, None] < t[None, :]).astype(np.float32), dtype=BF16)

    def rows(i):
        return (i, 0)

    def const(i):
        return (0, 0)

    small = pl.BlockSpec((1, TOP_K, tm), lambda i: (i, 0, 0))
    return pl.pallas_call(
        _out_ln_route_kernel,
        grid=(nt,),
        in_specs=[
            pl.BlockSpec((tm, D_MODEL), rows),
            pl.BlockSpec((tm, D_MODEL), rows),
            _resident((D_MODEL, D_MODEL), const),
            pl.BlockSpec((1, D_MODEL), const),
            pl.BlockSpec((1, D_MODEL), const),
            pl.BlockSpec((N_EXPERTS, D_MODEL), const),
            pl.BlockSpec((N_EXPERTS, 1), const),
            pl.BlockSpec((tm, tm), const),
            pl.BlockSpec((N_EXPERTS, 128), const),
        ],
        out_specs=[pl.BlockSpec((tm, D_MODEL), rows), pl.BlockSpec((tm * ROW_TILE, PACKED_LANES), rows), small, small,
                   small, pl.BlockSpec((N_EXPERTS, 128), const)],
        out_shape=[
            jax.ShapeDtypeStruct((n, D_MODEL), F32),
            jax.ShapeDtypeStruct((n * ROW_TILE, PACKED_LANES), jnp.uint32),
            jax.ShapeDtypeStruct((nt, TOP_K, tm), jnp.int32),
            jax.ShapeDtypeStruct((nt, TOP_K, tm), F32),
            jax.ShapeDtypeStruct((nt, TOP_K, tm), jnp.int32),
            jax.ShapeDtypeStruct((N_EXPERTS, 128), jnp.int32),
        ],
        scratch_shapes=[pltpu.VMEM((N_EXPERTS, 128), F32)],
        compiler_params=_cparams(("arbitrary",)),
        name=name,
    )(x, h, w_out, ln_g, ln_b, w_router_t, router_bias, upper, run0)


def _row_copy(src_hbm, row, dst, dst_row, sem):
    return pltpu.make_async_copy(src_hbm.at[pl.ds(row, 1)], dst.at[pl.ds(dst_row, 1)], sem)


def _tile_copy(src, src_row, dst, dst_row, sem):
    def tile(row):
        first = row * ROW_TILE
        return pl.ds(first if isinstance(row, int) else pl.multiple_of(first, ROW_TILE), ROW_TILE)

    return pltpu.make_async_copy(src.at[tile(src_row)], dst.at[tile(dst_row)], sem)


def _dispatch_kernel(zrow_ref, dest_ref, xa_ref, xb_ref, o_hbm, zbuf, sem, zsem, *, block_rows, n_first):
    tm = xa_ref.shape[0] // ROW_TILE
    n_zero = zrow_ref.shape[0]

    def zero_copy(b):
        start = pl.multiple_of(jnp.maximum(zrow_ref[b], 0) * ROW_TILE, ROW_TILE)
        return pltpu.make_async_copy(zbuf, o_hbm.at[pl.ds(start, block_rows * ROW_TILE)], zsem)

    def zero_unowned_rows():
        zbuf[...] = jnp.zeros_like(zbuf)

        def start(b, carry):
            @pl.when(zrow_ref[b] >= 0)
            def _():
                zero_copy(b).start()
            return carry

        def wait(b, carry):
            @pl.when(zrow_ref[b] >= 0)
            def _():
                zero_copy(b).wait()
            return carry

        lax.fori_loop(0, n_zero, start, 0)
        lax.fori_loop(0, n_zero, wait, 0)

    pl.when(pl.program_id(0) == 0)(zero_unowned_rows)

    def scatter_tile(x_ref):
        for t in range(tm):
            for k in range(TOP_K):
                _tile_copy(x_ref, t, o_hbm, dest_ref[0, k, t], sem).start(priority=k % 2)
        for k in range(TOP_K):
            pltpu.make_async_copy(x_ref, o_hbm.at[pl.ds(0, tm * ROW_TILE)], sem).wait()

    pl.when(pl.program_id(0) < n_first)(functools.partial(scatter_tile, xa_ref))
    pl.when(pl.program_id(0) >= n_first)(functools.partial(scatter_tile, xb_ref))


def _dispatch(zrows, dest_blk, x1p_a, x1p_b, *, n_rows, tm, block_rows):
    nf = x1p_a.shape[0] // ROW_TILE // tm
    ns = x1p_b.shape[0] // ROW_TILE // tm
    grid_spec = pltpu.PrefetchScalarGridSpec(
        num_scalar_prefetch=1,
        grid=(nf + ns,),
        in_specs=[
            pl.BlockSpec((1, TOP_K, tm), lambda i, z: (i, 0, 0), memory_space=pltpu.SMEM),
            pl.BlockSpec((tm * ROW_TILE, PACKED_LANES), lambda i, z: (jnp.minimum(i, nf - 1), 0)),
            pl.BlockSpec((tm * ROW_TILE, PACKED_LANES), lambda i, z: (jnp.clip(i - nf, 0, ns - 1), 0)),
        ],
        out_specs=pl.BlockSpec(memory_space=pl.ANY),
        scratch_shapes=[pltpu.VMEM((block_rows * ROW_TILE, PACKED_LANES), jnp.uint32),
                        pltpu.SemaphoreType.DMA(()), pltpu.SemaphoreType.DMA(())],
    )
    return pl.pallas_call(
        functools.partial(_dispatch_kernel, block_rows=block_rows, n_first=nf),
        grid_spec=grid_spec,
        out_shape=jax.ShapeDtypeStruct((n_rows * ROW_TILE, PACKED_LANES), jnp.uint32),
        compiler_params=_cparams(("arbitrary",)),
        name="dispatch",
    )(zrows, dest_blk, x1p_a, x1p_b)


def _moe_kernel(be_ref, nv_ref, nr_ref, x_ref, wg_ref, wu_ref, wd_ref, o_ref, wg_b, wu_b, wd_b):
    k = pl.program_id(0)
    m = x_ref.shape[0] // ROW_TILE
    half = m // 2
    live = k < nv_ref[0]
    upper_half_used = nr_ref[k] > half
    new_expert = jnp.logical_or(k == 0, be_ref[k] != be_ref[jnp.maximum(k - 1, 0)])

    @pl.when(jnp.logical_and(live, new_expert))
    def _():
        wg_b[...] = wg_ref[0].astype(BF16)
        wu_b[...] = wu_ref[0].astype(BF16)
        wd_b[...] = wd_ref[0].astype(BF16)

    def swiglu_rows(rows):
        lo, hi = _unpack_halves(jnp.concatenate(_load_row_tiles(x_ref, rows), axis=1))
        x = jnp.concatenate([lo.astype(BF16), hi.astype(BF16)], axis=1)
        g = jnp.dot(x, wg_b[...], preferred_element_type=F32)
        u = jnp.dot(x, wu_b[...], preferred_element_type=F32)
        hmid = (g * _sigmoid(g) * u).astype(BF16)
        y = jnp.dot(hmid, wd_b[...], preferred_element_type=F32)
        _store_row_tiles(o_ref, _pack_halves(y))

    @pl.when(jnp.logical_and(live, upper_half_used))
    def _():
        swiglu_rows(m)

    @pl.when(jnp.logical_and(live, jnp.logical_not(upper_half_used)))
    def _():
        swiglu_rows(half)
        o_ref[half * ROW_TILE:, :] = jnp.zeros((half * ROW_TILE, o_ref.shape[1]), o_ref.dtype)

    @pl.when(jnp.logical_not(live))
    def _():
        o_ref[...] = jnp.zeros_like(o_ref)


def _moe(xs, blk_e, n_valid, blk_rows, w_gate, w_up, w_down):
    n_blocks = blk_e.shape[0]
    m = MOE_BLOCK

    def rows(k, be, nv, nr):
        return (jnp.minimum(k, nv[0] - 1), 0)

    def wspec(shape):
        return pl.BlockSpec(shape, lambda k, be, nv, nr: (be[k], 0, 0))

    grid_spec = pltpu.PrefetchScalarGridSpec(
        num_scalar_prefetch=3,
        grid=(n_blocks,),
        in_specs=[
            pl.BlockSpec((m * ROW_TILE, PACKED_LANES), rows),
            wspec((1, D_MODEL, EXPERT_DIM)), wspec((1, D_MODEL, EXPERT_DIM)), wspec((1, EXPERT_DIM, D_MODEL)),
        ],
        out_specs=pl.BlockSpec((m * ROW_TILE, PACKED_LANES), lambda k, be, nv, nr: (k, 0)),
        scratch_shapes=[pltpu.VMEM((D_MODEL, EXPERT_DIM), BF16), pltpu.VMEM((D_MODEL, EXPERT_DIM), BF16),
                        pltpu.VMEM((EXPERT_DIM, D_MODEL), BF16)],
    )
    return pl.pallas_call(
        _moe_kernel,
        grid_spec=grid_spec,
        out_shape=jax.ShapeDtypeStruct(xs.shape, jnp.uint32),
        compiler_params=_cparams(("arbitrary",)),
        name="moe_experts",
    )(blk_e, n_valid, blk_rows, xs, w_gate, w_up, w_down)


def _final_kernel(dest_ref, dest_next_ref, x1_ref, w_ref, yb_hbm, sg_ref, su_ref, sd_ref, g_ref, b_ref, o_ref,
                  ybuf, sem):
    i = pl.program_id(0)
    tm = x1_ref.shape[0]

    def start(d_ref, slot, t):
        for k in range(TOP_K):
            _tile_copy(yb_hbm, d_ref[0, k, t], ybuf.at[slot, k], t, sem.at[slot]).start(priority=k % 2)

    def wait_all(slot):
        for k in range(TOP_K):
            pltpu.make_async_copy(yb_hbm.at[pl.ds(0, tm * ROW_TILE)], ybuf.at[slot, k], sem.at[slot]).wait()

    @pl.when(i == 0)
    def _():
        def issue(t, carry):
            start(dest_ref, 0, t)
            return carry

        lax.fori_loop(0, tm, issue, 0, unroll=4)

    cur = i % 2
    nxt = 1 - cur
    wait_all(cur)
    x1 = x1_ref[...]
    xb = x1.astype(BF16)
    g = jnp.dot(xb, sg_ref[...], preferred_element_type=F32)
    u = jnp.dot(xb, su_ref[...], preferred_element_type=F32)
    w = w_ref[...]
    per = tm // TOP_K
    r_lo = [None] * ROW_TILE
    r_hi = [None] * ROW_TILE
    for k in range(TOP_K):
        for t in range(k * per, (k + 1) * per):
            start(dest_next_ref, nxt, t)
        wk = jnp.broadcast_to(w[:, k:k + 1], (tm, PACKED_LANES))
        for s, piece in enumerate(_load_row_tiles(ybuf.at[cur, k], tm)):
            lo, hi = _unpack_halves(piece)
            r_lo[s] = wk * lo if r_lo[s] is None else r_lo[s] + wk * lo
            r_hi[s] = wk * hi if r_hi[s] is None else r_hi[s] + wk * hi
    y = jnp.dot((g * _sigmoid(g) * u).astype(BF16), sd_ref[...], preferred_element_type=F32)
    routed = jnp.concatenate(r_lo + r_hi, axis=1)
    o_ref[...] = _layer_norm(DN_ALPHA * x1 + (routed + y), g_ref[...], b_ref[...])

    @pl.when(i == pl.num_programs(0) - 1)
    def _():
        wait_all(nxt)


def _final(dest, x1, wts_t, yb, w_sg, w_su, w_sd, ln_g, ln_b, *, tm, name):
    n_tiles = x1.shape[0] // tm

    def tile(i):
        return (i, 0)

    def const(i):
        return (0, 0)

    smem = functools.partial(pl.BlockSpec, memory_space=pltpu.SMEM)
    return pl.pallas_call(
        _final_kernel,
        grid=(n_tiles,),
        in_specs=[
            smem((1, TOP_K, tm), lambda i: (i, 0, 0)),
            smem((1, TOP_K, tm), lambda i: (jnp.minimum(i + 1, n_tiles - 1), 0, 0)),
            pl.BlockSpec((tm, D_MODEL), tile),
            pl.BlockSpec((tm, TOP_K), tile),
            pl.BlockSpec(memory_space=pl.ANY),
            _resident((D_MODEL, EXPERT_DIM), const),
            _resident((D_MODEL, EXPERT_DIM), const),
            _resident((EXPERT_DIM, D_MODEL), const),
            pl.BlockSpec((1, D_MODEL), const),
            pl.BlockSpec((1, D_MODEL), const),
        ],
        out_specs=pl.BlockSpec((tm, D_MODEL), tile),
        out_shape=jax.ShapeDtypeStruct((n_tiles * tm, D_MODEL), F32),
        scratch_shapes=[pltpu.VMEM((2, TOP_K, tm * ROW_TILE, PACKED_LANES), jnp.uint32),
                        pltpu.SemaphoreType.DMA((2,))],
        compiler_params=_cparams(("arbitrary",)),
        name=name,
    )(dest, dest, x1, wts_t, yb, w_sg, w_su, w_sd, ln_g, ln_b)


def kernel(x_prompt, x_sample, cache_win_k, cache_win_v, state_hgrn, cache_mem_k, cache_mem_v, mem_prompt,
           w_in, w_mem_kv, a_sinks, b_lb_logits, b_norm_g, w_branch, w_out, ln1_g, ln1_b,
           w_router, router_bias, w_exp_gate, w_exp_up, w_exp_down, w_sh_gate, w_sh_up, w_sh_down, ln2_g, ln2_b):
    assert w_in.shape[0] == DEPTH == 1
    batch, seq, _ = x_prompt.shape
    dbatch, dseq, _ = x_sample.shape
    n_p = batch * seq
    n_s = dbatch * dseq
    n = n_p + n_s
    l = 0

    xp2 = x_prompt.reshape(n_p, D_MODEL)
    xs2 = x_sample.reshape(n_s, D_MODEL)
    win = w_in[l]
    o_k, o_v, o_b = A_WIDTH, A_WIDTH + A_KV_WIDTH, A_WIDTH + 2 * A_KV_WIDTH
    o_c = o_b + 4 * B_WIDTH
    o_g = o_c + C_WIDTH
    groups = [
        ("aq", 0, o_k, 1024, BF16, A_HEAD_DIM ** -0.5),
        ("akv", o_k, o_b, 2 * A_KV_WIDTH, F32, 1.0),
        ("hgrn", o_b, o_c, 1024, F32, 1.0),
        ("cq", o_c, o_g, 1024, BF16, C_HEAD_DIM ** -0.5),
        ("gate", o_g, win.shape[1], 1024, F32, 1.0),
    ]
    zp, zs = {}, {}
    for gname, c0, c1, tn, dt, scale in groups:
        wslice = win[:, c0:c1].astype(BF16)
        zp[gname] = _matmul(xp2, wslice, tm=1024, tn=tn, out_dtype=dt, scale=scale, name=f"proj_{gname}_p")
        zs[gname] = _matmul(xs2, wslice, tm=n_s, tn=tn, out_dtype=dt, scale=scale, name=f"proj_{gname}_s")

    a_p = _swa_prompt(zp["aq"], zp["akv"], a_sinks[l], batch=batch, seq=seq)
    lc = cache_win_k.shape[2]
    kc = cache_win_k[l].reshape(dbatch, lc, A_KV_WIDTH)
    vc = cache_win_v[l].reshape(dbatch, lc, A_KV_WIDTH)
    a_s = _swa_sample(zs["aq"], zs["akv"], kc, vc, a_sinks[l], row0=0, batch=dbatch, seq=dseq)

    lower = jnp.cumsum(jax.nn.softmax(b_lb_logits.astype(F32), axis=0), axis=0)[l]
    lbp = jnp.stack([jnp.log(lower), jnp.log1p(-lower), 1.0 - lower])
    ng = jnp.tile(b_norm_g[l].astype(F32), B_HEADS).reshape(1, B_WIDTH)
    s_zero = jnp.zeros((batch, B_HEADS, B_KEY_DIM, B_VAL_DIM), F32)
    b_p, hs_p = _hgrn(zp["hgrn"], lbp, ng, s_zero, batch=batch, seq=seq, blk=CHUNK, nbat=2 if batch % 2 == 0 else 1)
    b_s, hs_s = _hgrn(zs["hgrn"], lbp, ng, state_hgrn[l].astype(F32), batch=dbatch, seq=dseq, blk=dseq,
                      nbat=2 if dbatch % 2 == 0 else 1)

    mem = mem_prompt.reshape(batch * MEM_LEN, D_MODEL)
    wmem = w_mem_kv[l]
    mk = _matmul(mem, wmem[:, :C_WIDTH].astype(BF16), tm=batch * MEM_LEN, tn=512, out_dtype=F32, name="proj_mem_k")
    mv = _matmul(mem, wmem[:, C_WIDTH:].astype(BF16), tm=batch * MEM_LEN, tn=512, out_dtype=F32, name="proj_mem_v")
    mk = mk.reshape(batch, MEM_LEN, C_WIDTH)
    mv = mv.reshape(batch, MEM_LEN, C_WIDTH)
    c_p = _cattn(zp["cq"], mk, mv, row0=0, batch=batch, seq=seq, tq=512)
    c_s = _cattn(zs["cq"], cache_mem_k[l].reshape(dbatch, MEM_LEN, C_WIDTH),
                 cache_mem_v[l].reshape(dbatch, MEM_LEN, C_WIDTH), row0=0, batch=dbatch, seq=dseq, tq=dseq)

    w_br = w_branch[l].astype(BF16)
    route_w = (w_out[l].astype(BF16), ln1_g[l].reshape(1, D_MODEL), ln1_b[l].reshape(1, D_MODEL),
               w_router[l].T.astype(BF16), router_bias[l].reshape(N_EXPERTS, 1).astype(F32))
    segs = []
    cnt = jnp.zeros((N_EXPERTS, 128), jnp.int32)
    for tag, x2, abc, gate, tm_r in (("p", xp2, (a_p, b_p, c_p), zp["gate"], 512), ("s", xs2, (a_s, b_s, c_s), zs["gate"], n_s)):
        h = _merge(*abc, gate, w_br, tm=tm_r, name=f"merge_{tag}")
        x1, x1p, idx, wts, pos, cnt = _out_ln_route(x2, h, *route_w, cnt, tm=tm_r, name=f"out_ln_route_{tag}")
        segs.append(dict(tag=tag, x1=x1, x1p=x1p, idx=idx, wts=wts, pos=pos, tm=tm_r, rows=x2.shape[0]))

    m = MOE_BLOCK
    n_pairs = n * TOP_K
    n_blocks = (n_pairs + m - 1) // m + N_EXPERTS
    counts = cnt[:, 0]
    padded = (counts + m - 1) // m * m
    pad_end = jnp.cumsum(padded)
    pad_start = (pad_end - padded).astype(jnp.int32)
    n_valid = pad_end[-1:] // m
    e_ids = jnp.arange(N_EXPERTS, dtype=jnp.int32)
    blk_first = jnp.arange(n_blocks, dtype=jnp.int32) * m
    blk_e = jnp.sum((blk_first[:, None] >= pad_end[None, :]).astype(jnp.int32), axis=1)
    blk_e = jnp.minimum(blk_e, N_EXPERTS - 1)
    row_end = jnp.sum(jnp.where(blk_e[:, None] == e_ids, (pad_start + counts.astype(jnp.int32))[None, :], 0), axis=1)
    blk_rows = jnp.clip(row_end - blk_first, 0, m).astype(jnp.int32)
    z_pad = jnp.where(padded > counts, pad_end - m, -1)
    z_tail = jnp.where(blk_first >= pad_end[-1], blk_first, -1)
    zrows = jnp.concatenate([z_pad, z_tail]).astype(jnp.int32)

    def retile(dest, rows, tm_from, tm_to):
        d = dest.reshape(rows // tm_from, TOP_K, tm_from // tm_to, tm_to)
        return jnp.transpose(d, (0, 2, 1, 3)).reshape(rows // tm_to, TOP_K, tm_to)

    tm_d = 256
    for seg in segs:
        first_row = jnp.sum(jnp.where(seg["idx"][..., None] == e_ids, pad_start, 0), axis=-1)
        seg["dest"] = first_row + seg["pos"]
    dest_d = jnp.concatenate([retile(seg["dest"], seg["rows"], seg["tm"], tm_d) for seg in segs], axis=0)
    xs = _dispatch(zrows, dest_d, segs[0]["x1p"], segs[1]["x1p"], n_rows=n_blocks * m, tm=tm_d, block_rows=m)
    yb = _moe(xs, blk_e, n_valid.astype(jnp.int32), blk_rows, w_exp_gate[l].astype(F32), w_exp_up[l].astype(F32),
              w_exp_down[l].astype(F32))
    tm_f = 256
    shared_w = (w_sh_gate[l].astype(BF16), w_sh_up[l].astype(BF16), w_sh_down[l].astype(BF16),
                ln2_g[l].reshape(1, D_MODEL), ln2_b[l].reshape(1, D_MODEL))
    ys = []
    for seg in segs:
        rows = seg["rows"]
        wts_t = jnp.transpose(seg["wts"], (0, 2, 1)).reshape(rows, TOP_K)
        ys.append(_final(retile(seg["dest"], rows, seg["tm"], tm_f), seg["x1"], wts_t, yb, *shared_w, tm=tm_f,
                         name=f"combine_shared_ln2_{seg['tag']}"))
    y_p = ys[0].reshape(batch, seq, D_MODEL)
    y_s = ys[1].reshape(dbatch, dseq, D_MODEL)

    kv_p = zp["akv"].reshape(batch, seq, 2, A_KV_HEADS, A_HEAD_DIM)[:, -lc:]
    k_p, v_p = kv_p[:, :, 0], kv_p[:, :, 1]
    kv_s = zs["akv"].reshape(dbatch, dseq, 2, A_KV_HEADS, A_HEAD_DIM)
    k_s, v_s = kv_s[:, :, 0], kv_s[:, :, 1]
    wk_s = jnp.concatenate([cache_win_k[l].astype(F32), k_s], axis=1)[:, -lc:]
    wv_s = jnp.concatenate([cache_win_v[l].astype(F32), v_s], axis=1)[:, -lc:]
    mk_o = mk.reshape(batch, MEM_LEN, C_HEADS, C_HEAD_DIM)
    mv_o = mv.reshape(batch, MEM_LEN, C_HEADS, C_HEAD_DIM)
    return (y_p, y_s, k_p[None], v_p[None], hs_p[None], mk_o[None], mv_o[None], wk_s[None], wv_s[None], hs_s[None])
```

```python
import functools

import jax
import jax.numpy as jnp
import numpy as np
from jax import lax
from jax.experimental import pallas as pl
from jax.experimental.pallas import tpu as pltpu

F32 = jnp.float32
BF16 = jnp.bfloat16

D_MODEL = 2048
DEPTH = 1
PAST_LEN = 2048
CHUNK = 64
A_HEADS = 16
A_KV_HEADS = 4
A_GROUP = A_HEADS // A_KV_HEADS
A_HEAD_DIM = 64
A_WIDTH = A_HEADS * A_HEAD_DIM
A_KV_WIDTH = A_KV_HEADS * A_HEAD_DIM
WINDOW = 128
WIN_CHUNKS = WINDOW // CHUNK
B_HEADS = 8
B_KEY_DIM = 128
B_VAL_DIM = 128
B_WIDTH = B_HEADS * B_VAL_DIM
SUB = 16
MEM_LEN = 256
C_HEADS = 4
C_HEAD_DIM = 256
C_WIDTH = C_HEADS * C_HEAD_DIM
N_EXPERTS = 64
N_GROUPS = 8
GROUP_SIZE = N_EXPERTS // N_GROUPS
TOPK_GROUPS = 4
TOP_K = 8
EXPERT_DIM = 512
ROUTED_SCALE = 2.5
DN_ALPHA = (2 * DEPTH) ** 0.25
LN_EPS = 1e-5
RMS_EPS = 1e-6

MOE_BLOCK = 512
VMEM_LIMIT = 56 * 1024 * 1024


def _cparams(sem):
    return pltpu.CompilerParams(dimension_semantics=sem, vmem_limit_bytes=VMEM_LIMIT)


def _resident(shape, index_map):
    return pl.BlockSpec(shape, index_map, pipeline_mode=pl.Buffered(1))


def _sigmoid(x):
    return 1.0 / (1.0 + jnp.exp(-x))


def _layer_norm(x, g, b):
    mu = jnp.mean(x, axis=-1, keepdims=True)
    xc = x - mu
    var = jnp.mean(xc * xc, axis=-1, keepdims=True)
    return xc * lax.rsqrt(var + LN_EPS) * g + b


def _mm_kernel(x_ref, w_ref, o_ref, *, scale):
    acc = jnp.dot(x_ref[...].astype(BF16), w_ref[...], preferred_element_type=F32)
    if scale != 1.0:
        acc = acc * scale
    o_ref[...] = acc.astype(o_ref.dtype)


def _matmul(x, w, *, tm, tn, out_dtype, scale=1.0, name):
    m, k = x.shape
    n = w.shape[1]
    return pl.pallas_call(
        functools.partial(_mm_kernel, scale=scale),
        grid=(m // tm, n // tn),
        in_specs=[pl.BlockSpec((tm, k), lambda i, j: (i, 0)), pl.BlockSpec((k, tn), lambda i, j: (0, j))],
        out_specs=pl.BlockSpec((tm, tn), lambda i, j: (i, j)),
        out_shape=jax.ShapeDtypeStruct((m, n), out_dtype),
        compiler_params=_cparams(("parallel", "parallel")),
        name=name,
    )(x, w)


SWA_KEYS = 256
HEAD_PAIR = 2 * A_HEAD_DIM


def _kv_head_planes(x):
    lane = lax.broadcasted_iota(jnp.int32, (1, HEAD_PAIR), 1)
    low = lane < A_HEAD_DIM
    planes = []
    for pair in range(A_KV_HEADS // 2):
        own = x[:, pair * HEAD_PAIR:(pair + 1) * HEAD_PAIR]
        swapped = pltpu.roll(own, A_HEAD_DIM, 1)
        planes.append((jnp.where(low, own, 0.0).astype(BF16), jnp.where(low, 0.0, swapped).astype(BF16)))
        planes.append((jnp.where(low, swapped, 0.0).astype(BF16), jnp.where(low, 0.0, own).astype(BF16)))
    return planes


SWA_LOOKAHEAD = 8


def _swa_chunks(q_ref, rows, chunk_rows, k_planes, v_planes, n_band, bias_ref, valid_of, o_ref):
    zeros_k = jnp.zeros((SWA_KEYS - n_band, HEAD_PAIR), BF16)
    ones_v = jnp.ones((SWA_KEYS, HEAD_PAIR), BF16)
    tiles = [(c, h, x) for c in range(len(chunk_rows)) for h in range(A_KV_HEADS) for x in range(2)]

    def band(plane, c):
        return jnp.concatenate([plane[chunk_rows[c]:chunk_rows[c] + n_band], zeros_k], axis=0)

    def scores(t):
        c, h, x = tiles[t]
        q2 = jnp.concatenate([q_ref[c * rows:(c + 1) * rows, (2 * h + j) * HEAD_PAIR:(2 * h + j + 1) * HEAD_PAIR]
                              for j in range(2)], axis=0).astype(BF16)
        s = lax.dot_general(q2, band(k_planes[h][x], c), (((1,), (1,)), ((), ())), preferred_element_type=F32)
        s = s - bias_ref[h, x]
        valid = valid_of(c)
        return s if valid is None else jnp.where(valid, s, -jnp.inf)

    pending = {t: scores(t) for t in range(min(SWA_LOOKAHEAD, len(tiles)))}
    even = None
    for t, (c, h, x) in enumerate(tiles):
        s = pending.pop(t)
        p = jnp.exp(s - jnp.max(s, axis=-1, keepdims=True)).astype(BF16)
        if t + SWA_LOOKAHEAD < len(tiles):
            pending[t + SWA_LOOKAHEAD] = scores(t + SWA_LOOKAHEAD)
        vw = jnp.concatenate([band(v_planes[h][x], c), ones_v], axis=1)
        o = jnp.dot(p, vw, preferred_element_type=F32)
        o = o[:, :HEAD_PAIR] / o[:, HEAD_PAIR:]
        if x == 0:
            even = o
        else:
            out = even + o
            for j in range(2):
                o_ref[c * rows:(c + 1) * rows, (2 * h + j) * HEAD_PAIR:(2 * h + j + 1) * HEAD_PAIR] = (
                    out[j * rows:(j + 1) * rows].astype(o_ref.dtype))


def _swa_prompt_kernel(q_ref, kc_ref, kp_ref, vc_ref, vp_ref, bias_ref, o_ref, *, n_chunks):
    i = pl.program_id(1)
    pad = WIN_CHUNKS * CHUNK
    n_band = pad + CHUNK
    k_planes = _kv_head_planes(jnp.concatenate([kp_ref[...], kc_ref[...]], axis=0))
    v_planes = _kv_head_planes(jnp.concatenate([vp_ref[...], vc_ref[...]], axis=0))
    s_idx = lax.broadcasted_iota(jnp.int32, (1, SWA_KEYS), 1)

    def valid_of(c):
        return jnp.logical_or(s_idx + (i * (n_chunks * CHUNK) + c * CHUNK - pad) >= 0, s_idx >= n_band)

    _swa_chunks(q_ref, CHUNK, [c * CHUNK for c in range(n_chunks)], k_planes, v_planes, n_band, bias_ref, valid_of,
                o_ref)


def _swa_sample_kernel(q_ref, kn_ref, kc_ref, vn_ref, vc_ref, bias_ref, o_ref):
    k_planes = _kv_head_planes(jnp.concatenate([kc_ref[0], kn_ref[...]], axis=0))
    v_planes = _kv_head_planes(jnp.concatenate([vc_ref[0], vn_ref[...]], axis=0))
    n_band = kc_ref.shape[1] + kn_ref.shape[0]
    _swa_chunks(q_ref, q_ref.shape[0], [0], k_planes, v_planes, n_band, bias_ref, lambda c: None, o_ref)


def _alibi_slopes():
    return (2.0 ** (-8.0 * np.arange(1, A_HEADS + 1) / A_HEADS)).astype(np.float32)


def _swa_bias(q_pos, k_pos, valid, sinks):
    n_q, n_k = len(q_pos), len(k_pos)
    assert n_k < SWA_KEYS
    dist = np.abs(q_pos[:, None] - k_pos[None, :]).astype(np.float32)
    band = _alibi_slopes()[:, None, None] * dist[None]
    if valid is not None:
        band = np.where(valid[None], band, np.inf)
    pad = np.full((A_HEADS, n_q, SWA_KEYS - n_k - 1), np.inf, np.float32)
    sink = jnp.broadcast_to(-sinks.astype(F32)[:, None, None], (A_HEADS, n_q, 1))
    bias = jnp.concatenate([jnp.asarray(band.astype(np.float32)), sink, jnp.asarray(pad)], axis=2)
    bias = bias.reshape(A_KV_HEADS, 2, 2, n_q, SWA_KEYS).transpose(0, 2, 1, 3, 4)
    return bias.reshape(A_KV_HEADS, 2, 2 * n_q, SWA_KEYS)


def _swa_prompt(q, kv, sinks, *, batch, seq, n_chunks=4):
    tq = n_chunks * CHUNK
    pad = WIN_CHUNKS * CHUNK
    nb = seq // tq
    bias = _swa_bias(pad + np.arange(CHUNK), np.arange(pad + CHUNK), None, sinks)
    prev_per_blk = tq // pad

    def cur(b, i, col=0):
        return (b * nb + i, col)

    def prev(b, i, col=0):
        return (jnp.maximum((b * nb + i) * prev_per_blk - 1, b * nb * prev_per_blk), col)

    return pl.pallas_call(
        functools.partial(_swa_prompt_kernel, n_chunks=n_chunks),
        grid=(batch, nb),
        in_specs=[
            pl.BlockSpec((tq, A_WIDTH), cur),
            pl.BlockSpec((tq, A_KV_WIDTH), cur),
            pl.BlockSpec((pad, A_KV_WIDTH), prev),
            pl.BlockSpec((tq, A_KV_WIDTH), functools.partial(cur, col=1)),
            pl.BlockSpec((pad, A_KV_WIDTH), functools.partial(prev, col=1)),
            pl.BlockSpec(bias.shape, lambda b, i: (0, 0, 0, 0)),
        ],
        out_specs=pl.BlockSpec((tq, A_WIDTH), cur),
        out_shape=jax.ShapeDtypeStruct((batch * seq, A_WIDTH), BF16),
        compiler_params=_cparams(("parallel", "arbitrary")),
        name="swa_prompt",
    )(q, kv, kv, kv, kv, bias)


def _swa_sample(q, kv, k_cache, v_cache, sinks, *, row0, batch, seq):
    lc = k_cache.shape[1]
    q_pos = PAST_LEN + np.arange(seq)
    k_pos = PAST_LEN - lc + np.arange(lc + seq)
    cdiff = q_pos[:, None] // CHUNK - k_pos[None, :] // CHUNK
    valid = (cdiff >= 0) & (cdiff <= WIN_CHUNKS)
    bias = _swa_bias(q_pos, k_pos, valid, sinks)
    blk0 = row0 // seq

    def rows(b):
        return (blk0 + b, 0)

    return pl.pallas_call(
        _swa_sample_kernel,
        grid=(batch,),
        in_specs=[
            pl.BlockSpec((seq, A_WIDTH), rows),
            pl.BlockSpec((seq, A_KV_WIDTH), rows),
            pl.BlockSpec((1, lc, A_KV_WIDTH), lambda b: (b, 0, 0)),
            pl.BlockSpec((seq, A_KV_WIDTH), lambda b: (blk0 + b, 1)),
            pl.BlockSpec((1, lc, A_KV_WIDTH), lambda b: (b, 0, 0)),
            pl.BlockSpec(bias.shape, lambda b: (0, 0, 0, 0)),
        ],
        out_specs=pl.BlockSpec((seq, A_WIDTH), lambda b: (b, 0)),
        out_shape=jax.ShapeDtypeStruct((batch * seq, A_WIDTH), BF16),
        compiler_params=_cparams(("parallel",)),
        name="swa_sample",
    )(q, kv, k_cache, kv, v_cache, bias)


def _split3(x):
    hi = x.astype(BF16)
    r1 = x - hi.astype(F32)
    mid = r1.astype(BF16)
    lo = (r1 - mid.astype(F32)).astype(BF16)
    return hi, mid, lo


def _dot3(mat, parts):
    acc = jnp.dot(mat, parts[0], preferred_element_type=F32)
    acc = acc + jnp.dot(mat, parts[1], preferred_element_type=F32)
    return acc + jnp.dot(mat, parts[2], preferred_element_type=F32)


def _hgrn_kernel(zq_ref, zf_ref, zi_ref, zg_ref, lb_ref, ng_ref, tri_ref, tsel_ref, s0_ref, o_ref, sfin_ref, s_scr,
                 *, blk, nbat):
    j = pl.program_id(1)

    @pl.when(j == 0)
    def _():
        s_scr[...] = s0_ref[...]

    log_lb = lb_ref[0:1, :]
    log1m_lb = lb_ref[1:2, :]
    one_m_lb = lb_ref[2:3, :]
    row = lax.broadcasted_iota(jnp.int32, (blk, 1), 0)
    n_sub = blk // SUB
    tril = lax.broadcasted_iota(jnp.int32, (blk, blk), 0) >= lax.broadcasted_iota(jnp.int32, (blk, blk), 1)
    heads = [slice(h * B_KEY_DIM, (h + 1) * B_KEY_DIM) for h in range(B_HEADS)]

    def prepare(bi):
        bq = zq_ref[bi]
        fl = zf_ref[bi]
        q = bq * _sigmoid(bq) * (B_KEY_DIM ** -0.5)
        log_sig = jnp.minimum(fl, 0.0) - jnp.log1p(jnp.exp(-jnp.abs(fl)))
        c = log1m_lb + log_sig
        logf = jnp.maximum(log_lb, c) + jnp.log1p(jnp.exp(-jnp.abs(log_lb - c)))
        k = one_m_lb * _sigmoid(-fl)
        parts = _split3(logf)
        b = _dot3(tri_ref[...], parts)
        rq = _dot3(tsel_ref[...], parts)
        qt = (q * jnp.exp(b - rq)).astype(BF16)
        qb = (q * jnp.exp(b)).astype(BF16)
        b_last = b[blk - 1:blk, :]
        khat = (k * jnp.exp(b_last - b)).astype(BF16)
        vb = zi_ref[bi].astype(BF16)
        kts = []
        for i in range(n_sub):
            r_i = rq[i * SUB:i * SUB + 1, :]
            kts.append(jnp.where(row < (i + 1) * SUB, k * jnp.exp(r_i - b), 0.0).astype(BF16))
        s_old = [s_scr[bi, h] for h in range(B_HEADS)]
        a_raw, o_state, ds = [], [], []
        for h, hs in enumerate(heads):
            a_raw.append(jnp.concatenate(
                [lax.dot_general(qt[i * SUB:(i + 1) * SUB, hs], kts[i][:, hs], (((1,), (1,)), ((), ())),
                                 preferred_element_type=F32) for i in range(n_sub)], axis=0))
            o_state.append(jnp.dot(qb[:, hs], s_old[h].astype(BF16), preferred_element_type=F32))
            ds.append(lax.dot_general(khat[:, hs], vb[:, hs], (((0,), (0,)), ((), ())), preferred_element_type=F32))
        return dict(a_raw=a_raw, o_state=o_state, ds=ds, s_old=s_old, vb=vb, e_last=jnp.exp(b_last))

    def finish(bi, p):
        outs = []
        for h, hs in enumerate(heads):
            a = jnp.where(tril, p["a_raw"][h], 0.0).astype(BF16)
            o = jnp.dot(a, p["vb"][:, hs], preferred_element_type=F32) + p["o_state"][h]
            decay = jnp.transpose(jnp.broadcast_to(p["e_last"][:, hs], (B_KEY_DIM, B_KEY_DIM)))
            s_scr[bi, h] = decay * p["s_old"][h] + p["ds"][h]
            outs.append(o * lax.rsqrt(jnp.mean(o * o, axis=-1, keepdims=True) + RMS_EPS))
        bg = zg_ref[bi]
        o_all = jnp.concatenate(outs, axis=1) * ng_ref[...] * (bg * _sigmoid(bg))
        o_ref[bi] = o_all.astype(o_ref.dtype)

    prepared = [prepare(bi) for bi in range(nbat)]
    for bi in range(nbat):
        finish(bi, prepared[bi])

    @pl.when(j == pl.num_programs(1) - 1)
    def _():
        sfin_ref[...] = s_scr[...]


def _hgrn(zb, lbp, ng, s0, *, batch, seq, blk, nbat):
    nb = seq // blk
    t = np.arange(blk)
    tri = jnp.asarray((t[:, None] >= t[None, :]).astype(np.float32), dtype=BF16)
    tsel = jnp.asarray((t[None, :] < (t[:, None] // SUB) * SUB).astype(np.float32), dtype=BF16)
    z3 = zb.reshape(batch, seq, 4 * B_WIDTH)

    def zspec(col):
        return pl.BlockSpec((nbat, blk, B_WIDTH), lambda b, j: (b, j, col))

    state_spec = pl.BlockSpec((nbat, B_HEADS, B_KEY_DIM, B_VAL_DIM), lambda b, j: (b, 0, 0, 0))
    out, s_fin = pl.pallas_call(
        functools.partial(_hgrn_kernel, blk=blk, nbat=nbat),
        grid=(batch // nbat, nb),
        in_specs=[
            zspec(0), zspec(1), zspec(2), zspec(3),
            pl.BlockSpec((3, B_WIDTH), lambda b, j: (0, 0)),
            pl.BlockSpec((1, B_WIDTH), lambda b, j: (0, 0)),
            pl.BlockSpec((blk, blk), lambda b, j: (0, 0)),
            pl.BlockSpec((blk, blk), lambda b, j: (0, 0)),
            state_spec,
        ],
        out_specs=[pl.BlockSpec((nbat, blk, B_WIDTH), lambda b, j: (b, j, 0)), state_spec],
        out_shape=[jax.ShapeDtypeStruct((batch, seq, B_WIDTH), BF16),
                   jax.ShapeDtypeStruct((batch, B_HEADS, B_KEY_DIM, B_VAL_DIM), F32)],
        scratch_shapes=[pltpu.VMEM((nbat, B_HEADS, B_KEY_DIM, B_VAL_DIM), F32)],
        compiler_params=_cparams(("parallel", "arbitrary")),
        name=f"hgrn_blk{blk}",
    )(z3, z3, z3, z3, lbp, ng, tri, tsel, s0)
    return out.reshape(batch * seq, B_WIDTH), s_fin


def _cattn_kernel(q_ref, mk_ref, mv_ref, o_ref):
    mk = mk_ref[0].astype(BF16)
    mv = mv_ref[0].astype(BF16)
    q = q_ref[...]
    heads = [slice(h * C_HEAD_DIM, (h + 1) * C_HEAD_DIM) for h in range(C_HEADS)]
    scores = [lax.dot_general(q[:, hs], mk[:, hs], (((1,), (1,)), ((), ())), preferred_element_type=F32)
              for hs in heads]
    for hs, s in zip(heads, scores):
        p = jnp.exp(s - jnp.max(s, axis=-1, keepdims=True))
        p = p / jnp.sum(p, axis=-1, keepdims=True)
        o_ref[:, hs] = jnp.dot(p.astype(BF16), mv[:, hs], preferred_element_type=F32).astype(o_ref.dtype)


def _cattn(q, mk, mv, *, row0, batch, seq, tq):
    nb = seq // tq
    blk0 = row0 // tq
    mem_spec = pl.BlockSpec((1, MEM_LEN, C_WIDTH), lambda b, i: (b, 0, 0))
    return pl.pallas_call(
        _cattn_kernel,
        grid=(batch, nb),
        in_specs=[pl.BlockSpec((tq, C_WIDTH), lambda b, i: (blk0 + b * nb + i, 0)), mem_spec, mem_spec],
        out_specs=pl.BlockSpec((tq, C_WIDTH), lambda b, i: (b * nb + i, 0)),
        out_shape=jax.ShapeDtypeStruct((batch * seq, C_WIDTH), BF16),
        compiler_params=_cparams(("parallel", "parallel")),
        name=f"cattn_tq{tq}",
    )(q, mk, mv)


def _merge_kernel(a_ref, b_ref, c_ref, g0_ref, g1_ref, g2_ref, pa_ref, pb_ref, pc_ref, o_ref):
    da = jnp.dot(a_ref[...], pa_ref[...], preferred_element_type=F32)
    db = jnp.dot(b_ref[...], pb_ref[...], preferred_element_type=F32)
    dc = jnp.dot(c_ref[...], pc_ref[...], preferred_element_type=F32)
    h = _sigmoid(g0_ref[...]) * da + _sigmoid(g1_ref[...]) * db + _sigmoid(g2_ref[...]) * dc
    o_ref[...] = h.astype(o_ref.dtype)


def _merge(a, b, c, gl, p, *, tm, name):
    n, width = a.shape

    def rows(i):
        return (i, 0)

    return pl.pallas_call(
        _merge_kernel,
        grid=(n // tm,),
        in_specs=[
            pl.BlockSpec((tm, width), rows), pl.BlockSpec((tm, width), rows), pl.BlockSpec((tm, width), rows),
            pl.BlockSpec((tm, D_MODEL), lambda i: (i, 0)),
            pl.BlockSpec((tm, D_MODEL), lambda i: (i, 1)),
            pl.BlockSpec((tm, D_MODEL), lambda i: (i, 2)),
            _resident((width, D_MODEL), lambda i: (0, 0)),
            _resident((width, D_MODEL), lambda i: (1, 0)),
            _resident((width, D_MODEL), lambda i: (2, 0)),
        ],
        out_specs=pl.BlockSpec((tm, D_MODEL), rows),
        out_shape=jax.ShapeDtypeStruct((n, D_MODEL), BF16),
        compiler_params=_cparams(("parallel",)),
        name=name,
    )(a, b, c, gl, gl, gl, p, p, p)


def _first_index(hit_src, m, iota, size, axis):
    return jnp.min(jnp.where(hit_src == m, iota, size), axis=axis, keepdims=True)


def _route(x1, wr_ref, rb_ref, upper_ref, run_ref, idx_ref, wts_ref, pos_ref):
    tm = x1.shape[0]
    logits = lax.dot_general(wr_ref[...], x1.astype(BF16), (((1,), (1,)), ((), ())),
                             preferred_element_type=F32)
    scores = _sigmoid(logits)
    choice = scores + rb_ref[...]
    g3 = choice.reshape(N_GROUPS, GROUP_SIZE, tm)
    mem_iota = lax.broadcasted_iota(jnp.int32, g3.shape, 1)
    m1 = jnp.max(g3, axis=1, keepdims=True)
    first = _first_index(g3, m1, mem_iota, GROUP_SIZE, 1)
    m2 = jnp.max(jnp.where(mem_iota == first, -jnp.inf, g3), axis=1, keepdims=True)
    gscore = (m1 + m2).reshape(N_GROUPS, tm)
    g_iota = lax.broadcasted_iota(jnp.int32, gscore.shape, 0)
    gsel = jnp.zeros(gscore.shape, F32)
    cur = gscore
    for _ in range(TOPK_GROUPS):
        m = jnp.max(cur, axis=0, keepdims=True)
        hit = g_iota == _first_index(cur, m, g_iota, N_GROUPS, 0)
        gsel = jnp.where(hit, 1.0, gsel)
        cur = jnp.where(hit, -jnp.inf, cur)
    masked = jnp.where(gsel.reshape(N_GROUPS, 1, tm) > 0.5, g3, -jnp.inf).reshape(N_EXPERTS, tm)
    e_iota = lax.broadcasted_iota(jnp.int32, masked.shape, 0)
    sel = jnp.zeros(masked.shape, F32)
    cur = masked
    idxs, ws = [], []
    for _ in range(TOP_K):
        m = jnp.max(cur, axis=0, keepdims=True)
        ei = _first_index(cur, m, e_iota, N_EXPERTS, 0)
        hit = e_iota == ei
        idxs.append(ei)
        ws.append(jnp.sum(jnp.where(hit, scores, 0.0), axis=0, keepdims=True))
        sel = jnp.where(hit, 1.0, sel)
        cur = jnp.where(hit, -jnp.inf, cur)
    w = jnp.concatenate(ws, axis=0)
    w = w / jnp.sum(w, axis=0, keepdims=True) * ROUTED_SCALE
    excl = jnp.dot(sel.astype(BF16), upper_ref[...], preferred_element_type=F32)
    posfull = excl + run_ref[:, 0:1]
    run_ref[...] = run_ref[...] + jnp.sum(sel, axis=1, keepdims=True)
    pos = [jnp.sum(jnp.where(e_iota == ei, posfull, 0.0), axis=0, keepdims=True) for ei in idxs]
    idx_ref[0] = jnp.concatenate(idxs, axis=0)
    wts_ref[0] = w
    pos_ref[0] = jnp.concatenate(pos, axis=0).astype(jnp.int32)


def _pack_halves(x):
    c = x.shape[1] // 2
    lo = lax.bitcast_convert_type(x[:, :c].astype(BF16).astype(F32), jnp.uint32)
    hi = lax.bitcast_convert_type(x[:, c:].astype(BF16).astype(F32), jnp.uint32)
    return (hi & jnp.uint32(0xFFFF0000)) | (lo >> 16)


def _unpack_halves(w):
    lo = lax.bitcast_convert_type(w << 16, F32)
    hi = lax.bitcast_convert_type(w & jnp.uint32(0xFFFF0000), F32)
    return lo, hi


ROW_TILE = 8
PACKED_LANES = D_MODEL // 2 // ROW_TILE


def _store_row_tiles(ref, val):
    rows = val.shape[0]
    for s in range(ROW_TILE):
        ref[pl.ds(s, rows, stride=ROW_TILE), :] = val[:, s * PACKED_LANES:(s + 1) * PACKED_LANES]


def _load_row_tiles(ref, rows):
    return [ref[pl.ds(s, rows, stride=ROW_TILE), :] for s in range(ROW_TILE)]


def _out_ln_route_kernel(x_ref, h_ref, wo_ref, g_ref, b_ref, wr_ref, rb_ref, upper_ref, run0_ref,
                         x1_ref, x1p_ref, idx_ref, wts_ref, pos_ref, cnt_ref, run_ref):
    @pl.when(pl.program_id(0) == 0)
    def _():
        run_ref[...] = run0_ref[...].astype(F32)

    y = DN_ALPHA * x_ref[...] + jnp.dot(h_ref[...], wo_ref[...], preferred_element_type=F32)
    x1 = _layer_norm(y, g_ref[...], b_ref[...])
    x1_ref[...] = x1
    _store_row_tiles(x1p_ref, _pack_halves(x1))
    _route(x1, wr_ref, rb_ref, upper_ref, run_ref, idx_ref, wts_ref, pos_ref)
    cnt_ref[...] = run_ref[...].astype(jnp.int32)


def _out_ln_route(x, h, w_out, ln_g, ln_b, w_router_t, router_bias, run0, *, tm, name):
    n = h.shape[0]
    nt = n // tm
    t = np.arange(tm)
    upper = jnp.asarray((t[:, None] < t[None, :]).astype(np.float32), dtype=BF16)

    def rows(i):
        return (i, 0)

    def const(i):
        return (0, 0)

    small = pl.BlockSpec((1, TOP_K, tm), lambda i: (i, 0, 0))
    return pl.pallas_call(
        _out_ln_route_kernel,
        grid=(nt,),
        in_specs=[
            pl.BlockSpec((tm, D_MODEL), rows),
            pl.BlockSpec((tm, D_MODEL), rows),
            _resident((D_MODEL, D_MODEL), const),
            pl.BlockSpec((1, D_MODEL), const),
            pl.BlockSpec((1, D_MODEL), const),
            pl.BlockSpec((N_EXPERTS, D_MODEL), const),
            pl.BlockSpec((N_EXPERTS, 1), const),
            pl.BlockSpec((tm, tm), const),
            pl.BlockSpec((N_EXPERTS, 128), const),
        ],
        out_specs=[pl.BlockSpec((tm, D_MODEL), rows), pl.BlockSpec((tm * ROW_TILE, PACKED_LANES), rows), small, small,
                   small, pl.BlockSpec((N_EXPERTS, 128), const)],
        out_shape=[
            jax.ShapeDtypeStruct((n, D_MODEL), F32),
            jax.ShapeDtypeStruct((n * ROW_TILE, PACKED_LANES), jnp.uint32),
            jax.ShapeDtypeStruct((nt, TOP_K, tm), jnp.int32),
            jax.ShapeDtypeStruct((nt, TOP_K, tm), F32),
            jax.ShapeDtypeStruct((nt, TOP_K, tm), jnp.int32),
            jax.ShapeDtypeStruct((N_EXPERTS, 128), jnp.int32),
        ],
        scratch_shapes=[pltpu.VMEM((N_EXPERTS, 128), F32)],
        compiler_params=_cparams(("arbitrary",)),
        name=name,
    )(x, h, w_out, ln_g, ln_b, w_router_t, router_bias, upper, run0)


def _row_copy(src_hbm, row, dst, dst_row, sem):
    return pltpu.make_async_copy(src_hbm.at[pl.ds(row, 1)], dst.at[pl.ds(dst_row, 1)], sem)


def _tile_copy(src, src_row, dst, dst_row, sem):
    def tile(row):
        first = row * ROW_TILE
        return pl.ds(first if isinstance(row, int) else pl.multiple_of(first, ROW_TILE), ROW_TILE)

    return pltpu.make_async_copy(src.at[tile(src_row)], dst.at[tile(dst_row)], sem)


def _dispatch_kernel(zrow_ref, dest_ref, xa_ref, xb_ref, o_hbm, zbuf, sem, zsem, *, block_rows, n_first):
    tm = xa_ref.shape[0] // ROW_TILE
    n_zero = zrow_ref.shape[0]

    def zero_copy(b):
        start = pl.multiple_of(jnp.maximum(zrow_ref[b], 0) * ROW_TILE, ROW_TILE)
        return pltpu.make_async_copy(zbuf, o_hbm.at[pl.ds(start, block_rows * ROW_TILE)], zsem)

    def zero_unowned_rows():
        zbuf[...] = jnp.zeros_like(zbuf)

        def start(b, carry):
            @pl.when(zrow_ref[b] >= 0)
            def _():
                zero_copy(b).start()
            return carry

        def wait(b, carry):
            @pl.when(zrow_ref[b] >= 0)
            def _():
                zero_copy(b).wait()
            return carry

        lax.fori_loop(0, n_zero, start, 0)
        lax.fori_loop(0, n_zero, wait, 0)

    pl.when(pl.program_id(0) == 0)(zero_unowned_rows)

    def scatter_tile(x_ref):
        for t in range(tm):
            for k in range(TOP_K):
                _tile_copy(x_ref, t, o_hbm, dest_ref[0, k, t], sem).start(priority=k % 2)
        for k in range(TOP_K):
            pltpu.make_async_copy(x_ref, o_hbm.at[pl.ds(0, tm * ROW_TILE)], sem).wait()

    pl.when(pl.program_id(0) < n_first)(functools.partial(scatter_tile, xa_ref))
    pl.when(pl.program_id(0) >= n_first)(functools.partial(scatter_tile, xb_ref))


def _dispatch(zrows, dest_blk, x1p_a, x1p_b, *, n_rows, tm, block_rows):
    nf = x1p_a.shape[0] // ROW_TILE // tm
    ns = x1p_b.shape[0] // ROW_TILE // tm
    grid_spec = pltpu.PrefetchScalarGridSpec(
        num_scalar_prefetch=1,
        grid=(nf + ns,),
        in_specs=[
            pl.BlockSpec((1, TOP_K, tm), lambda i, z: (i, 0, 0), memory_space=pltpu.SMEM),
            pl.BlockSpec((tm * ROW_TILE, PACKED_LANES), lambda i, z: (jnp.minimum(i, nf - 1), 0)),
            pl.BlockSpec((tm * ROW_TILE, PACKED_LANES), lambda i, z: (jnp.clip(i - nf, 0, ns - 1), 0)),
        ],
        out_specs=pl.BlockSpec(memory_space=pl.ANY),
        scratch_shapes=[pltpu.VMEM((block_rows * ROW_TILE, PACKED_LANES), jnp.uint32),
                        pltpu.SemaphoreType.DMA(()), pltpu.SemaphoreType.DMA(())],
    )
    return pl.pallas_call(
        functools.partial(_dispatch_kernel, block_rows=block_rows, n_first=nf),
        grid_spec=grid_spec,
        out_shape=jax.ShapeDtypeStruct((n_rows * ROW_TILE, PACKED_LANES), jnp.uint32),
        compiler_params=_cparams(("arbitrary",)),
        name="dispatch",
    )(zrows, dest_blk, x1p_a, x1p_b)


def _moe_kernel(be_ref, nv_ref, nr_ref, nxt_ref, slot_ref, x_ref, wg_hbm, wu_hbm, wd_hbm, o_ref,
                wg_f, wu_f, wd_f, wg_b, wu_b, wd_b, wsem):
    k = pl.program_id(0)
    m = x_ref.shape[0] // ROW_TILE
    half = m // 2
    live = k < nv_ref[0]
    upper_half_used = nr_ref[k] > half
    new_expert = jnp.logical_or(k == 0, be_ref[k] != be_ref[jnp.maximum(k - 1, 0)])

    def weight_copies(e, slot):
        return [pltpu.make_async_copy(src.at[e], dst.at[slot], wsem.at[slot])
                for src, dst in ((wg_hbm, wg_f), (wu_hbm, wu_f), (wd_hbm, wd_f))]

    @pl.when(jnp.logical_and(live, k == 0))
    def _():
        for cp in weight_copies(be_ref[0], 0):
            cp.start()

    @pl.when(jnp.logical_and(live, new_expert))
    def _():
        slot = slot_ref[k]
        for cp in weight_copies(be_ref[k], slot):
            cp.wait()
        wg_b[...] = wg_f[slot].astype(BF16)
        wu_b[...] = wu_f[slot].astype(BF16)
        wd_b[...] = wd_f[slot].astype(BF16)

        @pl.when(nxt_ref[k] >= 0)
        def _():
            for cp in weight_copies(nxt_ref[k], 1 - slot):
                cp.start()

    def swiglu_rows(rows):
        lo, hi = _unpack_halves(jnp.concatenate(_load_row_tiles(x_ref, rows), axis=1))
        x = jnp.concatenate([lo.astype(BF16), hi.astype(BF16)], axis=1)
        g = jnp.dot(x, wg_b[...], preferred_element_type=F32)
        u = jnp.dot(x, wu_b[...], preferred_element_type=F32)
        hmid = (g * _sigmoid(g) * u).astype(BF16)
        y = jnp.dot(hmid, wd_b[...], preferred_element_type=F32)
        _store_row_tiles(o_ref, _pack_halves(y))

    @pl.when(jnp.logical_and(live, upper_half_used))
    def _():
        swiglu_rows(m)

    @pl.when(jnp.logical_and(live, jnp.logical_not(upper_half_used)))
    def _():
        swiglu_rows(half)
        o_ref[half * ROW_TILE:, :] = jnp.zeros((half * ROW_TILE, o_ref.shape[1]), o_ref.dtype)

    @pl.when(jnp.logical_not(live))
    def _():
        o_ref[...] = jnp.zeros_like(o_ref)


def _moe(xs, blk_e, n_valid, blk_rows, nxt_e, slot, w_gate, w_up, w_down):
    n_blocks = blk_e.shape[0]
    m = MOE_BLOCK

    def rows(k, be, nv, nr, nx, sl):
        return (jnp.minimum(k, nv[0] - 1), 0)

    hbm = pl.BlockSpec(memory_space=pl.ANY)
    grid_spec = pltpu.PrefetchScalarGridSpec(
        num_scalar_prefetch=5,
        grid=(n_blocks,),
        in_specs=[pl.BlockSpec((m * ROW_TILE, PACKED_LANES), rows), hbm, hbm, hbm],
        out_specs=pl.BlockSpec((m * ROW_TILE, PACKED_LANES), lambda k, be, nv, nr, nx, sl: (k, 0)),
        scratch_shapes=[pltpu.VMEM((2, D_MODEL, EXPERT_DIM), F32), pltpu.VMEM((2, D_MODEL, EXPERT_DIM), F32),
                        pltpu.VMEM((2, EXPERT_DIM, D_MODEL), F32),
                        pltpu.VMEM((D_MODEL, EXPERT_DIM), BF16), pltpu.VMEM((D_MODEL, EXPERT_DIM), BF16),
                        pltpu.VMEM((EXPERT_DIM, D_MODEL), BF16), pltpu.SemaphoreType.DMA((2,))],
    )
    return pl.pallas_call(
        _moe_kernel,
        grid_spec=grid_spec,
        out_shape=jax.ShapeDtypeStruct(xs.shape, jnp.uint32),
        compiler_params=_cparams(("arbitrary",)),
        name="moe_experts",
    )(blk_e, n_valid, blk_rows, nxt_e, slot, xs, w_gate, w_up, w_down)


def _final_kernel(dest_ref, dest_next_ref, x1_ref, w_ref, yb_hbm, sg_ref, su_ref, sd_ref, g_ref, b_ref, o_ref,
                  ybuf, sem):
    i = pl.program_id(0)
    tm = x1_ref.shape[0]

    def start(d_ref, slot, t):
        for k in range(TOP_K):
            _tile_copy(yb_hbm, d_ref[0, k, t], ybuf.at[slot, k], t, sem.at[slot]).start(priority=k % 2)

    def wait_all(slot):
        for k in range(TOP_K):
            pltpu.make_async_copy(yb_hbm.at[pl.ds(0, tm * ROW_TILE)], ybuf.at[slot, k], sem.at[slot]).wait()

    @pl.when(i == 0)
    def _():
        def issue(t, carry):
            start(dest_ref, 0, t)
            return carry

        lax.fori_loop(0, tm, issue, 0, unroll=4)

    cur = i % 2
    nxt = 1 - cur
    wait_all(cur)
    x1 = x1_ref[...]
    xb = x1.astype(BF16)
    g = jnp.dot(xb, sg_ref[...], preferred_element_type=F32)
    u = jnp.dot(xb, su_ref[...], preferred_element_type=F32)
    w = w_ref[...]
    per = tm // TOP_K
    r_lo = [None] * ROW_TILE
    r_hi = [None] * ROW_TILE
    for k in range(TOP_K):
        for t in range(k * per, (k + 1) * per):
            start(dest_next_ref, nxt, t)
        wk = jnp.broadcast_to(w[:, k:k + 1], (tm, PACKED_LANES))
        for s, piece in enumerate(_load_row_tiles(ybuf.at[cur, k], tm)):
            lo, hi = _unpack_halves(piece)
            r_lo[s] = wk * lo if r_lo[s] is None else r_lo[s] + wk * lo
            r_hi[s] = wk * hi if r_hi[s] is None else r_hi[s] + wk * hi
    y = jnp.dot((g * _sigmoid(g) * u).astype(BF16), sd_ref[...], preferred_element_type=F32)
    routed = jnp.concatenate(r_lo + r_hi, axis=1)
    o_ref[...] = _layer_norm(DN_ALPHA * x1 + (routed + y), g_ref[...], b_ref[...])

    @pl.when(i == pl.num_programs(0) - 1)
    def _():
        wait_all(nxt)


def _final(dest, x1, wts_t, yb, w_sg, w_su, w_sd, ln_g, ln_b, *, tm, name):
    n_tiles = x1.shape[0] // tm

    def tile(i):
        return (i, 0)

    def const(i):
        return (0, 0)

    smem = functools.partial(pl.BlockSpec, memory_space=pltpu.SMEM)
    return pl.pallas_call(
        _final_kernel,
        grid=(n_tiles,),
        in_specs=[
            smem((1, TOP_K, tm), lambda i: (i, 0, 0)),
            smem((1, TOP_K, tm), lambda i: (jnp.minimum(i + 1, n_tiles - 1), 0, 0)),
            pl.BlockSpec((tm, D_MODEL), tile),
            pl.BlockSpec((tm, TOP_K), tile),
            pl.BlockSpec(memory_space=pl.ANY),
            _resident((D_MODEL, EXPERT_DIM), const),
            _resident((D_MODEL, EXPERT_DIM), const),
            _resident((EXPERT_DIM, D_MODEL), const),
            pl.BlockSpec((1, D_MODEL), const),
            pl.BlockSpec((1, D_MODEL), const),
        ],
        out_specs=pl.BlockSpec((tm, D_MODEL), tile),
        out_shape=jax.ShapeDtypeStruct((n_tiles * tm, D_MODEL), F32),
        scratch_shapes=[pltpu.VMEM((2, TOP_K, tm * ROW_TILE, PACKED_LANES), jnp.uint32),
                        pltpu.SemaphoreType.DMA((2,))],
        compiler_params=_cparams(("arbitrary",)),
        name=name,
    )(dest, dest, x1, wts_t, yb, w_sg, w_su, w_sd, ln_g, ln_b)


def kernel(x_prompt, x_sample, cache_win_k, cache_win_v, state_hgrn, cache_mem_k, cache_mem_v, mem_prompt,
           w_in, w_mem_kv, a_sinks, b_lb_logits, b_norm_g, w_branch, w_out, ln1_g, ln1_b,
           w_router, router_bias, w_exp_gate, w_exp_up, w_exp_down, w_sh_gate, w_sh_up, w_sh_down, ln2_g, ln2_b):
    assert w_in.shape[0] == DEPTH == 1
    batch, seq, _ = x_prompt.shape
    dbatch, dseq, _ = x_sample.shape
    n_p = batch * seq
    n_s = dbatch * dseq
    n = n_p + n_s
    l = 0

    xp2 = x_prompt.reshape(n_p, D_MODEL)
    xs2 = x_sample.reshape(n_s, D_MODEL)
    win = w_in[l]
    o_k, o_v, o_b = A_WIDTH, A_WIDTH + A_KV_WIDTH, A_WIDTH + 2 * A_KV_WIDTH
    o_c = o_b + 4 * B_WIDTH
    o_g = o_c + C_WIDTH
    groups = [
        ("aq", 0, o_k, 1024, BF16, A_HEAD_DIM ** -0.5),
        ("akv", o_k, o_b, 2 * A_KV_WIDTH, F32, 1.0),
        ("hgrn", o_b, o_c, 1024, F32, 1.0),
        ("cq", o_c, o_g, 1024, BF16, C_HEAD_DIM ** -0.5),
        ("gate", o_g, win.shape[1], 1024, F32, 1.0),
    ]
    zp, zs = {}, {}
    for gname, c0, c1, tn, dt, scale in groups:
        wslice = win[:, c0:c1].astype(BF16)
        zp[gname] = _matmul(xp2, wslice, tm=1024, tn=tn, out_dtype=dt, scale=scale, name=f"proj_{gname}_p")
        zs[gname] = _matmul(xs2, wslice, tm=n_s, tn=tn, out_dtype=dt, scale=scale, name=f"proj_{gname}_s")

    a_p = _swa_prompt(zp["aq"], zp["akv"], a_sinks[l], batch=batch, seq=seq)
    lc = cache_win_k.shape[2]
    kc = cache_win_k[l].reshape(dbatch, lc, A_KV_WIDTH)
    vc = cache_win_v[l].reshape(dbatch, lc, A_KV_WIDTH)
    a_s = _swa_sample(zs["aq"], zs["akv"], kc, vc, a_sinks[l], row0=0, batch=dbatch, seq=dseq)

    lower = jnp.cumsum(jax.nn.softmax(b_lb_logits.astype(F32), axis=0), axis=0)[l]
    lbp = jnp.stack([jnp.log(lower), jnp.log1p(-lower), 1.0 - lower])
    ng = jnp.tile(b_norm_g[l].astype(F32), B_HEADS).reshape(1, B_WIDTH)
    s_zero = jnp.zeros((batch, B_HEADS, B_KEY_DIM, B_VAL_DIM), F32)
    b_p, hs_p = _hgrn(zp["hgrn"], lbp, ng, s_zero, batch=batch, seq=seq, blk=CHUNK, nbat=2 if batch % 2 == 0 else 1)
    b_s, hs_s = _hgrn(zs["hgrn"], lbp, ng, state_hgrn[l].astype(F32), batch=dbatch, seq=dseq, blk=dseq,
                      nbat=2 if dbatch % 2 == 0 else 1)

    mem = mem_prompt.reshape(batch * MEM_LEN, D_MODEL)
    wmem = w_mem_kv[l]
    mk = _matmul(mem, wmem[:, :C_WIDTH].astype(BF16), tm=batch * MEM_LEN, tn=512, out_dtype=F32, name="proj_mem_k")
    mv = _matmul(mem, wmem[:, C_WIDTH:].astype(BF16), tm=batch * MEM_LEN, tn=512, out_dtype=F32, name="proj_mem_v")
    mk = mk.reshape(batch, MEM_LEN, C_WIDTH)
    mv = mv.reshape(batch, MEM_LEN, C_WIDTH)
    c_p = _cattn(zp["cq"], mk, mv, row0=0, batch=batch, seq=seq, tq=512)
    c_s = _cattn(zs["cq"], cache_mem_k[l].reshape(dbatch, MEM_LEN, C_WIDTH),
                 cache_mem_v[l].reshape(dbatch, MEM_LEN, C_WIDTH), row0=0, batch=dbatch, seq=dseq, tq=dseq)

    w_br = w_branch[l].astype(BF16)
    route_w = (w_out[l].astype(BF16), ln1_g[l].reshape(1, D_MODEL), ln1_b[l].reshape(1, D_MODEL),
               w_router[l].T.astype(BF16), router_bias[l].reshape(N_EXPERTS, 1).astype(F32))
    segs = []
    cnt = jnp.zeros((N_EXPERTS, 128), jnp.int32)
    for tag, x2, abc, gate, tm_r in (("p", xp2, (a_p, b_p, c_p), zp["gate"], 512), ("s", xs2, (a_s, b_s, c_s), zs["gate"], n_s)):
        h = _merge(*abc, gate, w_br, tm=tm_r, name=f"merge_{tag}")
        x1, x1p, idx, wts, pos, cnt = _out_ln_route(x2, h, *route_w, cnt, tm=tm_r, name=f"out_ln_route_{tag}")
        segs.append(dict(tag=tag, x1=x1, x1p=x1p, idx=idx, wts=wts, pos=pos, tm=tm_r, rows=x2.shape[0]))

    m = MOE_BLOCK
    n_pairs = n * TOP_K
    n_blocks = (n_pairs + m - 1) // m + N_EXPERTS
    counts = cnt[:, 0]
    padded = (counts + m - 1) // m * m
    pad_end = jnp.cumsum(padded)
    pad_start = (pad_end - padded).astype(jnp.int32)
    n_valid = pad_end[-1:] // m
    e_ids = jnp.arange(N_EXPERTS, dtype=jnp.int32)
    blk_first = jnp.arange(n_blocks, dtype=jnp.int32) * m
    blk_e = jnp.sum((blk_first[:, None] >= pad_end[None, :]).astype(jnp.int32), axis=1)
    blk_e = jnp.minimum(blk_e, N_EXPERTS - 1)
    row_end = jnp.sum(jnp.where(blk_e[:, None] == e_ids, (pad_start + counts.astype(jnp.int32))[None, :], 0), axis=1)
    blk_rows = jnp.clip(row_end - blk_first, 0, m).astype(jnp.int32)
    owns = counts > 0
    later = (e_ids[None, :] > e_ids[:, None]) & owns[None, :]
    next_owner = jnp.min(jnp.where(later, e_ids[None, :], N_EXPERTS), axis=1)
    next_owner = jnp.where(next_owner < N_EXPERTS, next_owner, -1)
    owner_rank = jnp.cumsum(owns.astype(jnp.int32)) - 1
    nxt_e = jnp.sum(jnp.where(blk_e[:, None] == e_ids, next_owner[None, :], 0), axis=1).astype(jnp.int32)
    slot = (jnp.sum(jnp.where(blk_e[:, None] == e_ids, owner_rank[None, :], 0), axis=1) % 2).astype(jnp.int32)
    z_pad = jnp.where(padded > counts, pad_end - m, -1)
    z_tail = jnp.where(blk_first >= pad_end[-1], blk_first, -1)
    zrows = jnp.concatenate([z_pad, z_tail]).astype(jnp.int32)

    def retile(dest, rows, tm_from, tm_to):
        d = dest.reshape(rows // tm_from, TOP_K, tm_from // tm_to, tm_to)
        return jnp.transpose(d, (0, 2, 1, 3)).reshape(rows // tm_to, TOP_K, tm_to)

    tm_d = 256
    for seg in segs:
        first_row = jnp.sum(jnp.where(seg["idx"][..., None] == e_ids, pad_start, 0), axis=-1)
        seg["dest"] = first_row + seg["pos"]
    dest_d = jnp.concatenate([retile(seg["dest"], seg["rows"], seg["tm"], tm_d) for seg in segs], axis=0)
    xs = _dispatch(zrows, dest_d, segs[0]["x1p"], segs[1]["x1p"], n_rows=n_blocks * m, tm=tm_d, block_rows=m)
    yb = _moe(xs, blk_e, n_valid.astype(jnp.int32), blk_rows, nxt_e, slot, w_exp_gate[l].astype(F32), w_exp_up[l].astype(F32),
              w_exp_down[l].astype(F32))
    tm_f = 256
    shared_w = (w_sh_gate[l].astype(BF16), w_sh_up[l].astype(BF16), w_sh_down[l].astype(BF16),
                ln2_g[l].reshape(1, D_MODEL), ln2_b[l].reshape(1, D_MODEL))
    ys = []
    for seg in segs:
        rows = seg["rows"]
        wts_t = jnp.transpose(seg["wts"], (0, 2, 1)).reshape(rows, TOP_K)
        ys.append(_final(retile(seg["dest"], rows, seg["tm"], tm_f), seg["x1"], wts_t, yb, *shared_w, tm=tm_f,
                         name=f"combine_shared_ln2_{seg['tag']}"))
    y_p = ys[0].reshape(batch, seq, D_MODEL)
    y_s = ys[1].reshape(dbatch, dseq, D_MODEL)

    kv_p = zp["akv"].reshape(batch, seq, 2, A_KV_HEADS, A_HEAD_DIM)[:, -lc:]
    k_p, v_p = kv_p[:, :, 0], kv_p[:, :, 1]
    kv_s = zs["akv"].reshape(dbatch, dseq, 2, A_KV_HEADS, A_HEAD_DIM)
    k_s, v_s = kv_s[:, :, 0], kv_s[:, :, 1]
    wk_s = jnp.concatenate([cache_win_k[l].astype(F32), k_s], axis=1)[:, -lc:]
    wv_s = jnp.concatenate([cache_win_v[l].astype(F32), v_s], axis=1)[:, -lc:]
    mk_o = mk.reshape(batch, MEM_LEN, C_HEADS, C_HEAD_DIM)
    mv_o = mv.reshape(batch, MEM_LEN, C_HEADS, C_HEAD_DIM)
    return (y_p, y_s, k_p[None], v_p[None], hs_p[None], mk_o[None], mv_o[None], wk_s[None], wv_s[None], hs_s[None])
```

```python
import functools

import jax
import jax.numpy as jnp
import numpy as np
from jax import lax
from jax.experimental import pallas as pl
from jax.experimental.pallas import tpu as pltpu

F32 = jnp.float32
BF16 = jnp.bfloat16

D_MODEL = 2048
DEPTH = 1
PAST_LEN = 2048
CHUNK = 64
A_HEADS = 16
A_KV_HEADS = 4
A_GROUP = A_HEADS // A_KV_HEADS
A_HEAD_DIM = 64
A_WIDTH = A_HEADS * A_HEAD_DIM
A_KV_WIDTH = A_KV_HEADS * A_HEAD_DIM
WINDOW = 128
WIN_CHUNKS = WINDOW // CHUNK
B_HEADS = 8
B_KEY_DIM = 128
B_VAL_DIM = 128
B_WIDTH = B_HEADS * B_VAL_DIM
SUB = 16
MEM_LEN = 256
C_HEADS = 4
C_HEAD_DIM = 256
C_WIDTH = C_HEADS * C_HEAD_DIM
N_EXPERTS = 64
N_GROUPS = 8
GROUP_SIZE = N_EXPERTS // N_GROUPS
TOPK_GROUPS = 4
TOP_K = 8
EXPERT_DIM = 512
ROUTED_SCALE = 2.5
DN_ALPHA = (2 * DEPTH) ** 0.25
LN_EPS = 1e-5
RMS_EPS = 1e-6

MOE_BLOCK = 512
VMEM_LIMIT = 56 * 1024 * 1024


def _cparams(sem):
    return pltpu.CompilerParams(dimension_semantics=sem, vmem_limit_bytes=VMEM_LIMIT)


def _resident(shape, index_map):
    return pl.BlockSpec(shape, index_map, pipeline_mode=pl.Buffered(1))


def _sigmoid(x):
    return 1.0 / (1.0 + jnp.exp(-x))


def _layer_norm(x, g, b):
    mu = jnp.mean(x, axis=-1, keepdims=True)
    xc = x - mu
    var = jnp.mean(xc * xc, axis=-1, keepdims=True)
    return xc * lax.rsqrt(var + LN_EPS) * g + b


def _mm_kernel(x_ref, w_ref, o_ref, *, scale):
    acc = jnp.dot(x_ref[...].astype(BF16), w_ref[...], preferred_element_type=F32)
    if scale != 1.0:
        acc = acc * scale
    o_ref[...] = acc.astype(o_ref.dtype)


def _matmul(x, w, *, tm, tn, out_dtype, scale=1.0, name):
    m, k = x.shape
    n = w.shape[1]
    return pl.pallas_call(
        functools.partial(_mm_kernel, scale=scale),
        grid=(m // tm, n // tn),
        in_specs=[pl.BlockSpec((tm, k), lambda i, j: (i, 0)), pl.BlockSpec((k, tn), lambda i, j: (0, j))],
        out_specs=pl.BlockSpec((tm, tn), lambda i, j: (i, j)),
        out_shape=jax.ShapeDtypeStruct((m, n), out_dtype),
        compiler_params=_cparams(("parallel", "parallel")),
        name=name,
    )(x, w)


SWA_KEYS = 256
HEAD_PAIR = 2 * A_HEAD_DIM


def _kv_head_planes(x):
    lane = lax.broadcasted_iota(jnp.int32, (1, HEAD_PAIR), 1)
    low = lane < A_HEAD_DIM
    planes = []
    for pair in range(A_KV_HEADS // 2):
        own = x[:, pair * HEAD_PAIR:(pair + 1) * HEAD_PAIR]
        swapped = pltpu.roll(own, A_HEAD_DIM, 1)
        planes.append((jnp.where(low, own, 0.0).astype(BF16), jnp.where(low, 0.0, swapped).astype(BF16)))
        planes.append((jnp.where(low, swapped, 0.0).astype(BF16), jnp.where(low, 0.0, own).astype(BF16)))
    return planes


SWA_LOOKAHEAD = 8


def _swa_chunks(q_ref, rows, chunk_rows, k_planes, v_planes, n_band, bias_ref, valid_of, o_ref):
    zeros_k = jnp.zeros((SWA_KEYS - n_band, HEAD_PAIR), BF16)
    ones_v = jnp.ones((SWA_KEYS, HEAD_PAIR), BF16)
    tiles = [(c, h, x) for c in range(len(chunk_rows)) for h in range(A_KV_HEADS) for x in range(2)]

    def band(plane, c):
        return jnp.concatenate([plane[chunk_rows[c]:chunk_rows[c] + n_band], zeros_k], axis=0)

    def scores(t):
        c, h, x = tiles[t]
        q2 = jnp.concatenate([q_ref[c * rows:(c + 1) * rows, (2 * h + j) * HEAD_PAIR:(2 * h + j + 1) * HEAD_PAIR]
                              for j in range(2)], axis=0).astype(BF16)
        s = lax.dot_general(q2, band(k_planes[h][x], c), (((1,), (1,)), ((), ())), preferred_element_type=F32)
        s = s - bias_ref[h, x]
        valid = valid_of(c)
        return s if valid is None else jnp.where(valid, s, -jnp.inf)

    pending = {t: scores(t) for t in range(min(SWA_LOOKAHEAD, len(tiles)))}
    even = None
    for t, (c, h, x) in enumerate(tiles):
        s = pending.pop(t)
        p = jnp.exp(s - jnp.max(s, axis=-1, keepdims=True)).astype(BF16)
        if t + SWA_LOOKAHEAD < len(tiles):
            pending[t + SWA_LOOKAHEAD] = scores(t + SWA_LOOKAHEAD)
        vw = jnp.concatenate([band(v_planes[h][x], c), ones_v], axis=1)
        o = jnp.dot(p, vw, preferred_element_type=F32)
        o = o[:, :HEAD_PAIR] / o[:, HEAD_PAIR:]
        if x == 0:
            even = o
        else:
            out = even + o
            for j in range(2):
                o_ref[c * rows:(c + 1) * rows, (2 * h + j) * HEAD_PAIR:(2 * h + j + 1) * HEAD_PAIR] = (
                    out[j * rows:(j + 1) * rows].astype(o_ref.dtype))


def _swa_prompt_kernel(q_ref, kc_ref, kp_ref, vc_ref, vp_ref, bias_ref, o_ref, *, n_chunks):
    i = pl.program_id(1)
    pad = WIN_CHUNKS * CHUNK
    n_band = pad + CHUNK
    k_planes = _kv_head_planes(jnp.concatenate([kp_ref[...], kc_ref[...]], axis=0))
    v_planes = _kv_head_planes(jnp.concatenate([vp_ref[...], vc_ref[...]], axis=0))
    s_idx = lax.broadcasted_iota(jnp.int32, (1, SWA_KEYS), 1)

    def valid_of(c):
        return jnp.logical_or(s_idx + (i * (n_chunks * CHUNK) + c * CHUNK - pad) >= 0, s_idx >= n_band)

    _swa_chunks(q_ref, CHUNK, [c * CHUNK for c in range(n_chunks)], k_planes, v_planes, n_band, bias_ref, valid_of,
                o_ref)


def _swa_sample_kernel(q_ref, kn_ref, kc_ref, vn_ref, vc_ref, bias_ref, o_ref):
    k_planes = _kv_head_planes(jnp.concatenate([kc_ref[0], kn_ref[...]], axis=0))
    v_planes = _kv_head_planes(jnp.concatenate([vc_ref[0], vn_ref[...]], axis=0))
    n_band = kc_ref.shape[1] + kn_ref.shape[0]
    _swa_chunks(q_ref, q_ref.shape[0], [0], k_planes, v_planes, n_band, bias_ref, lambda c: None, o_ref)


def _alibi_slopes():
    return (2.0 ** (-8.0 * np.arange(1, A_HEADS + 1) / A_HEADS)).astype(np.float32)


def _swa_bias(q_pos, k_pos, valid, sinks):
    n_q, n_k = len(q_pos), len(k_pos)
    assert n_k < SWA_KEYS
    dist = np.abs(q_pos[:, None] - k_pos[None, :]).astype(np.float32)
    band = _alibi_slopes()[:, None, None] * dist[None]
    if valid is not None:
        band = np.where(valid[None], band, np.inf)
    pad = np.full((A_HEADS, n_q, SWA_KEYS - n_k - 1), np.inf, np.float32)
    sink = jnp.broadcast_to(-sinks.astype(F32)[:, None, None], (A_HEADS, n_q, 1))
    bias = jnp.concatenate([jnp.asarray(band.astype(np.float32)), sink, jnp.asarray(pad)], axis=2)
    bias = bias.reshape(A_KV_HEADS, 2, 2, n_q, SWA_KEYS).transpose(0, 2, 1, 3, 4)
    return bias.reshape(A_KV_HEADS, 2, 2 * n_q, SWA_KEYS)


def _swa_prompt(q, kv, sinks, *, batch, seq, n_chunks=4):
    tq = n_chunks * CHUNK
    pad = WIN_CHUNKS * CHUNK
    nb = seq // tq
    bias = _swa_bias(pad + np.arange(CHUNK), np.arange(pad + CHUNK), None, sinks)
    prev_per_blk = tq // pad

    def cur(b, i, col=0):
        return (b * nb + i, col)

    def prev(b, i, col=0):
        return (jnp.maximum((b * nb + i) * prev_per_blk - 1, b * nb * prev_per_blk), col)

    return pl.pallas_call(
        functools.partial(_swa_prompt_kernel, n_chunks=n_chunks),
        grid=(batch, nb),
        in_specs=[
            pl.BlockSpec((tq, A_WIDTH), cur),
            pl.BlockSpec((tq, A_KV_WIDTH), cur),
            pl.BlockSpec((pad, A_KV_WIDTH), prev),
            pl.BlockSpec((tq, A_KV_WIDTH), functools.partial(cur, col=1)),
            pl.BlockSpec((pad, A_KV_WIDTH), functools.partial(prev, col=1)),
            pl.BlockSpec(bias.shape, lambda b, i: (0, 0, 0, 0)),
        ],
        out_specs=pl.BlockSpec((tq, A_WIDTH), cur),
        out_shape=jax.ShapeDtypeStruct((batch * seq, A_WIDTH), BF16),
        compiler_params=_cparams(("parallel", "arbitrary")),
        name="swa_prompt",
    )(q, kv, kv, kv, kv, bias)


def _swa_sample(q, kv, k_cache, v_cache, sinks, *, row0, batch, seq):
    lc = k_cache.shape[1]
    q_pos = PAST_LEN + np.arange(seq)
    k_pos = PAST_LEN - lc + np.arange(lc + seq)
    cdiff = q_pos[:, None] // CHUNK - k_pos[None, :] // CHUNK
    valid = (cdiff >= 0) & (cdiff <= WIN_CHUNKS)
    bias = _swa_bias(q_pos, k_pos, valid, sinks)
    blk0 = row0 // seq

    def rows(b):
        return (blk0 + b, 0)

    return pl.pallas_call(
        _swa_sample_kernel,
        grid=(batch,),
        in_specs=[
            pl.BlockSpec((seq, A_WIDTH), rows),
            pl.BlockSpec((seq, A_KV_WIDTH), rows),
            pl.BlockSpec((1, lc, A_KV_WIDTH), lambda b: (b, 0, 0)),
            pl.BlockSpec((seq, A_KV_WIDTH), lambda b: (blk0 + b, 1)),
            pl.BlockSpec((1, lc, A_KV_WIDTH), lambda b: (b, 0, 0)),
            pl.BlockSpec(bias.shape, lambda b: (0, 0, 0, 0)),
        ],
        out_specs=pl.BlockSpec((seq, A_WIDTH), lambda b: (b, 0)),
        out_shape=jax.ShapeDtypeStruct((batch * seq, A_WIDTH), BF16),
        compiler_params=_cparams(("parallel",)),
        name="swa_sample",
    )(q, kv, k_cache, kv, v_cache, bias)


def _split3(x):
    hi = x.astype(BF16)
    r1 = x - hi.astype(F32)
    mid = r1.astype(BF16)
    lo = (r1 - mid.astype(F32)).astype(BF16)
    return hi, mid, lo


def _dot3(mat, parts):
    acc = jnp.dot(mat, parts[0], preferred_element_type=F32)
    acc = acc + jnp.dot(mat, parts[1], preferred_element_type=F32)
    return acc + jnp.dot(mat, parts[2], preferred_element_type=F32)


def _hgrn_kernel(zq_ref, zf_ref, zi_ref, zg_ref, lb_ref, ng_ref, tri_ref, tsel_ref, s0_ref, o_ref, sfin_ref, s_scr,
                 *, blk, nbat, ntime):
    j = pl.program_id(1)

    @pl.when(j == 0)
    def _():
        s_scr[...] = s0_ref[...]

    log_lb = lb_ref[0:1, :]
    log1m_lb = lb_ref[1:2, :]
    one_m_lb = lb_ref[2:3, :]
    row = lax.broadcasted_iota(jnp.int32, (blk, 1), 0)
    n_sub = blk // SUB
    tril = lax.broadcasted_iota(jnp.int32, (blk, blk), 0) >= lax.broadcasted_iota(jnp.int32, (blk, blk), 1)
    heads = [slice(h * B_KEY_DIM, (h + 1) * B_KEY_DIM) for h in range(B_HEADS)]

    def prepare(bi, ti):
        rows = slice(ti * blk, (ti + 1) * blk)
        bq = zq_ref[bi, rows, :]
        fl = zf_ref[bi, rows, :]
        q = bq * _sigmoid(bq) * (B_KEY_DIM ** -0.5)
        log_sig = jnp.minimum(fl, 0.0) - jnp.log1p(jnp.exp(-jnp.abs(fl)))
        c = log1m_lb + log_sig
        logf = jnp.maximum(log_lb, c) + jnp.log1p(jnp.exp(-jnp.abs(log_lb - c)))
        k = one_m_lb * _sigmoid(-fl)
        parts = _split3(logf)
        b = _dot3(tri_ref[...], parts)
        rq = _dot3(tsel_ref[...], parts)
        qt = (q * jnp.exp(b - rq)).astype(BF16)
        qb = (q * jnp.exp(b)).astype(BF16)
        b_last = b[blk - 1:blk, :]
        khat = (k * jnp.exp(b_last - b)).astype(BF16)
        vb = zi_ref[bi, rows, :].astype(BF16)
        kts = []
        for i in range(n_sub):
            r_i = rq[i * SUB:i * SUB + 1, :]
            kts.append(jnp.where(row < (i + 1) * SUB, k * jnp.exp(r_i - b), 0.0).astype(BF16))
        a_raw, ds = [], []
        for h, hs in enumerate(heads):
            a_raw.append(jnp.concatenate(
                [lax.dot_general(qt[i * SUB:(i + 1) * SUB, hs], kts[i][:, hs], (((1,), (1,)), ((), ())),
                                 preferred_element_type=F32) for i in range(n_sub)], axis=0))
            ds.append(lax.dot_general(khat[:, hs], vb[:, hs], (((0,), (0,)), ((), ())), preferred_element_type=F32))
        return dict(a_raw=a_raw, ds=ds, qb=qb, vb=vb, e_last=jnp.exp(b_last), rows=rows)

    def finish(bi, p):
        s_old = [s_scr[bi, h] for h in range(B_HEADS)]
        o_state = [jnp.dot(p["qb"][:, hs], s_old[h].astype(BF16), preferred_element_type=F32)
                   for h, hs in enumerate(heads)]
        outs = []
        for h, hs in enumerate(heads):
            a = jnp.where(tril, p["a_raw"][h], 0.0).astype(BF16)
            o = jnp.dot(a, p["vb"][:, hs], preferred_element_type=F32) + o_state[h]
            decay = jnp.transpose(jnp.broadcast_to(p["e_last"][:, hs], (B_KEY_DIM, B_KEY_DIM)))
            s_scr[bi, h] = decay * s_old[h] + p["ds"][h]
            outs.append(o * lax.rsqrt(jnp.mean(o * o, axis=-1, keepdims=True) + RMS_EPS))
        bg = zg_ref[bi, p["rows"], :]
        o_all = jnp.concatenate(outs, axis=1) * ng_ref[...] * (bg * _sigmoid(bg))
        o_ref[bi, p["rows"], :] = o_all.astype(o_ref.dtype)

    prepared = {(bi, ti): prepare(bi, ti) for ti in range(ntime) for bi in range(nbat)}
    for ti in range(ntime):
        for bi in range(nbat):
            finish(bi, prepared[bi, ti])

    @pl.when(j == pl.num_programs(1) - 1)
    def _():
        sfin_ref[...] = s_scr[...]


def _hgrn(zb, lbp, ng, s0, *, batch, seq, blk, nbat, ntime):
    nb = seq // (blk * ntime)
    t = np.arange(blk)
    tri = jnp.asarray((t[:, None] >= t[None, :]).astype(np.float32), dtype=BF16)
    tsel = jnp.asarray((t[None, :] < (t[:, None] // SUB) * SUB).astype(np.float32), dtype=BF16)
    z3 = zb.reshape(batch, seq, 4 * B_WIDTH)

    def zspec(col):
        return pl.BlockSpec((nbat, blk * ntime, B_WIDTH), lambda b, j: (b, j, col))

    state_spec = pl.BlockSpec((nbat, B_HEADS, B_KEY_DIM, B_VAL_DIM), lambda b, j: (b, 0, 0, 0))
    out, s_fin = pl.pallas_call(
        functools.partial(_hgrn_kernel, blk=blk, nbat=nbat, ntime=ntime),
        grid=(batch // nbat, nb),
        in_specs=[
            zspec(0), zspec(1), zspec(2), zspec(3),
            pl.BlockSpec((3, B_WIDTH), lambda b, j: (0, 0)),
            pl.BlockSpec((1, B_WIDTH), lambda b, j: (0, 0)),
            pl.BlockSpec((blk, blk), lambda b, j: (0, 0)),
            pl.BlockSpec((blk, blk), lambda b, j: (0, 0)),
            state_spec,
        ],
        out_specs=[pl.BlockSpec((nbat, blk * ntime, B_WIDTH), lambda b, j: (b, j, 0)), state_spec],
        out_shape=[jax.ShapeDtypeStruct((batch, seq, B_WIDTH), BF16),
                   jax.ShapeDtypeStruct((batch, B_HEADS, B_KEY_DIM, B_VAL_DIM), F32)],
        scratch_shapes=[pltpu.VMEM((nbat, B_HEADS, B_KEY_DIM, B_VAL_DIM), F32)],
        compiler_params=_cparams(("parallel", "arbitrary")),
        name=f"hgrn_blk{blk}",
    )(z3, z3, z3, z3, lbp, ng, tri, tsel, s0)
    return out.reshape(batch * seq, B_WIDTH), s_fin


def _cattn_kernel(q_ref, mk_ref, mv_ref, o_ref):
    mk = mk_ref[0].astype(BF16)
    mv = mv_ref[0].astype(BF16)
    q = q_ref[...]
    heads = [slice(h * C_HEAD_DIM, (h + 1) * C_HEAD_DIM) for h in range(C_HEADS)]
    scores = [lax.dot_general(q[:, hs], mk[:, hs], (((1,), (1,)), ((), ())), preferred_element_type=F32)
              for hs in heads]
    for hs, s in zip(heads, scores):
        p = jnp.exp(s - jnp.max(s, axis=-1, keepdims=True))
        p = p / jnp.sum(p, axis=-1, keepdims=True)
        o_ref[:, hs] = jnp.dot(p.astype(BF16), mv[:, hs], preferred_element_type=F32).astype(o_ref.dtype)


def _cattn(q, mk, mv, *, row0, batch, seq, tq):
    nb = seq // tq
    blk0 = row0 // tq
    mem_spec = pl.BlockSpec((1, MEM_LEN, C_WIDTH), lambda b, i: (b, 0, 0))
    return pl.pallas_call(
        _cattn_kernel,
        grid=(batch, nb),
        in_specs=[pl.BlockSpec((tq, C_WIDTH), lambda b, i: (blk0 + b * nb + i, 0)), mem_spec, mem_spec],
        out_specs=pl.BlockSpec((tq, C_WIDTH), lambda b, i: (b * nb + i, 0)),
        out_shape=jax.ShapeDtypeStruct((batch * seq, C_WIDTH), BF16),
        compiler_params=_cparams(("parallel", "parallel")),
        name=f"cattn_tq{tq}",
    )(q, mk, mv)


def _merge_kernel(a_ref, b_ref, c_ref, g0_ref, g1_ref, g2_ref, pa_ref, pb_ref, pc_ref, o_ref):
    da = jnp.dot(a_ref[...], pa_ref[...], preferred_element_type=F32)
    db = jnp.dot(b_ref[...], pb_ref[...], preferred_element_type=F32)
    dc = jnp.dot(c_ref[...], pc_ref[...], preferred_element_type=F32)
    h = _sigmoid(g0_ref[...]) * da + _sigmoid(g1_ref[...]) * db + _sigmoid(g2_ref[...]) * dc
    o_ref[...] = h.astype(o_ref.dtype)


def _merge(a, b, c, gl, p, *, tm, name):
    n, width = a.shape

    def rows(i):
        return (i, 0)

    return pl.pallas_call(
        _merge_kernel,
        grid=(n // tm,),
        in_specs=[
            pl.BlockSpec((tm, width), rows), pl.BlockSpec((tm, width), rows), pl.BlockSpec((tm, width), rows),
            pl.BlockSpec((tm, D_MODEL), lambda i: (i, 0)),
            pl.BlockSpec((tm, D_MODEL), lambda i: (i, 1)),
            pl.BlockSpec((tm, D_MODEL), lambda i: (i, 2)),
            _resident((width, D_MODEL), lambda i: (0, 0)),
            _resident((width, D_MODEL), lambda i: (1, 0)),
            _resident((width, D_MODEL), lambda i: (2, 0)),
        ],
        out_specs=pl.BlockSpec((tm, D_MODEL), rows),
        out_shape=jax.ShapeDtypeStruct((n, D_MODEL), BF16),
        compiler_params=_cparams(("parallel",)),
        name=name,
    )(a, b, c, gl, gl, gl, p, p, p)


def _first_index(hit_src, m, iota, size, axis):
    return jnp.min(jnp.where(hit_src == m, iota, size), axis=axis, keepdims=True)


def _route(x1, wr_ref, rb_ref, upper_ref, run_ref, idx_ref, wts_ref, pos_ref):
    tm = x1.shape[0]
    logits = lax.dot_general(wr_ref[...], x1.astype(BF16), (((1,), (1,)), ((), ())),
                             preferred_element_type=F32)
    scores = _sigmoid(logits)
    choice = scores + rb_ref[...]
    g3 = choice.reshape(N_GROUPS, GROUP_SIZE, tm)
    mem_iota = lax.broadcasted_iota(jnp.int32, g3.shape, 1)
    m1 = jnp.max(g3, axis=1, keepdims=True)
    first = _first_index(g3, m1, mem_iota, GROUP_SIZE, 1)
    m2 = jnp.max(jnp.where(mem_iota == first, -jnp.inf, g3), axis=1, keepdims=True)
    gscore = (m1 + m2).reshape(N_GROUPS, tm)
    g_iota = lax.broadcasted_iota(jnp.int32, gscore.shape, 0)
    gsel = jnp.zeros(gscore.shape, F32)
    cur = gscore
    for _ in range(TOPK_GROUPS):
        m = jnp.max(cur, axis=0, keepdims=True)
        hit = g_iota == _first_index(cur, m, g_iota, N_GROUPS, 0)
        gsel = jnp.where(hit, 1.0, gsel)
        cur = jnp.where(hit, -jnp.inf, cur)
    masked = jnp.where(gsel.reshape(N_GROUPS, 1, tm) > 0.5, g3, -jnp.inf).reshape(N_EXPERTS, tm)
    e_iota = lax.broadcasted_iota(jnp.int32, masked.shape, 0)
    sel = jnp.zeros(masked.shape, F32)
    cur = masked
    idxs, ws = [], []
    for _ in range(TOP_K):
        m = jnp.max(cur, axis=0, keepdims=True)
        ei = _first_index(cur, m, e_iota, N_EXPERTS, 0)
        hit = e_iota == ei
        idxs.append(ei)
        ws.append(jnp.sum(jnp.where(hit, scores, 0.0), axis=0, keepdims=True))
        sel = jnp.where(hit, 1.0, sel)
        cur = jnp.where(hit, -jnp.inf, cur)
    w = jnp.concatenate(ws, axis=0)
    w = w / jnp.sum(w, axis=0, keepdims=True) * ROUTED_SCALE
    excl = jnp.dot(sel.astype(BF16), upper_ref[...], preferred_element_type=F32)
    posfull = excl + run_ref[:, 0:1]
    run_ref[...] = run_ref[...] + jnp.sum(sel, axis=1, keepdims=True)
    pos = [jnp.sum(jnp.where(e_iota == ei, posfull, 0.0), axis=0, keepdims=True) for ei in idxs]
    idx_ref[0] = jnp.concatenate(idxs, axis=0)
    wts_ref[0] = w
    pos_ref[0] = jnp.concatenate(pos, axis=0).astype(jnp.int32)


def _pack_halves(x):
    c = x.shape[1] // 2
    lo = lax.bitcast_convert_type(x[:, :c].astype(BF16).astype(F32), jnp.uint32)
    hi = lax.bitcast_convert_type(x[:, c:].astype(BF16).astype(F32), jnp.uint32)
    return (hi & jnp.uint32(0xFFFF0000)) | (lo >> 16)


def _unpack_halves(w):
    lo = lax.bitcast_convert_type(w << 16, F32)
    hi = lax.bitcast_convert_type(w & jnp.uint32(0xFFFF0000), F32)
    return lo, hi


ROW_TILE = 8
PACKED_LANES = D_MODEL // 2 // ROW_TILE


def _store_row_tiles(ref, val):
    rows = val.shape[0]
    for s in range(ROW_TILE):
        ref[pl.ds(s, rows, stride=ROW_TILE), :] = val[:, s * PACKED_LANES:(s + 1) * PACKED_LANES]


def _load_row_tiles(ref, rows):
    return [ref[pl.ds(s, rows, stride=ROW_TILE), :] for s in range(ROW_TILE)]


def _out_ln_route_kernel(x_ref, h_ref, wo_ref, g_ref, b_ref, wr_ref, rb_ref, upper_ref, run0_ref,
                         x1_ref, x1p_ref, idx_ref, wts_ref, pos_ref, cnt_ref, run_ref):
    @pl.when(pl.program_id(0) == 0)
    def _():
        run_ref[...] = run0_ref[...].astype(F32)

    y = DN_ALPHA * x_ref[...] + jnp.dot(h_ref[...], wo_ref[...], preferred_element_type=F32)
    x1 = _layer_norm(y, g_ref[...], b_ref[...])
    x1_ref[...] = x1
    _store_row_tiles(x1p_ref, _pack_halves(x1))
    _route(x1, wr_ref, rb_ref, upper_ref, run_ref, idx_ref, wts_ref, pos_ref)
    cnt_ref[...] = run_ref[...].astype(jnp.int32)


def _out_ln_route(x, h, w_out, ln_g, ln_b, w_router_t, router_bias, run0, *, tm, name):
    n = h.shape[0]
    nt = n // tm
    t = np.arange(tm)
    upper = jnp.asarray((t[:, None] < t[None, :]).astype(np.float32), dtype=BF16)

    def rows(i):
        return (i, 0)

    def const(i):
        return (0, 0)

    small = pl.BlockSpec((1, TOP_K, tm), lambda i: (i, 0, 0))
    return pl.pallas_call(
        _out_ln_route_kernel,
        grid=(nt,),
        in_specs=[
            pl.BlockSpec((tm, D_MODEL), rows),
            pl.BlockSpec((tm, D_MODEL), rows),
            _resident((D_MODEL, D_MODEL), const),
            pl.BlockSpec((1, D_MODEL), const),
            pl.BlockSpec((1, D_MODEL), const),
            pl.BlockSpec((N_EXPERTS, D_MODEL), const),
            pl.BlockSpec((N_EXPERTS, 1), const),
            pl.BlockSpec((tm, tm), const),
            pl.BlockSpec((N_EXPERTS, 128), const),
        ],
        out_specs=[pl.BlockSpec((tm, D_MODEL), rows), pl.BlockSpec((tm * ROW_TILE, PACKED_LANES), rows), small, small,
                   small, pl.BlockSpec((N_EXPERTS, 128), const)],
        out_shape=[
            jax.ShapeDtypeStruct((n, D_MODEL), F32),
            jax.ShapeDtypeStruct((n * ROW_TILE, PACKED_LANES), jnp.uint32),
            jax.ShapeDtypeStruct((nt, TOP_K, tm), jnp.int32),
            jax.ShapeDtypeStruct((nt, TOP_K, tm), F32),
            jax.ShapeDtypeStruct((nt, TOP_K, tm), jnp.int32),
            jax.ShapeDtypeStruct((N_EXPERTS, 128), jnp.int32),
        ],
        scratch_shapes=[pltpu.VMEM((N_EXPERTS, 128), F32)],
        compiler_params=_cparams(("arbitrary",)),
        name=name,
    )(x, h, w_out, ln_g, ln_b, w_router_t, router_bias, upper, run0)


def _row_copy(src_hbm, row, dst, dst_row, sem):
    return pltpu.make_async_copy(src_hbm.at[pl.ds(row, 1)], dst.at[pl.ds(dst_row, 1)], sem)


def _tile_copy(src, src_row, dst, dst_row, sem):
    def tile(row):
        first = row * ROW_TILE
        return pl.ds(first if isinstance(row, int) else pl.multiple_of(first, ROW_TILE), ROW_TILE)

    return pltpu.make_async_copy(src.at[tile(src_row)], dst.at[tile(dst_row)], sem)


def _dispatch_kernel(zrow_ref, dest_ref, xa_ref, xb_ref, o_hbm, zbuf, sem, zsem, *, block_rows, n_first):
    tm = xa_ref.shape[0] // ROW_TILE
    n_zero = zrow_ref.shape[0]

    def zero_copy(b):
        start = pl.multiple_of(jnp.maximum(zrow_ref[b], 0) * ROW_TILE, ROW_TILE)
        return pltpu.make_async_copy(zbuf, o_hbm.at[pl.ds(start, block_rows * ROW_TILE)], zsem)

    def zero_unowned_rows():
        zbuf[...] = jnp.zeros_like(zbuf)

        def start(b, carry):
            @pl.when(zrow_ref[b] >= 0)
            def _():
                zero_copy(b).start()
            return carry

        def wait(b, carry):
            @pl.when(zrow_ref[b] >= 0)
            def _():
                zero_copy(b).wait()
            return carry

        lax.fori_loop(0, n_zero, start, 0)
        lax.fori_loop(0, n_zero, wait, 0)

    pl.when(pl.program_id(0) == 0)(zero_unowned_rows)

    def scatter_tile(x_ref):
        for t in range(tm):
            for k in range(TOP_K):
                _tile_copy(x_ref, t, o_hbm, dest_ref[0, k, t], sem).start(priority=k % 2)
        for k in range(TOP_K):
            pltpu.make_async_copy(x_ref, o_hbm.at[pl.ds(0, tm * ROW_TILE)], sem).wait()

    pl.when(pl.program_id(0) < n_first)(functools.partial(scatter_tile, xa_ref))
    pl.when(pl.program_id(0) >= n_first)(functools.partial(scatter_tile, xb_ref))


def _dispatch(zrows, dest_blk, x1p_a, x1p_b, *, n_rows, tm, block_rows):
    nf = x1p_a.shape[0] // ROW_TILE // tm
    ns = x1p_b.shape[0] // ROW_TILE // tm
    grid_spec = pltpu.PrefetchScalarGridSpec(
        num_scalar_prefetch=1,
        grid=(nf + ns,),
        in_specs=[
            pl.BlockSpec((1, TOP_K, tm), lambda i, z: (i, 0, 0), memory_space=pltpu.SMEM),
            pl.BlockSpec((tm * ROW_TILE, PACKED_LANES), lambda i, z: (jnp.minimum(i, nf - 1), 0)),
            pl.BlockSpec((tm * ROW_TILE, PACKED_LANES), lambda i, z: (jnp.clip(i - nf, 0, ns - 1), 0)),
        ],
        out_specs=pl.BlockSpec(memory_space=pl.ANY),
        scratch_shapes=[pltpu.VMEM((block_rows * ROW_TILE, PACKED_LANES), jnp.uint32),
                        pltpu.SemaphoreType.DMA(()), pltpu.SemaphoreType.DMA(())],
    )
    return pl.pallas_call(
        functools.partial(_dispatch_kernel, block_rows=block_rows, n_first=nf),
        grid_spec=grid_spec,
        out_shape=jax.ShapeDtypeStruct((n_rows * ROW_TILE, PACKED_LANES), jnp.uint32),
        compiler_params=_cparams(("arbitrary",)),
        name="dispatch",
    )(zrows, dest_blk, x1p_a, x1p_b)


def _moe_kernel(be_ref, nv_ref, nr_ref, nxt_ref, slot_ref, x_ref, wg_hbm, wu_hbm, wd_hbm, o_ref,
                wg_f, wu_f, wd_f, wg_b, wu_b, wd_b, wsem):
    k = pl.program_id(0)
    m = x_ref.shape[0] // ROW_TILE
    half = m // 2
    live = k < nv_ref[0]
    upper_half_used = nr_ref[k] > half
    new_expert = jnp.logical_or(k == 0, be_ref[k] != be_ref[jnp.maximum(k - 1, 0)])

    def weight_copies(e, slot):
        return [pltpu.make_async_copy(src.at[e], dst.at[slot], wsem.at[slot])
                for src, dst in ((wg_hbm, wg_f), (wu_hbm, wu_f), (wd_hbm, wd_f))]

    @pl.when(jnp.logical_and(live, k == 0))
    def _():
        for cp in weight_copies(be_ref[0], 0):
            cp.start()

    @pl.when(jnp.logical_and(live, new_expert))
    def _():
        slot = slot_ref[k]
        for cp in weight_copies(be_ref[k], slot):
            cp.wait()
        wg_b[...] = wg_f[slot].astype(BF16)
        wu_b[...] = wu_f[slot].astype(BF16)
        wd_b[...] = wd_f[slot].astype(BF16)

        @pl.when(nxt_ref[k] >= 0)
        def _():
            for cp in weight_copies(nxt_ref[k], 1 - slot):
                cp.start()

    def swiglu_rows(rows):
        lo, hi = _unpack_halves(jnp.concatenate(_load_row_tiles(x_ref, rows), axis=1))
        x = jnp.concatenate([lo.astype(BF16), hi.astype(BF16)], axis=1)
        g = jnp.dot(x, wg_b[...], preferred_element_type=F32)
        u = jnp.dot(x, wu_b[...], preferred_element_type=F32)
        hmid = (g * _sigmoid(g) * u).astype(BF16)
        y = jnp.dot(hmid, wd_b[...], preferred_element_type=F32)
        _store_row_tiles(o_ref, _pack_halves(y))

    @pl.when(jnp.logical_and(live, upper_half_used))
    def _():
        swiglu_rows(m)

    @pl.when(jnp.logical_and(live, jnp.logical_not(upper_half_used)))
    def _():
        swiglu_rows(half)
        o_ref[half * ROW_TILE:, :] = jnp.zeros((half * ROW_TILE, o_ref.shape[1]), o_ref.dtype)

    @pl.when(jnp.logical_not(live))
    def _():
        o_ref[...] = jnp.zeros_like(o_ref)


def _moe(xs, blk_e, n_valid, blk_rows, nxt_e, slot, w_gate, w_up, w_down):
    n_blocks = blk_e.shape[0]
    m = MOE_BLOCK

    def rows(k, be, nv, nr, nx, sl):
        return (jnp.minimum(k, nv[0] - 1), 0)

    hbm = pl.BlockSpec(memory_space=pl.ANY)
    grid_spec = pltpu.PrefetchScalarGridSpec(
        num_scalar_prefetch=5,
        grid=(n_blocks,),
        in_specs=[pl.BlockSpec((m * ROW_TILE, PACKED_LANES), rows), hbm, hbm, hbm],
        out_specs=pl.BlockSpec((m * ROW_TILE, PACKED_LANES), lambda k, be, nv, nr, nx, sl: (k, 0)),
        scratch_shapes=[pltpu.VMEM((2, D_MODEL, EXPERT_DIM), F32), pltpu.VMEM((2, D_MODEL, EXPERT_DIM), F32),
                        pltpu.VMEM((2, EXPERT_DIM, D_MODEL), F32),
                        pltpu.VMEM((D_MODEL, EXPERT_DIM), BF16), pltpu.VMEM((D_MODEL, EXPERT_DIM), BF16),
                        pltpu.VMEM((EXPERT_DIM, D_MODEL), BF16), pltpu.SemaphoreType.DMA((2,))],
    )
    return pl.pallas_call(
        _moe_kernel,
        grid_spec=grid_spec,
        out_shape=jax.ShapeDtypeStruct(xs.shape, jnp.uint32),
        compiler_params=_cparams(("arbitrary",)),
        name="moe_experts",
    )(blk_e, n_valid, blk_rows, nxt_e, slot, xs, w_gate, w_up, w_down)


def _final_kernel(dest_ref, dest_next_ref, x1_ref, w_ref, yb_hbm, sg_ref, su_ref, sd_ref, g_ref, b_ref, o_ref,
                  ybuf, sem):
    i = pl.program_id(0)
    tm = x1_ref.shape[0]

    def start(d_ref, slot, t):
        for k in range(TOP_K):
            _tile_copy(yb_hbm, d_ref[0, k, t], ybuf.at[slot, k], t, sem.at[slot]).start(priority=k % 2)

    def wait_all(slot):
        for k in range(TOP_K):
            pltpu.make_async_copy(yb_hbm.at[pl.ds(0, tm * ROW_TILE)], ybuf.at[slot, k], sem.at[slot]).wait()

    @pl.when(i == 0)
    def _():
        def issue(t, carry):
            start(dest_ref, 0, t)
            return carry

        lax.fori_loop(0, tm, issue, 0, unroll=4)

    cur = i % 2
    nxt = 1 - cur
    wait_all(cur)
    x1 = x1_ref[...]
    xb = x1.astype(BF16)
    g = jnp.dot(xb, sg_ref[...], preferred_element_type=F32)
    u = jnp.dot(xb, su_ref[...], preferred_element_type=F32)
    w = w_ref[...]
    per = tm // TOP_K
    r_lo = [None] * ROW_TILE
    r_hi = [None] * ROW_TILE
    for k in range(TOP_K):
        for t in range(k * per, (k + 1) * per):
            start(dest_next_ref, nxt, t)
        wk = jnp.broadcast_to(w[:, k:k + 1], (tm, PACKED_LANES))
        for s, piece in enumerate(_load_row_tiles(ybuf.at[cur, k], tm)):
            lo, hi = _unpack_halves(piece)
            r_lo[s] = wk * lo if r_lo[s] is None else r_lo[s] + wk * lo
            r_hi[s] = wk * hi if r_hi[s] is None else r_hi[s] + wk * hi
    y = jnp.dot((g * _sigmoid(g) * u).astype(BF16), sd_ref[...], preferred_element_type=F32)
    routed = jnp.concatenate(r_lo + r_hi, axis=1)
    o_ref[...] = _layer_norm(DN_ALPHA * x1 + (routed + y), g_ref[...], b_ref[...])

    @pl.when(i == pl.num_programs(0) - 1)
    def _():
        wait_all(nxt)


def _final(dest, x1, wts_t, yb, w_sg, w_su, w_sd, ln_g, ln_b, *, tm, name):
    n_tiles = x1.shape[0] // tm

    def tile(i):
        return (i, 0)

    def const(i):
        return (0, 0)

    smem = functools.partial(pl.BlockSpec, memory_space=pltpu.SMEM)
    return pl.pallas_call(
        _final_kernel,
        grid=(n_tiles,),
        in_specs=[
            smem((1, TOP_K, tm), lambda i: (i, 0, 0)),
            smem((1, TOP_K, tm), lambda i: (jnp.minimum(i + 1, n_tiles - 1), 0, 0)),
            pl.BlockSpec((tm, D_MODEL), tile),
            pl.BlockSpec((tm, TOP_K), tile),
            pl.BlockSpec(memory_space=pl.ANY),
            _resident((D_MODEL, EXPERT_DIM), const),
            _resident((D_MODEL, EXPERT_DIM), const),
            _resident((EXPERT_DIM, D_MODEL), const),
            pl.BlockSpec((1, D_MODEL), const),
            pl.BlockSpec((1, D_MODEL), const),
        ],
        out_specs=pl.BlockSpec((tm, D_MODEL), tile),
        out_shape=jax.ShapeDtypeStruct((n_tiles * tm, D_MODEL), F32),
        scratch_shapes=[pltpu.VMEM((2, TOP_K, tm * ROW_TILE, PACKED_LANES), jnp.uint32),
                        pltpu.SemaphoreType.DMA((2,))],
        compiler_params=_cparams(("arbitrary",)),
        name=name,
    )(dest, dest, x1, wts_t, yb, w_sg, w_su, w_sd, ln_g, ln_b)


def kernel(x_prompt, x_sample, cache_win_k, cache_win_v, state_hgrn, cache_mem_k, cache_mem_v, mem_prompt,
           w_in, w_mem_kv, a_sinks, b_lb_logits, b_norm_g, w_branch, w_out, ln1_g, ln1_b,
           w_router, router_bias, w_exp_gate, w_exp_up, w_exp_down, w_sh_gate, w_sh_up, w_sh_down, ln2_g, ln2_b):
    assert w_in.shape[0] == DEPTH == 1
    batch, seq, _ = x_prompt.shape
    dbatch, dseq, _ = x_sample.shape
    n_p = batch * seq
    n_s = dbatch * dseq
    n = n_p + n_s
    l = 0

    xp2 = x_prompt.reshape(n_p, D_MODEL)
    xs2 = x_sample.reshape(n_s, D_MODEL)
    win = w_in[l]
    o_k, o_v, o_b = A_WIDTH, A_WIDTH + A_KV_WIDTH, A_WIDTH + 2 * A_KV_WIDTH
    o_c = o_b + 4 * B_WIDTH
    o_g = o_c + C_WIDTH
    groups = [
        ("aq", 0, o_k, 1024, BF16, A_HEAD_DIM ** -0.5),
        ("akv", o_k, o_b, 2 * A_KV_WIDTH, F32, 1.0),
        ("hgrn", o_b, o_c, 1024, F32, 1.0),
        ("cq", o_c, o_g, 1024, BF16, C_HEAD_DIM ** -0.5),
        ("gate", o_g, win.shape[1], 1024, F32, 1.0),
    ]
    zp, zs = {}, {}
    for gname, c0, c1, tn, dt, scale in groups:
        wslice = win[:, c0:c1].astype(BF16)
        zp[gname] = _matmul(xp2, wslice, tm=1024, tn=tn, out_dtype=dt, scale=scale, name=f"proj_{gname}_p")
        zs[gname] = _matmul(xs2, wslice, tm=n_s, tn=tn, out_dtype=dt, scale=scale, name=f"proj_{gname}_s")

    a_p = _swa_prompt(zp["aq"], zp["akv"], a_sinks[l], batch=batch, seq=seq)
    lc = cache_win_k.shape[2]
    kc = cache_win_k[l].reshape(dbatch, lc, A_KV_WIDTH)
    vc = cache_win_v[l].reshape(dbatch, lc, A_KV_WIDTH)
    a_s = _swa_sample(zs["aq"], zs["akv"], kc, vc, a_sinks[l], row0=0, batch=dbatch, seq=dseq)

    lower = jnp.cumsum(jax.nn.softmax(b_lb_logits.astype(F32), axis=0), axis=0)[l]
    lbp = jnp.stack([jnp.log(lower), jnp.log1p(-lower), 1.0 - lower])
    ng = jnp.tile(b_norm_g[l].astype(F32), B_HEADS).reshape(1, B_WIDTH)
    s_zero = jnp.zeros((batch, B_HEADS, B_KEY_DIM, B_VAL_DIM), F32)
    b_p, hs_p = _hgrn(zp["hgrn"], lbp, ng, s_zero, batch=batch, seq=seq, blk=CHUNK, nbat=2 if batch % 2 == 0 else 1,
                      ntime=2 if seq % (2 * CHUNK) == 0 else 1)
    b_s, hs_s = _hgrn(zs["hgrn"], lbp, ng, state_hgrn[l].astype(F32), batch=dbatch, seq=dseq, blk=dseq,
                      nbat=2 if dbatch % 2 == 0 else 1, ntime=1)

    mem = mem_prompt.reshape(batch * MEM_LEN, D_MODEL)
    wmem = w_mem_kv[l]
    mk = _matmul(mem, wmem[:, :C_WIDTH].astype(BF16), tm=batch * MEM_LEN, tn=512, out_dtype=F32, name="proj_mem_k")
    mv = _matmul(mem, wmem[:, C_WIDTH:].astype(BF16), tm=batch * MEM_LEN, tn=512, out_dtype=F32, name="proj_mem_v")
    mk = mk.reshape(batch, MEM_LEN, C_WIDTH)
    mv = mv.reshape(batch, MEM_LEN, C_WIDTH)
    c_p = _cattn(zp["cq"], mk, mv, row0=0, batch=batch, seq=seq, tq=512)
    c_s = _cattn(zs["cq"], cache_mem_k[l].reshape(dbatch, MEM_LEN, C_WIDTH),
                 cache_mem_v[l].reshape(dbatch, MEM_LEN, C_WIDTH), row0=0, batch=dbatch, seq=dseq, tq=dseq)

    w_br = w_branch[l].astype(BF16)
    route_w = (w_out[l].astype(BF16), ln1_g[l].reshape(1, D_MODEL), ln1_b[l].reshape(1, D_MODEL),
               w_router[l].T.astype(BF16), router_bias[l].reshape(N_EXPERTS, 1).astype(F32))
    segs = []
    cnt = jnp.zeros((N_EXPERTS, 128), jnp.int32)
    for tag, x2, abc, gate, tm_r in (("p", xp2, (a_p, b_p, c_p), zp["gate"], 512), ("s", xs2, (a_s, b_s, c_s), zs["gate"], n_s)):
        h = _merge(*abc, gate, w_br, tm=tm_r, name=f"merge_{tag}")
        x1, x1p, idx, wts, pos, cnt = _out_ln_route(x2, h, *route_w, cnt, tm=tm_r, name=f"out_ln_route_{tag}")
        segs.append(dict(tag=tag, x1=x1, x1p=x1p, idx=idx, wts=wts, pos=pos, tm=tm_r, rows=x2.shape[0]))

    m = MOE_BLOCK
    n_pairs = n * TOP_K
    n_blocks = (n_pairs + m - 1) // m + N_EXPERTS
    counts = cnt[:, 0]
    padded = (counts + m - 1) // m * m
    pad_end = jnp.cumsum(padded)
    pad_start = (pad_end - padded).astype(jnp.int32)
    n_valid = pad_end[-1:] // m
    e_ids = jnp.arange(N_EXPERTS, dtype=jnp.int32)
    blk_first = jnp.arange(n_blocks, dtype=jnp.int32) * m
    blk_e = jnp.sum((blk_first[:, None] >= pad_end[None, :]).astype(jnp.int32), axis=1)
    blk_e = jnp.minimum(blk_e, N_EXPERTS - 1)
    row_end = jnp.sum(jnp.where(blk_e[:, None] == e_ids, (pad_start + counts.astype(jnp.int32))[None, :], 0), axis=1)
    blk_rows = jnp.clip(row_end - blk_first, 0, m).astype(jnp.int32)
    owns = counts > 0
    later = (e_ids[None, :] > e_ids[:, None]) & owns[None, :]
    next_owner = jnp.min(jnp.where(later, e_ids[None, :], N_EXPERTS), axis=1)
    next_owner = jnp.where(next_owner < N_EXPERTS, next_owner, -1)
    owner_rank = jnp.cumsum(owns.astype(jnp.int32)) - 1
    nxt_e = jnp.sum(jnp.where(blk_e[:, None] == e_ids, next_owner[None, :], 0), axis=1).astype(jnp.int32)
    slot = (jnp.sum(jnp.where(blk_e[:, None] == e_ids, owner_rank[None, :], 0), axis=1) % 2).astype(jnp.int32)
    z_pad = jnp.where(padded > counts, pad_end - m, -1)
    z_tail = jnp.where(blk_first >= pad_end[-1], blk_first, -1)
    zrows = jnp.concatenate([z_pad, z_tail]).astype(jnp.int32)

    def retile(dest, rows, tm_from, tm_to):
        d = dest.reshape(rows // tm_from, TOP_K, tm_from // tm_to, tm_to)
        return jnp.transpose(d, (0, 2, 1, 3)).reshape(rows // tm_to, TOP_K, tm_to)

    tm_d = 256
    for seg in segs:
        first_row = jnp.sum(jnp.where(seg["idx"][..., None] == e_ids, pad_start, 0), axis=-1)
        seg["dest"] = first_row + seg["pos"]
    dest_d = jnp.concatenate([retile(seg["dest"], seg["rows"], seg["tm"], tm_d) for seg in segs], axis=0)
    xs = _dispatch(zrows, dest_d, segs[0]["x1p"], segs[1]["x1p"], n_rows=n_blocks * m, tm=tm_d, block_rows=m)
    yb = _moe(xs, blk_e, n_valid.astype(jnp.int32), blk_rows, nxt_e, slot, w_exp_gate[l].astype(F32), w_exp_up[l].astype(F32),
              w_exp_down[l].astype(F32))
    tm_f = 256
    shared_w = (w_sh_gate[l].astype(BF16), w_sh_up[l].astype(BF16), w_sh_down[l].astype(BF16),
                ln2_g[l].reshape(1, D_MODEL), ln2_b[l].reshape(1, D_MODEL))
    ys = []
    for seg in segs:
        rows = seg["rows"]
        wts_t = jnp.transpose(seg["wts"], (0, 2, 1)).reshape(rows, TOP_K)
        ys.append(_final(retile(seg["dest"], rows, seg["tm"], tm_f), seg["x1"], wts_t, yb, *shared_w, tm=tm_f,
                         name=f"combine_shared_ln2_{seg['tag']}"))
    y_p = ys[0].reshape(batch, seq, D_MODEL)
    y_s = ys[1].reshape(dbatch, dseq, D_MODEL)

    kv_p = zp["akv"].reshape(batch, seq, 2, A_KV_HEADS, A_HEAD_DIM)[:, -lc:]
    k_p, v_p = kv_p[:, :, 0], kv_p[:, :, 1]
    kv_s = zs["akv"].reshape(dbatch, dseq, 2, A_KV_HEADS, A_HEAD_DIM)
    k_s, v_s = kv_s[:, :, 0], kv_s[:, :, 1]
    wk_s = jnp.concatenate([cache_win_k[l].astype(F32), k_s], axis=1)[:, -lc:]
    wv_s = jnp.concatenate([cache_win_v[l].astype(F32), v_s], axis=1)[:, -lc:]
    mk_o = mk.reshape(batch, MEM_LEN, C_HEADS, C_HEAD_DIM)
    mv_o = mv.reshape(batch, MEM_LEN, C_HEADS, C_HEAD_DIM)
    return (y_p, y_s, k_p[None], v_p[None], hs_p[None], mk_o[None], mv_o[None], wk_s[None], wv_s[None], hs_s[None])
```

```python
import functools

import jax
import jax.numpy as jnp
import numpy as np
from jax import lax
from jax.experimental import pallas as pl
from jax.experimental.pallas import tpu as pltpu

F32 = jnp.float32
BF16 = jnp.bfloat16

D_MODEL = 2048
DEPTH = 1
PAST_LEN = 2048
CHUNK = 64
A_HEADS = 16
A_KV_HEADS = 4
A_GROUP = A_HEADS // A_KV_HEADS
A_HEAD_DIM = 64
A_WIDTH = A_HEADS * A_HEAD_DIM
A_KV_WIDTH = A_KV_HEADS * A_HEAD_DIM
WINDOW = 128
WIN_CHUNKS = WINDOW // CHUNK
B_HEADS = 8
B_KEY_DIM = 128
B_VAL_DIM = 128
B_WIDTH = B_HEADS * B_VAL_DIM
SUB = 16
MEM_LEN = 256
C_HEADS = 4
C_HEAD_DIM = 256
C_WIDTH = C_HEADS * C_HEAD_DIM
N_EXPERTS = 64
N_GROUPS = 8
GROUP_SIZE = N_EXPERTS // N_GROUPS
TOPK_GROUPS = 4
TOP_K = 8
EXPERT_DIM = 512
ROUTED_SCALE = 2.5
DN_ALPHA = (2 * DEPTH) ** 0.25
LN_EPS = 1e-5
RMS_EPS = 1e-6

MOE_BLOCK = 512
VMEM_LIMIT = 56 * 1024 * 1024


def _cparams(sem):
    return pltpu.CompilerParams(dimension_semantics=sem, vmem_limit_bytes=VMEM_LIMIT)


def _resident(shape, index_map):
    return pl.BlockSpec(shape, index_map, pipeline_mode=pl.Buffered(1))


def _sigmoid(x):
    return 1.0 / (1.0 + jnp.exp(-x))


def _layer_norm(x, g, b):
    mu = jnp.mean(x, axis=-1, keepdims=True)
    xc = x - mu
    var = jnp.mean(xc * xc, axis=-1, keepdims=True)
    return xc * lax.rsqrt(var + LN_EPS) * g + b


def _mm_kernel(x_ref, w_ref, o_ref, *, scale):
    acc = jnp.dot(x_ref[...].astype(BF16), w_ref[...], preferred_element_type=F32)
    if scale != 1.0:
        acc = acc * scale
    o_ref[...] = acc.astype(o_ref.dtype)


def _matmul(x, w, *, tm, tn, out_dtype, scale=1.0, name):
    m, k = x.shape
    n = w.shape[1]
    return pl.pallas_call(
        functools.partial(_mm_kernel, scale=scale),
        grid=(m // tm, n // tn),
        in_specs=[pl.BlockSpec((tm, k), lambda i, j: (i, 0)), pl.BlockSpec((k, tn), lambda i, j: (0, j))],
        out_specs=pl.BlockSpec((tm, tn), lambda i, j: (i, j)),
        out_shape=jax.ShapeDtypeStruct((m, n), out_dtype),
        compiler_params=_cparams(("parallel", "parallel")),
        name=name,
    )(x, w)


SWA_KEYS = 256
HEAD_PAIR = 2 * A_HEAD_DIM


def _kv_head_planes(x):
    lane = lax.broadcasted_iota(jnp.int32, (1, HEAD_PAIR), 1)
    low = lane < A_HEAD_DIM
    planes = []
    for pair in range(A_KV_HEADS // 2):
        own = x[:, pair * HEAD_PAIR:(pair + 1) * HEAD_PAIR]
        swapped = pltpu.roll(own, A_HEAD_DIM, 1)
        planes.append((jnp.where(low, own, 0.0).astype(BF16), jnp.where(low, 0.0, swapped).astype(BF16)))
        planes.append((jnp.where(low, swapped, 0.0).astype(BF16), jnp.where(low, 0.0, own).astype(BF16)))
    return planes


SWA_LOOKAHEAD = 8


def _swa_chunks(q_ref, rows, chunk_rows, k_planes, v_planes, n_band, bias_ref, valid_of, o_ref):
    zeros_k = jnp.zeros((SWA_KEYS - n_band, HEAD_PAIR), BF16)
    ones_v = jnp.ones((SWA_KEYS, HEAD_PAIR), BF16)
    tiles = [(c, h, x) for c in range(len(chunk_rows)) for h in range(A_KV_HEADS) for x in range(2)]

    def band(plane, c):
        return jnp.concatenate([plane[chunk_rows[c]:chunk_rows[c] + n_band], zeros_k], axis=0)

    def scores(t):
        c, h, x = tiles[t]
        q2 = jnp.concatenate([q_ref[c * rows:(c + 1) * rows, (2 * h + j) * HEAD_PAIR:(2 * h + j + 1) * HEAD_PAIR]
                              for j in range(2)], axis=0).astype(BF16)
        s = lax.dot_general(q2, band(k_planes[h][x], c), (((1,), (1,)), ((), ())), preferred_element_type=F32)
        s = s - bias_ref[h, x]
        valid = valid_of(c)
        return s if valid is None else jnp.where(valid, s, -jnp.inf)

    pending = {t: scores(t) for t in range(min(SWA_LOOKAHEAD, len(tiles)))}
    even = None
    for t, (c, h, x) in enumerate(tiles):
        s = pending.pop(t)
        p = jnp.exp(s - jnp.max(s, axis=-1, keepdims=True)).astype(BF16)
        if t + SWA_LOOKAHEAD < len(tiles):
            pending[t + SWA_LOOKAHEAD] = scores(t + SWA_LOOKAHEAD)
        vw = jnp.concatenate([band(v_planes[h][x], c), ones_v], axis=1)
        o = jnp.dot(p, vw, preferred_element_type=F32)
        o = o[:, :HEAD_PAIR] / o[:, HEAD_PAIR:]
        if x == 0:
            even = o
        else:
            out = even + o
            for j in range(2):
                o_ref[c * rows:(c + 1) * rows, (2 * h + j) * HEAD_PAIR:(2 * h + j + 1) * HEAD_PAIR] = (
                    out[j * rows:(j + 1) * rows].astype(o_ref.dtype))


def _swa_prompt_kernel(q_ref, kc_ref, kp_ref, vc_ref, vp_ref, bias_ref, o_ref, *, n_chunks):
    i = pl.program_id(1)
    pad = WIN_CHUNKS * CHUNK
    n_band = pad + CHUNK
    k_planes = _kv_head_planes(jnp.concatenate([kp_ref[...], kc_ref[...]], axis=0))
    v_planes = _kv_head_planes(jnp.concatenate([vp_ref[...], vc_ref[...]], axis=0))
    s_idx = lax.broadcasted_iota(jnp.int32, (1, SWA_KEYS), 1)

    def valid_of(c):
        return jnp.logical_or(s_idx + (i * (n_chunks * CHUNK) + c * CHUNK - pad) >= 0, s_idx >= n_band)

    _swa_chunks(q_ref, CHUNK, [c * CHUNK for c in range(n_chunks)], k_planes, v_planes, n_band, bias_ref, valid_of,
                o_ref)


def _swa_sample_kernel(q_ref, kn_ref, kc_ref, vn_ref, vc_ref, bias_ref, o_ref):
    k_planes = _kv_head_planes(jnp.concatenate([kc_ref[0], kn_ref[...]], axis=0))
    v_planes = _kv_head_planes(jnp.concatenate([vc_ref[0], vn_ref[...]], axis=0))
    n_band = kc_ref.shape[1] + kn_ref.shape[0]
    _swa_chunks(q_ref, q_ref.shape[0], [0], k_planes, v_planes, n_band, bias_ref, lambda c: None, o_ref)


def _alibi_slopes():
    return (2.0 ** (-8.0 * np.arange(1, A_HEADS + 1) / A_HEADS)).astype(np.float32)


def _swa_bias(q_pos, k_pos, valid, sinks):
    n_q, n_k = len(q_pos), len(k_pos)
    assert n_k < SWA_KEYS
    dist = np.abs(q_pos[:, None] - k_pos[None, :]).astype(np.float32)
    band = _alibi_slopes()[:, None, None] * dist[None]
    if valid is not None:
        band = np.where(valid[None], band, np.inf)
    pad = np.full((A_HEADS, n_q, SWA_KEYS - n_k - 1), np.inf, np.float32)
    sink = jnp.broadcast_to(-sinks.astype(F32)[:, None, None], (A_HEADS, n_q, 1))
    bias = jnp.concatenate([jnp.asarray(band.astype(np.float32)), sink, jnp.asarray(pad)], axis=2)
    bias = bias.reshape(A_KV_HEADS, 2, 2, n_q, SWA_KEYS).transpose(0, 2, 1, 3, 4)
    return bias.reshape(A_KV_HEADS, 2, 2 * n_q, SWA_KEYS)


def _swa_prompt(q, kv, sinks, *, batch, seq, n_chunks=4):
    tq = n_chunks * CHUNK
    pad = WIN_CHUNKS * CHUNK
    nb = seq // tq
    bias = _swa_bias(pad + np.arange(CHUNK), np.arange(pad + CHUNK), None, sinks)
    prev_per_blk = tq // pad

    def cur(b, i, col=0):
        return (b * nb + i, col)

    def prev(b, i, col=0):
        return (jnp.maximum((b * nb + i) * prev_per_blk - 1, b * nb * prev_per_blk), col)

    return pl.pallas_call(
        functools.partial(_swa_prompt_kernel, n_chunks=n_chunks),
        grid=(batch, nb),
        in_specs=[
            pl.BlockSpec((tq, A_WIDTH), cur),
            pl.BlockSpec((tq, A_KV_WIDTH), cur),
            pl.BlockSpec((pad, A_KV_WIDTH), prev),
            pl.BlockSpec((tq, A_KV_WIDTH), functools.partial(cur, col=1)),
            pl.BlockSpec((pad, A_KV_WIDTH), functools.partial(prev, col=1)),
            pl.BlockSpec(bias.shape, lambda b, i: (0, 0, 0, 0)),
        ],
        out_specs=pl.BlockSpec((tq, A_WIDTH), cur),
        out_shape=jax.ShapeDtypeStruct((batch * seq, A_WIDTH), BF16),
        compiler_params=_cparams(("parallel", "arbitrary")),
        name="swa_prompt",
    )(q, kv, kv, kv, kv, bias)


def _swa_sample(q, kv, k_cache, v_cache, sinks, *, row0, batch, seq):
    lc = k_cache.shape[1]
    q_pos = PAST_LEN + np.arange(seq)
    k_pos = PAST_LEN - lc + np.arange(lc + seq)
    cdiff = q_pos[:, None] // CHUNK - k_pos[None, :] // CHUNK
    valid = (cdiff >= 0) & (cdiff <= WIN_CHUNKS)
    bias = _swa_bias(q_pos, k_pos, valid, sinks)
    blk0 = row0 // seq

    def rows(b):
        return (blk0 + b, 0)

    return pl.pallas_call(
        _swa_sample_kernel,
        grid=(batch,),
        in_specs=[
            pl.BlockSpec((seq, A_WIDTH), rows),
            pl.BlockSpec((seq, A_KV_WIDTH), rows),
            pl.BlockSpec((1, lc, A_KV_WIDTH), lambda b: (b, 0, 0)),
            pl.BlockSpec((seq, A_KV_WIDTH), lambda b: (blk0 + b, 1)),
            pl.BlockSpec((1, lc, A_KV_WIDTH), lambda b: (b, 0, 0)),
            pl.BlockSpec(bias.shape, lambda b: (0, 0, 0, 0)),
        ],
        out_specs=pl.BlockSpec((seq, A_WIDTH), lambda b: (b, 0)),
        out_shape=jax.ShapeDtypeStruct((batch * seq, A_WIDTH), BF16),
        compiler_params=_cparams(("parallel",)),
        name="swa_sample",
    )(q, kv, k_cache, kv, v_cache, bias)


def _split3(x):
    hi = x.astype(BF16)
    r1 = x - hi.astype(F32)
    mid = r1.astype(BF16)
    lo = (r1 - mid.astype(F32)).astype(BF16)
    return hi, mid, lo


def _dot3(mat, parts):
    acc = jnp.dot(mat, parts[0], preferred_element_type=F32)
    acc = acc + jnp.dot(mat, parts[1], preferred_element_type=F32)
    return acc + jnp.dot(mat, parts[2], preferred_element_type=F32)


def _hgrn_kernel(zq_ref, zf_ref, zi_ref, zg_ref, lb_ref, ng_ref, tri_ref, tsel_ref, s0_ref, o_ref, sfin_ref, s_scr,
                 *, blk, nbat, ntime):
    j = pl.program_id(1)

    @pl.when(j == 0)
    def _():
        s_scr[...] = s0_ref[...]

    log_lb = lb_ref[0:1, :]
    log1m_lb = lb_ref[1:2, :]
    one_m_lb = lb_ref[2:3, :]
    row = lax.broadcasted_iota(jnp.int32, (blk, 1), 0)
    n_sub = blk // SUB
    tril = lax.broadcasted_iota(jnp.int32, (blk, blk), 0) >= lax.broadcasted_iota(jnp.int32, (blk, blk), 1)
    heads = [slice(h * B_KEY_DIM, (h + 1) * B_KEY_DIM) for h in range(B_HEADS)]

    def prepare(bi, ti):
        rows = slice(ti * blk, (ti + 1) * blk)
        bq = zq_ref[bi, rows, :]
        fl = zf_ref[bi, rows, :]
        q = bq * _sigmoid(bq) * (B_KEY_DIM ** -0.5)
        log_sig = jnp.minimum(fl, 0.0) - jnp.log1p(jnp.exp(-jnp.abs(fl)))
        c = log1m_lb + log_sig
        logf = jnp.maximum(log_lb, c) + jnp.log1p(jnp.exp(-jnp.abs(log_lb - c)))
        k = one_m_lb * _sigmoid(-fl)
        parts = _split3(logf)
        b = _dot3(tri_ref[...], parts)
        rq = _dot3(tsel_ref[...], parts)
        qt = (q * jnp.exp(b - rq)).astype(BF16)
        qb = (q * jnp.exp(b)).astype(BF16)
        b_last = b[blk - 1:blk, :]
        khat = (k * jnp.exp(b_last - b)).astype(BF16)
        vb = zi_ref[bi, rows, :].astype(BF16)
        kts = []
        for i in range(n_sub):
            r_i = rq[i * SUB:i * SUB + 1, :]
            kts.append(jnp.where(row < (i + 1) * SUB, k * jnp.exp(r_i - b), 0.0).astype(BF16))
        a_raw, ds = [], []
        for h, hs in enumerate(heads):
            a_raw.append(jnp.concatenate(
                [lax.dot_general(qt[i * SUB:(i + 1) * SUB, hs], kts[i][:, hs], (((1,), (1,)), ((), ())),
                                 preferred_element_type=F32) for i in range(n_sub)], axis=0))
            ds.append(lax.dot_general(khat[:, hs], vb[:, hs], (((0,), (0,)), ((), ())), preferred_element_type=F32))
        return dict(a_raw=a_raw, ds=ds, qb=qb, vb=vb, e_last=jnp.exp(b_last), rows=rows)

    def finish(bi, p):
        s_old = [s_scr[bi, h] for h in range(B_HEADS)]
        o_state = [jnp.dot(p["qb"][:, hs], s_old[h].astype(BF16), preferred_element_type=F32)
                   for h, hs in enumerate(heads)]
        outs = []
        for h, hs in enumerate(heads):
            a = jnp.where(tril, p["a_raw"][h], 0.0).astype(BF16)
            o = jnp.dot(a, p["vb"][:, hs], preferred_element_type=F32) + o_state[h]
            decay = jnp.transpose(jnp.broadcast_to(p["e_last"][:, hs], (B_KEY_DIM, B_KEY_DIM)))
            s_scr[bi, h] = decay * s_old[h] + p["ds"][h]
            outs.append(o * lax.rsqrt(jnp.mean(o * o, axis=-1, keepdims=True) + RMS_EPS))
        bg = zg_ref[bi, p["rows"], :]
        o_all = jnp.concatenate(outs, axis=1) * ng_ref[...] * (bg * _sigmoid(bg))
        o_ref[bi, p["rows"], :] = o_all.astype(o_ref.dtype)

    prepared = {(bi, ti): prepare(bi, ti) for ti in range(ntime) for bi in range(nbat)}
    for ti in range(ntime):
        for bi in range(nbat):
            finish(bi, prepared[bi, ti])

    @pl.when(j == pl.num_programs(1) - 1)
    def _():
        sfin_ref[...] = s_scr[...]


def _hgrn(zb, lbp, ng, s0, *, batch, seq, blk, nbat, ntime):
    nb = seq // (blk * ntime)
    t = np.arange(blk)
    tri = jnp.asarray((t[:, None] >= t[None, :]).astype(np.float32), dtype=BF16)
    tsel = jnp.asarray((t[None, :] < (t[:, None] // SUB) * SUB).astype(np.float32), dtype=BF16)
    z3 = zb.reshape(batch, seq, 4 * B_WIDTH)

    def zspec(col):
        return pl.BlockSpec((nbat, blk * ntime, B_WIDTH), lambda b, j: (b, j, col))

    state_spec = pl.BlockSpec((nbat, B_HEADS, B_KEY_DIM, B_VAL_DIM), lambda b, j: (b, 0, 0, 0))
    out, s_fin = pl.pallas_call(
        functools.partial(_hgrn_kernel, blk=blk, nbat=nbat, ntime=ntime),
        grid=(batch // nbat, nb),
        in_specs=[
            zspec(0), zspec(1), zspec(2), zspec(3),
            pl.BlockSpec((3, B_WIDTH), lambda b, j: (0, 0)),
            pl.BlockSpec((1, B_WIDTH), lambda b, j: (0, 0)),
            pl.BlockSpec((blk, blk), lambda b, j: (0, 0)),
            pl.BlockSpec((blk, blk), lambda b, j: (0, 0)),
            state_spec,
        ],
        out_specs=[pl.BlockSpec((nbat, blk * ntime, B_WIDTH), lambda b, j: (b, j, 0)), state_spec],
        out_shape=[jax.ShapeDtypeStruct((batch, seq, B_WIDTH), BF16),
                   jax.ShapeDtypeStruct((batch, B_HEADS, B_KEY_DIM, B_VAL_DIM), F32)],
        scratch_shapes=[pltpu.VMEM((nbat, B_HEADS, B_KEY_DIM, B_VAL_DIM), F32)],
        compiler_params=_cparams(("parallel", "arbitrary")),
        name=f"hgrn_blk{blk}",
    )(z3, z3, z3, z3, lbp, ng, tri, tsel, s0)
    return out.reshape(batch * seq, B_WIDTH), s_fin


def _cattn_kernel(q_ref, mk_ref, mv_ref, o_ref):
    mk = mk_ref[0].astype(BF16)
    mv = mv_ref[0].astype(BF16)
    q = q_ref[...]
    heads = [slice(h * C_HEAD_DIM, (h + 1) * C_HEAD_DIM) for h in range(C_HEADS)]
    scores = [lax.dot_general(q[:, hs], mk[:, hs], (((1,), (1,)), ((), ())), preferred_element_type=F32)
              for hs in heads]
    for hs, s in zip(heads, scores):
        p = jnp.exp(s - jnp.max(s, axis=-1, keepdims=True))
        p = p / jnp.sum(p, axis=-1, keepdims=True)
        o_ref[:, hs] = jnp.dot(p.astype(BF16), mv[:, hs], preferred_element_type=F32).astype(o_ref.dtype)


def _cattn(q, mk, mv, *, row0, batch, seq, tq):
    nb = seq // tq
    blk0 = row0 // tq
    mem_spec = pl.BlockSpec((1, MEM_LEN, C_WIDTH), lambda b, i: (b, 0, 0))
    return pl.pallas_call(
        _cattn_kernel,
        grid=(batch, nb),
        in_specs=[pl.BlockSpec((tq, C_WIDTH), lambda b, i: (blk0 + b * nb + i, 0)), mem_spec, mem_spec],
        out_specs=pl.BlockSpec((tq, C_WIDTH), lambda b, i: (b * nb + i, 0)),
        out_shape=jax.ShapeDtypeStruct((batch * seq, C_WIDTH), BF16),
        compiler_params=_cparams(("parallel", "parallel")),
        name=f"cattn_tq{tq}",
    )(q, mk, mv)


def _merge_kernel(a_ref, b_ref, c_ref, g0_ref, g1_ref, g2_ref, pa_ref, pb_ref, pc_ref, o_ref):
    da = jnp.dot(a_ref[...], pa_ref[...], preferred_element_type=F32)
    db = jnp.dot(b_ref[...], pb_ref[...], preferred_element_type=F32)
    dc = jnp.dot(c_ref[...], pc_ref[...], preferred_element_type=F32)
    h = _sigmoid(g0_ref[...]) * da + _sigmoid(g1_ref[...]) * db + _sigmoid(g2_ref[...]) * dc
    o_ref[...] = h.astype(o_ref.dtype)


def _merge(a, b, c, gl, p, *, tm, name):
    n, width = a.shape

    def rows(i):
        return (i, 0)

    return pl.pallas_call(
        _merge_kernel,
        grid=(n // tm,),
        in_specs=[
            pl.BlockSpec((tm, width), rows), pl.BlockSpec((tm, width), rows), pl.BlockSpec((tm, width), rows),
            pl.BlockSpec((tm, D_MODEL), lambda i: (i, 0)),
            pl.BlockSpec((tm, D_MODEL), lambda i: (i, 1)),
            pl.BlockSpec((tm, D_MODEL), lambda i: (i, 2)),
            _resident((width, D_MODEL), lambda i: (0, 0)),
            _resident((width, D_MODEL), lambda i: (1, 0)),
            _resident((width, D_MODEL), lambda i: (2, 0)),
        ],
        out_specs=pl.BlockSpec((tm, D_MODEL), rows),
        out_shape=jax.ShapeDtypeStruct((n, D_MODEL), BF16),
        compiler_params=_cparams(("parallel",)),
        name=name,
    )(a, b, c, gl, gl, gl, p, p, p)


def _first_index(hit_src, m, iota, size, axis):
    return jnp.min(jnp.where(hit_src == m, iota, size), axis=axis, keepdims=True)


def _route(x1, wr_ref, rb_ref, upper_ref, run_ref, idx_ref, wts_ref, pos_ref):
    tm = x1.shape[0]
    logits = lax.dot_general(wr_ref[...], x1.astype(BF16), (((1,), (1,)), ((), ())),
                             preferred_element_type=F32)
    scores = _sigmoid(logits)
    choice = scores + rb_ref[...]
    g3 = choice.reshape(N_GROUPS, GROUP_SIZE, tm)
    mem_iota = lax.broadcasted_iota(jnp.int32, g3.shape, 1)
    m1 = jnp.max(g3, axis=1, keepdims=True)
    first = _first_index(g3, m1, mem_iota, GROUP_SIZE, 1)
    m2 = jnp.max(jnp.where(mem_iota == first, -jnp.inf, g3), axis=1, keepdims=True)
    gscore = (m1 + m2).reshape(N_GROUPS, tm)
    g_iota = lax.broadcasted_iota(jnp.int32, gscore.shape, 0)
    gsel = jnp.zeros(gscore.shape, F32)
    cur = gscore
    for _ in range(TOPK_GROUPS):
        m = jnp.max(cur, axis=0, keepdims=True)
        hit = g_iota == _first_index(cur, m, g_iota, N_GROUPS, 0)
        gsel = jnp.where(hit, 1.0, gsel)
        cur = jnp.where(hit, -jnp.inf, cur)
    masked = jnp.where(gsel.reshape(N_GROUPS, 1, tm) > 0.5, g3, -jnp.inf).reshape(N_EXPERTS, tm)
    e_iota = lax.broadcasted_iota(jnp.int32, masked.shape, 0)
    sel = jnp.zeros(masked.shape, F32)
    cur = masked
    idxs, ws = [], []
    for _ in range(TOP_K):
        m = jnp.max(cur, axis=0, keepdims=True)
        ei = _first_index(cur, m, e_iota, N_EXPERTS, 0)
        hit = e_iota == ei
        idxs.append(ei)
        ws.append(jnp.sum(jnp.where(hit, scores, 0.0), axis=0, keepdims=True))
        sel = jnp.where(hit, 1.0, sel)
        cur = jnp.where(hit, -jnp.inf, cur)
    w = jnp.concatenate(ws, axis=0)
    w = w / jnp.sum(w, axis=0, keepdims=True) * ROUTED_SCALE
    excl = jnp.dot(sel.astype(BF16), upper_ref[...], preferred_element_type=F32)
    posfull = excl + run_ref[:, 0:1]
    run_ref[...] = run_ref[...] + jnp.sum(sel, axis=1, keepdims=True)
    pos = [jnp.sum(jnp.where(e_iota == ei, posfull, 0.0), axis=0, keepdims=True) for ei in idxs]
    idx_ref[0] = jnp.concatenate(idxs, axis=0)
    wts_ref[0] = w
    pos_ref[0] = jnp.concatenate(pos, axis=0).astype(jnp.int32)


def _pack_halves(x):
    c = x.shape[1] // 2
    lo = lax.bitcast_convert_type(x[:, :c].astype(BF16).astype(F32), jnp.uint32)
    hi = lax.bitcast_convert_type(x[:, c:].astype(BF16).astype(F32), jnp.uint32)
    return (hi & jnp.uint32(0xFFFF0000)) | (lo >> 16)


def _unpack_halves(w):
    lo = lax.bitcast_convert_type(w << 16, F32)
    hi = lax.bitcast_convert_type(w & jnp.uint32(0xFFFF0000), F32)
    return lo, hi


ROW_TILE = 8
PACKED_LANES = D_MODEL // 2 // ROW_TILE


def _store_row_tiles(ref, val):
    rows = val.shape[0]
    for s in range(ROW_TILE):
        ref[pl.ds(s, rows, stride=ROW_TILE), :] = val[:, s * PACKED_LANES:(s + 1) * PACKED_LANES]


def _load_row_tiles(ref, rows):
    return [ref[pl.ds(s, rows, stride=ROW_TILE), :] for s in range(ROW_TILE)]


def _out_ln_route_kernel(x_ref, h_ref, wo_ref, g_ref, b_ref, wr_ref, rb_ref, upper_ref, run0_ref,
                         x1_ref, x1p_ref, idx_ref, wts_ref, pos_ref, cnt_ref, run_ref):
    @pl.when(pl.program_id(0) == 0)
    def _():
        run_ref[...] = run0_ref[...].astype(F32)

    y = DN_ALPHA * x_ref[...] + jnp.dot(h_ref[...], wo_ref[...], preferred_element_type=F32)
    x1 = _layer_norm(y, g_ref[...], b_ref[...])
    x1_ref[...] = x1
    _store_row_tiles(x1p_ref, _pack_halves(x1))
    _route(x1, wr_ref, rb_ref, upper_ref, run_ref, idx_ref, wts_ref, pos_ref)
    cnt_ref[...] = run_ref[...].astype(jnp.int32)


def _out_ln_route(x, h, w_out, ln_g, ln_b, w_router_t, router_bias, run0, *, tm, name):
    n = h.shape[0]
    nt = n // tm
    t = np.arange(tm)
    upper = jnp.asarray((t[:, None] < t[None, :]).astype(np.float32), dtype=BF16)

    def rows(i):
        return (i, 0)

    def const(i):
        return (0, 0)

    small = pl.BlockSpec((1, TOP_K, tm), lambda i: (i, 0, 0))
    return pl.pallas_call(
        _out_ln_route_kernel,
        grid=(nt,),
        in_specs=[
            pl.BlockSpec((tm, D_MODEL), rows),
            pl.BlockSpec((tm, D_MODEL), rows),
            _resident((D_MODEL, D_MODEL), const),
            pl.BlockSpec((1, D_MODEL), const),
            pl.BlockSpec((1, D_MODEL), const),
            pl.BlockSpec((N_EXPERTS, D_MODEL), const),
            pl.BlockSpec((N_EXPERTS, 1), const),
            pl.BlockSpec((tm, tm), const),
            pl.BlockSpec((N_EXPERTS, 128), const),
        ],
        out_specs=[pl.BlockSpec((tm, D_MODEL), rows), pl.BlockSpec((tm * ROW_TILE, PACKED_LANES), rows), small, small,
                   small, pl.BlockSpec((N_EXPERTS, 128), const)],
        out_shape=[
            jax.ShapeDtypeStruct((n, D_MODEL), F32),
            jax.ShapeDtypeStruct((n * ROW_TILE, PACKED_LANES), jnp.uint32),
            jax.ShapeDtypeStruct((nt, TOP_K, tm), jnp.int32),
            jax.ShapeDtypeStruct((nt, TOP_K, tm), F32),
            jax.ShapeDtypeStruct((nt, TOP_K, tm), jnp.int32),
            jax.ShapeDtypeStruct((N_EXPERTS, 128), jnp.int32),
        ],
        scratch_shapes=[pltpu.VMEM((N_EXPERTS, 128), F32)],
        compiler_params=_cparams(("arbitrary",)),
        name=name,
    )(x, h, w_out, ln_g, ln_b, w_router_t, router_bias, upper, run0)


def _row_copy(src_hbm, row, dst, dst_row, sem):
    return pltpu.make_async_copy(src_hbm.at[pl.ds(row, 1)], dst.at[pl.ds(dst_row, 1)], sem)


def _tile_copy(src, src_row, dst, dst_row, sem):
    def tile(row):
        first = row * ROW_TILE
        return pl.ds(first if isinstance(row, int) else pl.multiple_of(first, ROW_TILE), ROW_TILE)

    return pltpu.make_async_copy(src.at[tile(src_row)], dst.at[tile(dst_row)], sem)


def _dispatch_kernel(zrow_ref, dest_ref, xa_ref, xb_ref, o_hbm, zbuf, sem, zsem, *, block_rows, n_first):
    tm = xa_ref.shape[0] // ROW_TILE
    n_zero = zrow_ref.shape[0]

    def zero_copy(b):
        start = pl.multiple_of(jnp.maximum(zrow_ref[b], 0) * ROW_TILE, ROW_TILE)
        return pltpu.make_async_copy(zbuf, o_hbm.at[pl.ds(start, block_rows * ROW_TILE)], zsem)

    def zero_unowned_rows():
        zbuf[...] = jnp.zeros_like(zbuf)

        def start(b, carry):
            @pl.when(zrow_ref[b] >= 0)
            def _():
                zero_copy(b).start()
            return carry

        def wait(b, carry):
            @pl.when(zrow_ref[b] >= 0)
            def _():
                zero_copy(b).wait()
            return carry

        lax.fori_loop(0, n_zero, start, 0)
        lax.fori_loop(0, n_zero, wait, 0)

    pl.when(pl.program_id(0) == 0)(zero_unowned_rows)

    def scatter_tile(x_ref):
        for t in range(tm):
            for k in range(TOP_K):
                _tile_copy(x_ref, t, o_hbm, dest_ref[0, k, t], sem).start(priority=k % 2)
        for k in range(TOP_K):
            pltpu.make_async_copy(x_ref, o_hbm.at[pl.ds(0, tm * ROW_TILE)], sem).wait()

    pl.when(pl.program_id(0) < n_first)(functools.partial(scatter_tile, xa_ref))
    pl.when(pl.program_id(0) >= n_first)(functools.partial(scatter_tile, xb_ref))


def _dispatch(zrows, dest_blk, x1p_a, x1p_b, *, n_rows, tm, block_rows):
    nf = x1p_a.shape[0] // ROW_TILE // tm
    ns = x1p_b.shape[0] // ROW_TILE // tm
    grid_spec = pltpu.PrefetchScalarGridSpec(
        num_scalar_prefetch=1,
        grid=(nf + ns,),
        in_specs=[
            pl.BlockSpec((1, TOP_K, tm), lambda i, z: (i, 0, 0), memory_space=pltpu.SMEM),
            pl.BlockSpec((tm * ROW_TILE, PACKED_LANES), lambda i, z: (jnp.minimum(i, nf - 1), 0)),
            pl.BlockSpec((tm * ROW_TILE, PACKED_LANES), lambda i, z: (jnp.clip(i - nf, 0, ns - 1), 0)),
        ],
        out_specs=pl.BlockSpec(memory_space=pl.ANY),
        scratch_shapes=[pltpu.VMEM((block_rows * ROW_TILE, PACKED_LANES), jnp.uint32),
                        pltpu.SemaphoreType.DMA(()), pltpu.SemaphoreType.DMA(())],
    )
    return pl.pallas_call(
        functools.partial(_dispatch_kernel, block_rows=block_rows, n_first=nf),
        grid_spec=grid_spec,
        out_shape=jax.ShapeDtypeStruct((n_rows * ROW_TILE, PACKED_LANES), jnp.uint32),
        compiler_params=_cparams(("arbitrary",)),
        name="dispatch",
    )(zrows, dest_blk, x1p_a, x1p_b)


def _moe_kernel(be_ref, nv_ref, nr_ref, nxt_ref, slot_ref, x_ref, wg_hbm, wu_hbm, wd_hbm, o_ref,
                wg_f, wu_f, wd_f, wg_b, wu_b, wd_b, wsem):
    k = pl.program_id(0)
    m = x_ref.shape[0] // ROW_TILE
    live = k < nv_ref[0]
    new_expert = jnp.logical_or(k == 0, be_ref[k] != be_ref[jnp.maximum(k - 1, 0)])

    def weight_copies(e, slot):
        return [pltpu.make_async_copy(src.at[e], dst.at[slot], wsem.at[slot])
                for src, dst in ((wg_hbm, wg_f), (wu_hbm, wu_f), (wd_hbm, wd_f))]

    @pl.when(jnp.logical_and(live, k == 0))
    def _():
        for cp in weight_copies(be_ref[0], 0):
            cp.start()

    @pl.when(jnp.logical_and(live, new_expert))
    def _():
        slot = slot_ref[k]
        for cp in weight_copies(be_ref[k], slot):
            cp.wait()
        wg_b[...] = wg_f[slot].astype(BF16)
        wu_b[...] = wu_f[slot].astype(BF16)
        wd_b[...] = wd_f[slot].astype(BF16)

        @pl.when(nxt_ref[k] >= 0)
        def _():
            for cp in weight_copies(nxt_ref[k], 1 - slot):
                cp.start()

    def swiglu_rows(rows):
        lo, hi = _unpack_halves(jnp.concatenate(_load_row_tiles(x_ref, rows), axis=1))
        x = jnp.concatenate([lo.astype(BF16), hi.astype(BF16)], axis=1)
        g = jnp.dot(x, wg_b[...], preferred_element_type=F32)
        u = jnp.dot(x, wu_b[...], preferred_element_type=F32)
        hmid = (g * _sigmoid(g) * u).astype(BF16)
        y = jnp.dot(hmid, wd_b[...], preferred_element_type=F32)
        _store_row_tiles(o_ref, _pack_halves(y))

    quarter = m // 4
    quarters_used = (nr_ref[k] + quarter - 1) // quarter
    for n_q in range(1, 5):
        @pl.when(jnp.logical_and(live, quarters_used == n_q))
        def _(n_q=n_q):
            swiglu_rows(n_q * quarter)
            if n_q < 4:
                rest = (m - n_q * quarter) * ROW_TILE
                o_ref[n_q * quarter * ROW_TILE:, :] = jnp.zeros((rest, o_ref.shape[1]), o_ref.dtype)

    @pl.when(jnp.logical_not(live))
    def _():
        o_ref[...] = jnp.zeros_like(o_ref)


def _moe(xs, blk_e, n_valid, blk_rows, nxt_e, slot, w_gate, w_up, w_down):
    n_blocks = blk_e.shape[0]
    m = MOE_BLOCK

    def rows(k, be, nv, nr, nx, sl):
        return (jnp.minimum(k, nv[0] - 1), 0)

    hbm = pl.BlockSpec(memory_space=pl.ANY)
    grid_spec = pltpu.PrefetchScalarGridSpec(
        num_scalar_prefetch=5,
        grid=(n_blocks,),
        in_specs=[pl.BlockSpec((m * ROW_TILE, PACKED_LANES), rows), hbm, hbm, hbm],
        out_specs=pl.BlockSpec((m * ROW_TILE, PACKED_LANES), lambda k, be, nv, nr, nx, sl: (k, 0)),
        scratch_shapes=[pltpu.VMEM((2, D_MODEL, EXPERT_DIM), F32), pltpu.VMEM((2, D_MODEL, EXPERT_DIM), F32),
                        pltpu.VMEM((2, EXPERT_DIM, D_MODEL), F32),
                        pltpu.VMEM((D_MODEL, EXPERT_DIM), BF16), pltpu.VMEM((D_MODEL, EXPERT_DIM), BF16),
                        pltpu.VMEM((EXPERT_DIM, D_MODEL), BF16), pltpu.SemaphoreType.DMA((2,))],
    )
    return pl.pallas_call(
        _moe_kernel,
        grid_spec=grid_spec,
        out_shape=jax.ShapeDtypeStruct(xs.shape, jnp.uint32),
        compiler_params=_cparams(("arbitrary",)),
        name="moe_experts",
    )(blk_e, n_valid, blk_rows, nxt_e, slot, xs, w_gate, w_up, w_down)


def _final_kernel(dest_ref, dest_next_ref, x1_ref, w_ref, yb_hbm, sg_ref, su_ref, sd_ref, g_ref, b_ref, o_ref,
                  ybuf, sem):
    i = pl.program_id(0)
    tm = x1_ref.shape[0]

    def start(d_ref, slot, t):
        for k in range(TOP_K):
            _tile_copy(yb_hbm, d_ref[0, k, t], ybuf.at[slot, k], t, sem.at[slot]).start(priority=k % 2)

    def wait_all(slot):
        for k in range(TOP_K):
            pltpu.make_async_copy(yb_hbm.at[pl.ds(0, tm * ROW_TILE)], ybuf.at[slot, k], sem.at[slot]).wait()

    @pl.when(i == 0)
    def _():
        def issue(t, carry):
            start(dest_ref, 0, t)
            return carry

        lax.fori_loop(0, tm, issue, 0, unroll=4)

    cur = i % 2
    nxt = 1 - cur
    wait_all(cur)
    x1 = x1_ref[...]
    xb = x1.astype(BF16)
    g = jnp.dot(xb, sg_ref[...], preferred_element_type=F32)
    u = jnp.dot(xb, su_ref[...], preferred_element_type=F32)
    w = w_ref[...]
    per = tm // TOP_K
    r_lo = [None] * ROW_TILE
    r_hi = [None] * ROW_TILE
    for k in range(TOP_K):
        for t in range(k * per, (k + 1) * per):
            start(dest_next_ref, nxt, t)
        wk = jnp.broadcast_to(w[:, k:k + 1], (tm, PACKED_LANES))
        for s, piece in enumerate(_load_row_tiles(ybuf.at[cur, k], tm)):
            lo, hi = _unpack_halves(piece)
            r_lo[s] = wk * lo if r_lo[s] is None else r_lo[s] + wk * lo
            r_hi[s] = wk * hi if r_hi[s] is None else r_hi[s] + wk * hi
    y = jnp.dot((g * _sigmoid(g) * u).astype(BF16), sd_ref[...], preferred_element_type=F32)
    routed = jnp.concatenate(r_lo + r_hi, axis=1)
    o_ref[...] = _layer_norm(DN_ALPHA * x1 + (routed + y), g_ref[...], b_ref[...])

    @pl.when(i == pl.num_programs(0) - 1)
    def _():
        wait_all(nxt)


def _final(dest, x1, wts_t, yb, w_sg, w_su, w_sd, ln_g, ln_b, *, tm, name):
    n_tiles = x1.shape[0] // tm

    def tile(i):
        return (i, 0)

    def const(i):
        return (0, 0)

    smem = functools.partial(pl.BlockSpec, memory_space=pltpu.SMEM)
    return pl.pallas_call(
        _final_kernel,
        grid=(n_tiles,),
        in_specs=[
            smem((1, TOP_K, tm), lambda i: (i, 0, 0)),
            smem((1, TOP_K, tm), lambda i: (jnp.minimum(i + 1, n_tiles - 1), 0, 0)),
            pl.BlockSpec((tm, D_MODEL), tile),
            pl.BlockSpec((tm, TOP_K), tile),
            pl.BlockSpec(memory_space=pl.ANY),
            _resident((D_MODEL, EXPERT_DIM), const),
            _resident((D_MODEL, EXPERT_DIM), const),
            _resident((EXPERT_DIM, D_MODEL), const),
            pl.BlockSpec((1, D_MODEL), const),
            pl.BlockSpec((1, D_MODEL), const),
        ],
        out_specs=pl.BlockSpec((tm, D_MODEL), tile),
        out_shape=jax.ShapeDtypeStruct((n_tiles * tm, D_MODEL), F32),
        scratch_shapes=[pltpu.VMEM((2, TOP_K, tm * ROW_TILE, PACKED_LANES), jnp.uint32),
                        pltpu.SemaphoreType.DMA((2,))],
        compiler_params=_cparams(("arbitrary",)),
        name=name,
    )(dest, dest, x1, wts_t, yb, w_sg, w_su, w_sd, ln_g, ln_b)


def kernel(x_prompt, x_sample, cache_win_k, cache_win_v, state_hgrn, cache_mem_k, cache_mem_v, mem_prompt,
           w_in, w_mem_kv, a_sinks, b_lb_logits, b_norm_g, w_branch, w_out, ln1_g, ln1_b,
           w_router, router_bias, w_exp_gate, w_exp_up, w_exp_down, w_sh_gate, w_sh_up, w_sh_down, ln2_g, ln2_b):
    assert w_in.shape[0] == DEPTH == 1
    batch, seq, _ = x_prompt.shape
    dbatch, dseq, _ = x_sample.shape
    n_p = batch * seq
    n_s = dbatch * dseq
    n = n_p + n_s
    l = 0

    xp2 = x_prompt.reshape(n_p, D_MODEL)
    xs2 = x_sample.reshape(n_s, D_MODEL)
    win = w_in[l]
    o_k, o_v, o_b = A_WIDTH, A_WIDTH + A_KV_WIDTH, A_WIDTH + 2 * A_KV_WIDTH
    o_c = o_b + 4 * B_WIDTH
    o_g = o_c + C_WIDTH
    groups = [
        ("aq", 0, o_k, 1024, BF16, A_HEAD_DIM ** -0.5),
        ("akv", o_k, o_b, 2 * A_KV_WIDTH, F32, 1.0),
        ("hgrn", o_b, o_c, 1024, F32, 1.0),
        ("cq", o_c, o_g, 1024, BF16, C_HEAD_DIM ** -0.5),
        ("gate", o_g, win.shape[1], 1024, F32, 1.0),
    ]
    zp, zs = {}, {}
    for gname, c0, c1, tn, dt, scale in groups:
        wslice = win[:, c0:c1].astype(BF16)
        zp[gname] = _matmul(xp2, wslice, tm=1024, tn=tn, out_dtype=dt, scale=scale, name=f"proj_{gname}_p")
        zs[gname] = _matmul(xs2, wslice, tm=n_s, tn=tn, out_dtype=dt, scale=scale, name=f"proj_{gname}_s")

    a_p = _swa_prompt(zp["aq"], zp["akv"], a_sinks[l], batch=batch, seq=seq)
    lc = cache_win_k.shape[2]
    kc = cache_win_k[l].reshape(dbatch, lc, A_KV_WIDTH)
    vc = cache_win_v[l].reshape(dbatch, lc, A_KV_WIDTH)
    a_s = _swa_sample(zs["aq"], zs["akv"], kc, vc, a_sinks[l], row0=0, batch=dbatch, seq=dseq)

    lower = jnp.cumsum(jax.nn.softmax(b_lb_logits.astype(F32), axis=0), axis=0)[l]
    lbp = jnp.stack([jnp.log(lower), jnp.log1p(-lower), 1.0 - lower])
    ng = jnp.tile(b_norm_g[l].astype(F32), B_HEADS).reshape(1, B_WIDTH)
    s_zero = jnp.zeros((batch, B_HEADS, B_KEY_DIM, B_VAL_DIM), F32)
    b_p, hs_p = _hgrn(zp["hgrn"], lbp, ng, s_zero, batch=batch, seq=seq, blk=CHUNK, nbat=2 if batch % 2 == 0 else 1,
                      ntime=4 if seq % (4 * CHUNK) == 0 else 1)
    b_s, hs_s = _hgrn(zs["hgrn"], lbp, ng, state_hgrn[l].astype(F32), batch=dbatch, seq=dseq, blk=dseq,
                      nbat=2 if dbatch % 2 == 0 else 1, ntime=1)

    mem = mem_prompt.reshape(batch * MEM_LEN, D_MODEL)
    wmem = w_mem_kv[l]
    mk = _matmul(mem, wmem[:, :C_WIDTH].astype(BF16), tm=batch * MEM_LEN, tn=512, out_dtype=F32, name="proj_mem_k")
    mv = _matmul(mem, wmem[:, C_WIDTH:].astype(BF16), tm=batch * MEM_LEN, tn=512, out_dtype=F32, name="proj_mem_v")
    mk = mk.reshape(batch, MEM_LEN, C_WIDTH)
    mv = mv.reshape(batch, MEM_LEN, C_WIDTH)
    c_p = _cattn(zp["cq"], mk, mv, row0=0, batch=batch, seq=seq, tq=512)
    c_s = _cattn(zs["cq"], cache_mem_k[l].reshape(dbatch, MEM_LEN, C_WIDTH),
                 cache_mem_v[l].reshape(dbatch, MEM_LEN, C_WIDTH), row0=0, batch=dbatch, seq=dseq, tq=dseq)

    w_br = w_branch[l].astype(BF16)
    route_w = (w_out[l].astype(BF16), ln1_g[l].reshape(1, D_MODEL), ln1_b[l].reshape(1, D_MODEL),
               w_router[l].T.astype(BF16), router_bias[l].reshape(N_EXPERTS, 1).astype(F32))
    segs = []
    cnt = jnp.zeros((N_EXPERTS, 128), jnp.int32)
    for tag, x2, abc, gate, tm_r in (("p", xp2, (a_p, b_p, c_p), zp["gate"], 512), ("s", xs2, (a_s, b_s, c_s), zs["gate"], n_s)):
        h = _merge(*abc, gate, w_br, tm=tm_r, name=f"merge_{tag}")
        x1, x1p, idx, wts, pos, cnt = _out_ln_route(x2, h, *route_w, cnt, tm=tm_r, name=f"out_ln_route_{tag}")
        segs.append(dict(tag=tag, x1=x1, x1p=x1p, idx=idx, wts=wts, pos=pos, tm=tm_r, rows=x2.shape[0]))

    m = MOE_BLOCK
    n_pairs = n * TOP_K
    n_blocks = (n_pairs + m - 1) // m + N_EXPERTS
    counts = cnt[:, 0]
    padded = (counts + m - 1) // m * m
    pad_end = jnp.cumsum(padded)
    pad_start = (pad_end - padded).astype(jnp.int32)
    n_valid = pad_end[-1:] // m
    e_ids = jnp.arange(N_EXPERTS, dtype=jnp.int32)
    blk_first = jnp.arange(n_blocks, dtype=jnp.int32) * m
    blk_e = jnp.sum((blk_first[:, None] >= pad_end[None, :]).astype(jnp.int32), axis=1)
    blk_e = jnp.minimum(blk_e, N_EXPERTS - 1)
    row_end = jnp.sum(jnp.where(blk_e[:, None] == e_ids, (pad_start + counts.astype(jnp.int32))[None, :], 0), axis=1)
    blk_rows = jnp.clip(row_end - blk_first, 0, m).astype(jnp.int32)
    owns = counts > 0
    later = (e_ids[None, :] > e_ids[:, None]) & owns[None, :]
    next_owner = jnp.min(jnp.where(later, e_ids[None, :], N_EXPERTS), axis=1)
    next_owner = jnp.where(next_owner < N_EXPERTS, next_owner, -1)
    owner_rank = jnp.cumsum(owns.astype(jnp.int32)) - 1
    nxt_e = jnp.sum(jnp.where(blk_e[:, None] == e_ids, next_owner[None, :], 0), axis=1).astype(jnp.int32)
    slot = (jnp.sum(jnp.where(blk_e[:, None] == e_ids, owner_rank[None, :], 0), axis=1) % 2).astype(jnp.int32)
    z_pad = jnp.where(padded > counts, pad_end - m, -1)
    z_tail = jnp.where(blk_first >= pad_end[-1], blk_first, -1)
    zrows = jnp.concatenate([z_pad, z_tail]).astype(jnp.int32)

    def retile(dest, rows, tm_from, tm_to):
        d = dest.reshape(rows // tm_from, TOP_K, tm_from // tm_to, tm_to)
        return jnp.transpose(d, (0, 2, 1, 3)).reshape(rows // tm_to, TOP_K, tm_to)

    tm_d = 256
    for seg in segs:
        first_row = jnp.sum(jnp.where(seg["idx"][..., None] == e_ids, pad_start, 0), axis=-1)
        seg["dest"] = first_row + seg["pos"]
    dest_d = jnp.concatenate([retile(seg["dest"], seg["rows"], seg["tm"], tm_d) for seg in segs], axis=0)
    xs = _dispatch(zrows, dest_d, segs[0]["x1p"], segs[1]["x1p"], n_rows=n_blocks * m, tm=tm_d, block_rows=m)
    yb = _moe(xs, blk_e, n_valid.astype(jnp.int32), blk_rows, nxt_e, slot, w_exp_gate[l].astype(F32), w_exp_up[l].astype(F32),
              w_exp_down[l].astype(F32))
    tm_f = 256
    shared_w = (w_sh_gate[l].astype(BF16), w_sh_up[l].astype(BF16), w_sh_down[l].astype(BF16),
                ln2_g[l].reshape(1, D_MODEL), ln2_b[l].reshape(1, D_MODEL))
    ys = []
    for seg in segs:
        rows = seg["rows"]
        wts_t = jnp.transpose(seg["wts"], (0, 2, 1)).reshape(rows, TOP_K)
        ys.append(_final(retile(seg["dest"], rows, seg["tm"], tm_f), seg["x1"], wts_t, yb, *shared_w, tm=tm_f,
                         name=f"combine_shared_ln2_{seg['tag']}"))
    y_p = ys[0].reshape(batch, seq, D_MODEL)
    y_s = ys[1].reshape(dbatch, dseq, D_MODEL)

    kv_p = zp["akv"].reshape(batch, seq, 2, A_KV_HEADS, A_HEAD_DIM)[:, -lc:]
    k_p, v_p = kv_p[:, :, 0], kv_p[:, :, 1]
    kv_s = zs["akv"].reshape(dbatch, dseq, 2, A_KV_HEADS, A_HEAD_DIM)
    k_s, v_s = kv_s[:, :, 0], kv_s[:, :, 1]
    wk_s = jnp.concatenate([cache_win_k[l].astype(F32), k_s], axis=1)[:, -lc:]
    wv_s = jnp.concatenate([cache_win_v[l].astype(F32), v_s], axis=1)[:, -lc:]
    mk_o = mk.reshape(batch, MEM_LEN, C_HEADS, C_HEAD_DIM)
    mv_o = mv.reshape(batch, MEM_LEN, C_HEADS, C_HEAD_DIM)
    return (y_p, y_s, k_p[None], v_p[None], hs_p[None], mk_o[None], mv_o[None], wk_s[None], wv_s[None], hs_s[None])
```
